```python
import numpy as np
import jax
import jax.numpy as jnp
from jax import lax

D_MODEL = 1024
BATCH = 2
SEQ = 8192
DEPTH = 2
DEC_BATCH = 128
DEC_SEQ = 4
PAST_LEN = 2048
PAGE_SIZE = 128

N_MIXERS = 2
N_POOL_LAYERS = (DEPTH + 1) // 2
N_ATTN_LAYERS = DEPTH // 2

POOL_WINDOWS = (2, 4, 8, 16)
N_POOL_GROUPS = 4
POOL_GROUP = D_MODEL // N_POOL_GROUPS
POOL_MAX_WIN = 16
POOL_BUF = POOL_MAX_WIN - 1

HEAD_DIM = 64
N_HEADS = D_MODEL // HEAD_DIM
ATT_WIDTH = N_HEADS * HEAD_DIM
DIL_GROUPS = ((128, 1), (512, 4), (2048, 16))
N_DIL = 3
ROPE_THETA = 10000.0

N_EXPERT_GROUPS = 4
EXPERTS_PER_GROUP = 8
TOP_K = 2
EXPERT_FF = D_MODEL // 4

ALPHA = (2.0 * DEPTH) ** 0.25
BETA = (8.0 * DEPTH) ** -0.25
LN_EPS = 1e-5

kernel_name = 'hybrid_pool_dilated_attn_hmoe_step'


def layer_norm(x, g, b):
    xf = x.astype(jnp.float32)
    mu = jnp.mean(xf, axis=-1, keepdims=True)
    var = jnp.mean(jnp.square(xf - mu), axis=-1, keepdims=True)
    return ((xf - mu) * lax.rsqrt(var + LN_EPS) * g + b).astype(x.dtype)


def modulation(c, w, b):
    m = c @ w + b
    return [t[:, None, :] for t in jnp.split(m, 6, axis=-1)]


def rope(x, pos):
    half = HEAD_DIM // 2
    inv = ROPE_THETA ** (-jnp.arange(half, dtype=jnp.float32) / half)
    ang = pos.astype(jnp.float32)[:, None] * inv[None, :]
    cos = jnp.cos(ang)[:, None, :]
    sin = jnp.sin(ang)[:, None, :]
    xf = x.astype(jnp.float32)
    x1, x2 = xf[..., :half], xf[..., half:]
    return jnp.concatenate([x1 * cos - x2 * sin, x2 * cos + x1 * sin], axis=-1).astype(x.dtype)


def pool_mix(z, n_out, w_pool, scale):
    B, L, _ = z.shape
    zf = z.astype(jnp.float32)
    cs = jnp.pad(jnp.cumsum(zf, axis=1), ((0, 0), (POOL_MAX_WIN, 0), (0, 0)))
    r0 = L - n_out
    diffs = []
    for g, w in enumerate(POOL_WINDOWS):
        sl = slice(g * POOL_GROUP, (g + 1) * POOL_GROUP)
        win_sum = cs[:, POOL_MAX_WIN + r0:, sl] - cs[:, POOL_MAX_WIN - w + r0:POOL_MAX_WIN - w + L, sl]
        cnt = np.minimum(np.arange(r0, L) + 1, w).astype(np.float32)
        diffs.append(win_sum / cnt[None, :, None] - zf[:, r0:, sl])
    d = jnp.stack(diffs, axis=2).astype(z.dtype)
    y = jnp.einsum('bsgc,gce->bsge', d, w_pool).reshape(B, n_out, D_MODEL)
    return y * scale


def dilated_attn_prompt(q, k, v, window, dil):
    B, S, H, E = q.shape
    n_back = window // dil
    blk = n_back
    L = S // dil
    nb = -(-L // blk)
    Lp = nb * blk

    def to_sub(t):
        return t.reshape(B, L, dil, H, E).transpose(0, 2, 1, 3, 4)

    qs = jnp.pad(to_sub(q), ((0, 0), (0, 0), (0, Lp - L), (0, 0), (0, 0))).reshape(B, dil, nb, blk, H, E)

    def key_windows(t):
        tp = jnp.pad(to_sub(t), ((0, 0), (0, 0), (blk, Lp - L), (0, 0), (0, 0))).reshape(B, dil, nb + 1, blk, H, E)
        return jnp.concatenate([tp[:, :, :-1], tp[:, :, 1:]], axis=3)

    kw = key_windows(k)
    vw = key_windows(v)
    s = jnp.einsum('brnqhe,brnkhe->brnhqk', qs, kw, preferred_element_type=jnp.float32) * (E ** -0.5)
    dist = np.arange(blk)[:, None] + blk - np.arange(2 * blk)[None, :]
    band = (dist >= 0) & (dist <= n_back)
    kvalid = (np.arange(nb)[:, None] * blk + np.arange(2 * blk)[None, :] - blk) >= 0
    mask = band[None, :, :] & kvalid[:, None, :]
    s = jnp.where(mask[None, None, :, None, :, :], s, -jnp.inf)
    lse = jax.nn.logsumexp(s, axis=-1)
    p = jnp.exp(s - lse[..., None])
    o = jnp.einsum('brnhqk,brnkhe->brnqhe', p, vw.astype(jnp.float32))
    o = o.reshape(B, dil, Lp, H, E)[:, :, :L].transpose(0, 2, 1, 3, 4).reshape(B, S, H, E)
    lse = lse.transpose(0, 1, 2, 4, 3).reshape(B, dil, Lp, H)[:, :, :L].transpose(0, 2, 1, 3).reshape(B, S, H)
    return o, lse


def dilated_attn_sample(q, k_all, v_all, n_buf, window, dil):
    T = q.shape[1]
    E = q.shape[-1]
    n_back = window // dil
    idx = n_buf + np.arange(T)[:, None] - dil * np.arange(n_back + 1)[None, :]
    valid = idx >= 0
    idx = np.maximum(idx, 0)
    kg = k_all[:, idx]
    vg = v_all[:, idx]
    s = jnp.einsum('bthe,btmhe->bthm', q, kg, preferred_element_type=jnp.float32) * (E ** -0.5)
    s = jnp.where(valid[None, :, None, :], s, -jnp.inf)
    lse = jax.nn.logsumexp(s, axis=-1)
    p = jnp.exp(s - lse[..., None])
    o = jnp.einsum('bthm,btmhe->bthe', p, vg.astype(jnp.float32))
    return o, lse


def attn_mixer(h, pos, w_qkv, w_o, caches):
    B, S, _ = h.shape
    qkv = (h @ w_qkv).reshape(B, S, N_DIL, 3, N_HEADS, HEAD_DIM)
    outs, lses, rows = [], [], []
    for g, (win, dil) in enumerate(DIL_GROUPS):
        q = rope(qkv[:, :, g, 0], pos)
        k = rope(qkv[:, :, g, 1], pos)
        v = qkv[:, :, g, 2]
        if caches is None:
            o, l = dilated_attn_prompt(q, k, v, win, dil)
            keep = min(win, S)
            rows.append(k[:, S - keep:])
            rows.append(v[:, S - keep:])
        else:
            kc, vc = caches[g]
            n_buf = kc.shape[1]
            k_all = jnp.concatenate([kc.astype(k.dtype), k], axis=1)
            v_all = jnp.concatenate([vc.astype(v.dtype), v], axis=1)
            o, l = dilated_attn_sample(q, k_all, v_all, n_buf, win, dil)
            rows.append(k)
            rows.append(v)
        outs.append(o)
        lses.append(l)
    wts = jax.nn.softmax(jnp.stack(lses, axis=0), axis=0)
    o = jnp.einsum('gbsh,gbshe->bshe', wts, jnp.stack(outs, axis=0))
    y = o.reshape(B, S, ATT_WIDTH).astype(h.dtype) @ w_o
    return y, rows


def hier_moe(h, w_group, b_group, w_er, b_er, w1, w3, w2):
    B, S, D = h.shape
    t = h.reshape(-1, D)
    g_logits = (t @ w_group + b_group).astype(jnp.float32)
    g_prob = jax.nn.softmax(g_logits, axis=-1)
    g_idx = jnp.argmax(g_logits, axis=-1)
    g_p = jnp.take_along_axis(g_prob, g_idx[:, None], axis=1)[:, 0]
    e_all = (jnp.einsum('nd,gde->nge', t, w_er) + b_er).astype(jnp.float32)
    e_logits = jnp.take_along_axis(e_all, g_idx[:, None, None], axis=1)[:, 0]
    top_v, top_i = lax.top_k(e_logits, TOP_K)
    top_w = jax.nn.softmax(top_v, axis=-1) * g_p[:, None]
    e_w = jnp.sum(top_w[..., None] * jax.nn.one_hot(top_i, EXPERTS_PER_GROUP, dtype=jnp.float32), axis=1)
    comb = jax.nn.one_hot(g_idx, N_EXPERT_GROUPS, dtype=jnp.float32)[:, :, None] * e_w[:, None, :]
    out = jnp.zeros(t.shape, jnp.float32)
    for g in range(N_EXPERT_GROUPS):
        a = jnp.einsum('nd,edf->nef', t, w1[g])
        b = jnp.einsum('nd,edf->nef', t, w3[g])
        hid = (jax.nn.silu(a) * b * comb[:, g, :, None]).astype(h.dtype)
        out = out + jnp.einsum('nef,efd->nd', hid, w2[g], preferred_element_type=jnp.float32)
    return out.astype(h.dtype).reshape(B, S, D)


def setup_inputs(seed: int = 0) -> dict:
    key = jax.random.key(seed)
    ks = jax.random.split(key, 32)
    nrm = jax.random.normal
    f32 = jnp.float32
    D = D_MODEL
    G, E, F = N_EXPERT_GROUPS, EXPERTS_PER_GROUP, EXPERT_FF
    inp = {}
    inp['x_prompt'] = nrm(ks[0], (BATCH, SEQ, D), f32)
    inp['x_sample'] = nrm(ks[1], (DEC_BATCH, DEC_SEQ, D), f32)
    inp['state_pool'] = nrm(ks[2], (N_POOL_LAYERS, DEC_BATCH, POOL_BUF, D), f32)
    for j, (win, _) in enumerate(DIL_GROUPS):
        n_buf = min(win, PAST_LEN)
        shp = (N_ATTN_LAYERS, DEC_BATCH, n_buf, N_HEADS, HEAD_DIM)
        inp['cache_k_w' + str(win)] = nrm(ks[3 + 2 * j], shp, f32)
        inp['cache_v_w' + str(win)] = nrm(ks[4 + 2 * j], shp, f32)
    inp['c_prompt'] = nrm(ks[9], (BATCH, D), f32)
    inp['c_sample'] = nrm(ks[10], (DEC_BATCH, D), f32)
    inp['mod_w'] = nrm(ks[11], (DEPTH, D, 6 * D), f32) * (0.5 * D ** -0.5)
    inp['mod_b'] = nrm(ks[12], (DEPTH, 6 * D), f32) * 0.02
    inp['ln1_g'] = 1.0 + 0.1 * nrm(ks[13], (DEPTH, D), f32)
    inp['ln1_b'] = 0.02 * nrm(ks[14], (DEPTH, D), f32)
    inp['ln2_g'] = 1.0 + 0.1 * nrm(ks[15], (DEPTH, D), f32)
    inp['ln2_b'] = 0.02 * nrm(ks[16], (DEPTH, D), f32)
    inp['pool_w'] = nrm(ks[17], (N_POOL_LAYERS, N_POOL_GROUPS, POOL_GROUP, POOL_GROUP), f32) * (POOL_GROUP ** -0.5 * BETA)
    inp['pool_scale'] = 1.0 + 0.1 * nrm(ks[18], (N_POOL_LAYERS, D), f32)
    inp['attn_w_qkv'] = nrm(ks[19], (N_ATTN_LAYERS, D, N_DIL * 3 * ATT_WIDTH), f32) * (D ** -0.5)
    inp['attn_w_o'] = nrm(ks[20], (N_ATTN_LAYERS, ATT_WIDTH, D), f32) * (ATT_WIDTH ** -0.5 * BETA)
    inp['moe_w_group'] = nrm(ks[21], (DEPTH, D, G), f32) * (D ** -0.5)
    inp['moe_b_group'] = 0.01 * nrm(ks[22], (DEPTH, G), f32)
    inp['moe_w_expert_router'] = nrm(ks[23], (DEPTH, G, D, E), f32) * (D ** -0.5)
    inp['moe_b_expert_router'] = 0.01 * nrm(ks[24], (DEPTH, G, E), f32)
    inp['moe_w1'] = nrm(ks[25], (DEPTH, G, E, D, F), f32) * (D ** -0.5)
    inp['moe_w3'] = nrm(ks[26], (DEPTH, G, E, D, F), f32) * (D ** -0.5)
    inp['moe_w2'] = nrm(ks[27], (DEPTH, G, E, F, D), f32) * (F ** -0.5 * BETA)
    return inp


def reference(x_prompt, x_sample, state_pool, cache_k_w128, cache_v_w128, cache_k_w512, cache_v_w512,
              cache_k_w2048, cache_v_w2048, c_prompt, c_sample, mod_w, mod_b, ln1_g, ln1_b, ln2_g, ln2_b,
              pool_w, pool_scale, attn_w_qkv, attn_w_o, moe_w_group, moe_b_group, moe_w_expert_router,
              moe_b_expert_router, moe_w1, moe_w3, moe_w2):
    kv_caches = ((cache_k_w128, cache_v_w128), (cache_k_w512, cache_v_w512), (cache_k_w2048, cache_v_w2048))
    S = x_prompt.shape[1]
    T = x_sample.shape[1]
    pos_p = jnp.arange(S, dtype=jnp.int32)
    pos_s = PAST_LEN + jnp.arange(T, dtype=jnp.int32)
    xp, xs = x_prompt, x_sample
    pool_new_p, pool_new_s = [], []
    kv_new_p = [[] for _ in range(2 * N_DIL)]
    kv_new_s = [[] for _ in range(2 * N_DIL)]
    for i in range(DEPTH):
        li = i // N_MIXERS
        mp = modulation(c_prompt, mod_w[i], mod_b[i])
        ms = modulation(c_sample, mod_w[i], mod_b[i])
        hp = xp * (1.0 + mp[1]) + mp[0]
        hs = xs * (1.0 + ms[1]) + ms[0]
        if i % N_MIXERS == 0:
            yp = pool_mix(hp, S, pool_w[li], pool_scale[li])
            ys = pool_mix(jnp.concatenate([state_pool[li].astype(hs.dtype), hs], axis=1), T, pool_w[li], pool_scale[li])
            pool_new_p.append(hp[:, S - POOL_BUF:])
            pool_new_s.append(hs)
        else:
            yp, rows_p = attn_mixer(hp, pos_p, attn_w_qkv[li], attn_w_o[li], None)
            layer_caches = tuple((kc[li], vc[li]) for kc, vc in kv_caches)
            ys, rows_s = attn_mixer(hs, pos_s, attn_w_qkv[li], attn_w_o[li], layer_caches)
            for j in range(2 * N_DIL):
                kv_new_p[j].append(rows_p[j])
                kv_new_s[j].append(rows_s[j])
        xp = layer_norm(ALPHA * xp + mp[2] * yp, ln1_g[i], ln1_b[i])
        xs = layer_norm(ALPHA * xs + ms[2] * ys, ln1_g[i], ln1_b[i])
        hp = xp * (1.0 + mp[4]) + mp[3]
        hs = xs * (1.0 + ms[4]) + ms[3]
        fp = hier_moe(hp, moe_w_group[i], moe_b_group[i], moe_w_expert_router[i], moe_b_expert_router[i], moe_w1[i], moe_w3[i], moe_w2[i])
        fs = hier_moe(hs, moe_w_group[i], moe_b_group[i], moe_w_expert_router[i], moe_b_expert_router[i], moe_w1[i], moe_w3[i], moe_w2[i])
        xp = layer_norm(ALPHA * xp + mp[5] * fp, ln2_g[i], ln2_b[i])
        xs = layer_norm(ALPHA * xs + ms[5] * fs, ln2_g[i], ln2_b[i])
    pool_p = jnp.stack(pool_new_p, axis=0)
    pool_s = jnp.stack(pool_new_s, axis=0)
    k128_p = jnp.stack(kv_new_p[0], axis=0)
    v128_p = jnp.stack(kv_new_p[1], axis=0)
    k512_p = jnp.stack(kv_new_p[2], axis=0)
    v512_p = jnp.stack(kv_new_p[3], axis=0)
    k2048_p = jnp.stack(kv_new_p[4], axis=0)
    v2048_p = jnp.stack(kv_new_p[5], axis=0)
    k128_s = jnp.stack(kv_new_s[0], axis=0)
    v128_s = jnp.stack(kv_new_s[1], axis=0)
    k512_s = jnp.stack(kv_new_s[2], axis=0)
    v512_s = jnp.stack(kv_new_s[3], axis=0)
    k2048_s = jnp.stack(kv_new_s[4], axis=0)
    v2048_s = jnp.stack(kv_new_s[5], axis=0)
    return (xp, xs, pool_p, pool_s, k128_p, v128_p, k512_p, v512_p, k2048_p, v2048_p,
            k128_s, v128_s, k512_s, v512_s, k2048_s, v2048_s)
```

```python
import functools

import numpy as np
import jax
import jax.numpy as jnp
from jax import lax
from jax.experimental import pallas as pl
from jax.experimental.pallas import tpu as pltpu

F32 = jnp.float32
BF16 = jnp.bfloat16
HIGHEST = lax.Precision.HIGHEST

D_MODEL = 1024
POOL_WINDOWS = (2, 4, 8, 16)
N_POOL_GROUPS = len(POOL_WINDOWS)
POOL_GROUP = D_MODEL // N_POOL_GROUPS
POOL_BUF = max(POOL_WINDOWS) - 1
POOL_HALO = 16
HEAD_DIM = 64
N_HEADS = D_MODEL // HEAD_DIM
DIL_GROUPS = ((128, 1), (512, 4), (2048, 16))
N_DIL = len(DIL_GROUPS)
ATT_BAND = 128
ROPE_THETA = 10000.0
N_EXPERT_GROUPS = 4
EXPERTS_PER_GROUP = 8
PAST_LEN = 2048
LN_EPS = 1e-5

LANES = 128
SUBLANES = 8
VMEM_LIMIT_BYTES = 56 * 1024 * 1024

EXT = LANES
EXT_EXPERT0 = N_EXPERT_GROUPS
EXT_ONEHOT0 = 120
ROW_W = D_MODEL + EXT

MOE_TILE = 512
SORT_CHUNK = 512
DMA_CHUNK = 64
DMA_LAG = 4

_NT = (((1,), (1,)), ((), ()))


def _cparams(*sem):
    return pltpu.CompilerParams(dimension_semantics=sem, vmem_limit_bytes=VMEM_LIMIT_BYTES)


def _iota(shape, dim):
    return lax.broadcasted_iota(jnp.int32, shape, dim)


def _mod_kernel(c_ref, w_ref, b_ref, o_ref):
    o_ref[...] = jnp.dot(c_ref[...], w_ref[...], precision=HIGHEST, preferred_element_type=F32) + b_ref[...]


def _modulation(c_all, mod_w, mod_b):
    n_layers, d, n_out = mod_w.shape
    rows = c_all.shape[0]
    tn = n_out // 4
    return pl.pallas_call(
        _mod_kernel,
        out_shape=jax.ShapeDtypeStruct((n_layers, rows, n_out), F32),
        grid=(n_layers, n_out // tn),
        in_specs=[
            pl.BlockSpec((rows, d), lambda l, j: (0, 0)),
            pl.BlockSpec((None, d, tn), lambda l, j: (l, 0, j)),
            pl.BlockSpec((None, 1, tn), lambda l, j: (l, 0, j)),
        ],
        out_specs=pl.BlockSpec((None, rows, tn), lambda l, j: (l, 0, j)),
        compiler_params=_cparams("arbitrary", "arbitrary"),
        name="modulation",
    )(c_all, mod_w, mod_b.reshape(n_layers, 1, n_out))


def _mod_chunk(m, j):
    return m[:, j * D_MODEL:(j + 1) * D_MODEL]


def _layer_norm(u, g, b):
    mu = jnp.mean(u, axis=-1, keepdims=True)
    uc = u - mu
    var = jnp.mean(uc * uc, axis=-1, keepdims=True)
    return uc * lax.rsqrt(var + LN_EPS) * g + b


def _route(h2, wr_ref, br_ref):
    logits = jnp.dot(h2, wr_ref[...], precision=HIGHEST, preferred_element_type=F32) + br_ref[...]
    lane_f = _iota(logits.shape, 1).astype(F32)
    neg = F32(-jnp.inf)
    big = F32(EXT)
    gl = jnp.where(lane_f < N_EXPERT_GROUPS, logits, neg)
    gmax = jnp.max(gl, axis=-1, keepdims=True)
    gidx = jnp.min(jnp.where(gl == gmax, lane_f, big), axis=-1, keepdims=True)
    g_p = 1.0 / jnp.sum(jnp.exp(gl - gmax), axis=-1, keepdims=True)
    e_lo = EXT_EXPERT0 + EXPERTS_PER_GROUP * gidx
    el = jnp.where((lane_f >= e_lo) & (lane_f < e_lo + EXPERTS_PER_GROUP), logits, neg)
    v1 = jnp.max(el, axis=-1, keepdims=True)
    i1 = jnp.min(jnp.where(el == v1, lane_f, big), axis=-1, keepdims=True)
    el2 = jnp.where(lane_f == i1, neg, el)
    v2 = jnp.max(el2, axis=-1, keepdims=True)
    i2 = jnp.min(jnp.where(el2 == v2, lane_f, big), axis=-1, keepdims=True)
    r = jnp.exp(v2 - v1)
    w1 = g_p / (1.0 + r)
    w2 = w1 * r
    ext = jnp.where(lane_f == i1, w1, 0.0) + jnp.where(lane_f == i2, w2, 0.0)
    return ext + jnp.where(lane_f == gidx + EXT_ONEHOT0, 1.0, 0.0)


def _post_mixer(x, y, m, lng_ref, lnb_ref, wr_ref, br_ref, alpha):
    x1 = _layer_norm(alpha * x + _mod_chunk(m, 2) * y, lng_ref[...], lnb_ref[...])
    h2 = x1 * (1.0 + _mod_chunk(m, 4)) + _mod_chunk(m, 3)
    return x1, h2, _route(h2, wr_ref, br_ref)


def _pool_prompt_kernel(x_ref, xprev_ref, m_ref, pw_ref, ps_ref, lng_ref, lnb_ref, wr_ref, br_ref,
                        x1_ref, hrow_ref, ext_ref, tail_ref, zbuf, *, tile, alpha):
    i = pl.program_id(1)
    m = m_ref[...]
    shift, scale = _mod_chunk(m, 0), _mod_chunk(m, 1)
    x = x_ref[...]
    z = x * (1.0 + scale) + shift
    zprev = xprev_ref[...] * (1.0 + scale) + shift
    zbuf[0:POOL_HALO, :] = jnp.where(i > 0, zprev, 0.0)
    zbuf[POOL_HALO:, :] = z
    pos = i * tile + _iota((tile, 1), 0)
    ys = []
    for g, w in enumerate(POOL_WINDOWS):
        cols = slice(g * POOL_GROUP, (g + 1) * POOL_GROUP)
        win = zbuf[pl.ds(POOL_HALO, tile), cols]
        for j in range(1, w):
            win = win + zbuf[pl.ds(POOL_HALO - j, tile), cols]
        cnt = jnp.minimum(pos + 1, w).astype(F32)
        d = win / cnt - z[:, cols]
        ys.append(jnp.dot(d.astype(BF16), pw_ref[g], preferred_element_type=F32))
    y = jnp.concatenate(ys, axis=-1) * ps_ref[...]
    x1, h2, ext = _post_mixer(x, y, m, lng_ref, lnb_ref, wr_ref, br_ref, alpha)
    x1_ref[...] = x1
    hrow_ref[:, 0:D_MODEL] = h2
    hrow_ref[:, D_MODEL:] = ext
    ext_ref[...] = ext

    @pl.when(i == pl.num_programs(1) - 1)
    def _():
        tail_ref[...] = zbuf[pl.ds(POOL_HALO + tile - POOL_BUF, POOL_BUF), :]


def _pool_prompt(x, m, pool_w, pool_scale, ln_g, ln_b, wr, br, alpha, tile=256):
    b, s, d = x.shape
    nt = s // tile
    halo_blocks = tile // POOL_HALO
    full = lambda bi, i: (0, 0)
    kern = functools.partial(_pool_prompt_kernel, tile=tile, alpha=alpha)
    return pl.pallas_call(
        kern,
        out_shape=(
            jax.ShapeDtypeStruct((b, s, d), F32),
            jax.ShapeDtypeStruct((b * s, ROW_W), F32),
            jax.ShapeDtypeStruct((b * s, EXT), F32),
            jax.ShapeDtypeStruct((b, POOL_BUF, d), F32),
        ),
        grid=(b, nt),
        in_specs=[
            pl.BlockSpec((None, tile, d), lambda bi, i: (bi, i, 0)),
            pl.BlockSpec((None, POOL_HALO, d), lambda bi, i: (bi, jnp.maximum(i * halo_blocks - 1, 0), 0)),
            pl.BlockSpec((None, 1, 6 * d), lambda bi, i: (bi, 0, 0)),
            pl.BlockSpec((N_POOL_GROUPS, POOL_GROUP, POOL_GROUP), lambda bi, i: (0, 0, 0)),
            pl.BlockSpec((1, d), full),
            pl.BlockSpec((1, d), full),
            pl.BlockSpec((1, d), full),
            pl.BlockSpec((d, EXT), full),
            pl.BlockSpec((1, EXT), full),
        ],
        out_specs=(
            pl.BlockSpec((None, tile, d), lambda bi, i: (bi, i, 0)),
            pl.BlockSpec((tile, ROW_W), lambda bi, i: (bi * nt + i, 0)),
            pl.BlockSpec((tile, EXT), lambda bi, i: (bi * nt + i, 0)),
            pl.BlockSpec((None, POOL_BUF, d), lambda bi, i: (bi, 0, 0)),
        ),
        scratch_shapes=[pltpu.VMEM((tile + POOL_HALO, d), F32)],
        compiler_params=_cparams("arbitrary", "arbitrary"),
        name="pool_prompt",
    )(x, x, m, pool_w, pool_scale, ln_g, ln_b, wr, br)


def _pool_sample_kernel(st_ref, x_ref, m_ref, pw_ref, ps_ref, lng_ref, lnb_ref, wr_ref, br_ref,
                        x1_ref, hrow_ref, ext_ref, znew_ref, *, n_new, alpha):
    d = D_MODEL
    rows = [st_ref[:, r * d:(r + 1) * d] for r in range(POOL_BUF)]
    xs, ms = [], []
    for t in range(n_new):
        m = m_ref[:, t * 6 * d:(t + 1) * 6 * d]
        x = x_ref[:, t * d:(t + 1) * d]
        z = x * (1.0 + _mod_chunk(m, 1)) + _mod_chunk(m, 0)
        znew_ref[:, t * d:(t + 1) * d] = z
        rows.append(z)
        xs.append(x)
        ms.append(m)
    for t in range(n_new):
        last = POOL_BUF + t
        ys = []
        for g, w in enumerate(POOL_WINDOWS):
            cols = slice(g * POOL_GROUP, (g + 1) * POOL_GROUP)
            win = rows[last][:, cols]
            for j in range(1, w):
                win = win + rows[last - j][:, cols]
            dgrp = win / F32(w) - rows[last][:, cols]
            ys.append(jnp.dot(dgrp.astype(BF16), pw_ref[g], preferred_element_type=F32))
        y = jnp.concatenate(ys, axis=-1) * ps_ref[...]
        x1, h2, ext = _post_mixer(xs[t], y, ms[t], lng_ref, lnb_ref, wr_ref, br_ref, alpha)
        x1_ref[:, t * d:(t + 1) * d] = x1
        hrow_ref[:, t * ROW_W:t * ROW_W + d] = h2
        hrow_ref[:, t * ROW_W + d:(t + 1) * ROW_W] = ext
        ext_ref[:, t * EXT:(t + 1) * EXT] = ext


def _pool_sample(state, x, m, pool_w, pool_scale, ln_g, ln_b, wr, br, alpha, bb=32):
    b = x.shape[0]
    d = D_MODEL
    n_new = x.shape[1] // d
    full = lambda i: (0, 0)
    kern = functools.partial(_pool_sample_kernel, n_new=n_new, alpha=alpha)
    return pl.pallas_call(
        kern,
        out_shape=(
            jax.ShapeDtypeStruct((b, n_new * d), F32),
            jax.ShapeDtypeStruct((b, n_new * ROW_W), F32),
            jax.ShapeDtypeStruct((b, n_new * EXT), F32),
            jax.ShapeDtypeStruct((b, n_new * d), F32),
        ),
        grid=(b // bb,),
        in_specs=[
            pl.BlockSpec((bb, POOL_BUF * d), lambda i: (i, 0)),
            pl.BlockSpec((bb, n_new * d), lambda i: (i, 0)),
            pl.BlockSpec((bb, n_new * 6 * d), lambda i: (i, 0)),
            pl.BlockSpec((N_POOL_GROUPS, POOL_GROUP, POOL_GROUP), lambda i: (0, 0, 0)),
            pl.BlockSpec((1, d), full),
            pl.BlockSpec((1, d), full),
            pl.BlockSpec((1, d), full),
            pl.BlockSpec((d, EXT), full),
            pl.BlockSpec((1, EXT), full),
        ],
        out_specs=(
            pl.BlockSpec((bb, n_new * d), lambda i: (i, 0)),
            pl.BlockSpec((bb, n_new * ROW_W), lambda i: (i, 0)),
            pl.BlockSpec((bb, n_new * EXT), lambda i: (i, 0)),
            pl.BlockSpec((bb, n_new * d), lambda i: (i, 0)),
        ),
        compiler_params=_cparams("arbitrary"),
        name="pool_sample",
    )(state, x, m, pool_w, pool_scale, ln_g, ln_b, wr, br)


def _sort_kernel(extp_ref, exts_ref, dest_ref, meta_ref, oh_ref, *, n_chunks_p, n_chunks_s):
    ch = SORT_CHUNK
    r_io = _iota((SUBLANES, EXT), 0)
    l_io = _iota((SUBLANES, EXT), 1)
    sel = jnp.where((l_io == r_io + EXT_ONEHOT0) & (r_io < N_EXPERT_GROUPS), 1.0, 0.0).astype(BF16)

    def count_from(ext_ref, chunk0):
        def body(c, cnt):
            ext = ext_ref[pl.ds(pl.multiple_of(c * ch, ch), ch), :]
            oh = lax.dot_general(sel, ext.astype(BF16), _NT, preferred_element_type=F32)
            oh_ref[chunk0 + c] = oh
            return cnt + jnp.sum(oh, axis=-1, keepdims=True)
        return body

    counts = lax.fori_loop(0, n_chunks_p, count_from(extp_ref, 0), jnp.zeros((SUBLANES, 1), F32))
    counts = lax.fori_loop(0, n_chunks_s, count_from(exts_ref, n_chunks_p), counts)
    padded = jnp.floor((counts + (MOE_TILE - 1)) * (1.0 / MOE_TILE)) * MOE_TILE
    row = _iota((SUBLANES, 1), 0)
    starts = jnp.zeros((SUBLANES, 1), F32)
    for g in range(1, N_EXPERT_GROUPS):
        starts = starts + jnp.where(row >= g, padded[g - 1:g, :], 0.0)
    tri = jnp.where(_iota((ch, ch), 0) < _iota((ch, ch), 1), 1.0, 0.0).astype(BF16)

    def dest_body(c, base):
        oh = oh_ref[c]
        pre = jnp.dot(oh.astype(BF16), tri, preferred_element_type=F32)
        dest = jnp.sum(oh * (base + pre), axis=0, keepdims=True)
        dest_ref[pl.ds(c, 1), :] = dest.astype(jnp.int32)
        return base + jnp.sum(oh, axis=-1, keepdims=True)

    lax.fori_loop(0, n_chunks_p + n_chunks_s, dest_body, starts)
    ends = starts + padded
    mrow = _iota((SUBLANES, EXT), 0)
    tile_lo = (_iota((SUBLANES, EXT), 1) * MOE_TILE).astype(F32)
    tgroup = jnp.sum(jnp.where((mrow < N_EXPERT_GROUPS - 1) & (tile_lo >= ends), 1.0, 0.0), axis=0, keepdims=True)
    n_used = jnp.sum(jnp.where(row < N_EXPERT_GROUPS, padded, 0.0), axis=0, keepdims=True) * (1.0 / MOE_TILE)
    meta = jnp.where(mrow == 0, tgroup, jnp.where(mrow == 1, n_used, 0.0))
    meta_ref[...] = meta.astype(jnp.int32)


def _group_sort(ext_p, ext_s):
    n_p, n_s = ext_p.shape[0], ext_s.shape[0]
    ncp, ncs = n_p // SORT_CHUNK, n_s // SORT_CHUNK
    n_tiles = (n_p + n_s) // MOE_TILE + N_EXPERT_GROUPS
    assert n_tiles <= EXT
    dest, meta = pl.pallas_call(
        functools.partial(_sort_kernel, n_chunks_p=ncp, n_chunks_s=ncs),
        out_shape=(
            jax.ShapeDtypeStruct((ncp + ncs, SORT_CHUNK), jnp.int32),
            jax.ShapeDtypeStruct((SUBLANES, EXT), jnp.int32),
        ),
        grid=(1,),
        in_specs=[pl.BlockSpec((n_p, EXT), lambda i: (0, 0)), pl.BlockSpec((n_s, EXT), lambda i: (0, 0))],
        out_specs=(
            pl.BlockSpec((ncp + ncs, SORT_CHUNK), lambda i: (0, 0)),
            pl.BlockSpec((SUBLANES, EXT), lambda i: (0, 0)),
        ),
        scratch_shapes=[pltpu.VMEM((ncp + ncs, SUBLANES, SORT_CHUNK), F32)],
        compiler_params=_cparams("arbitrary"),
        name="group_sort",
    )(ext_p, ext_s)
    return dest.reshape(-1), meta[0, :n_tiles], meta[1, :1], n_tiles


def _row_copy(src_ref, src_row, dst_ref, dst_row, sem):
    return pltpu.make_async_copy(src_ref.at[pl.ds(src_row, 1)], dst_ref.at[pl.ds(dst_row, 1)], sem)


def _permute_rows(n_rows, copy_of, wait_one):
    n_chunks = n_rows // DMA_CHUNK

    def wait_chunk():
        def one(k, a):
            wait_one()
            return a
        lax.fori_loop(0, DMA_CHUNK, one, 0)

    def body(c, carry):
        @pl.when(c >= DMA_LAG)
        def _():
            wait_chunk()

        def issue(k, a):
            copy_of(c * DMA_CHUNK + k).start()
            return a

        lax.fori_loop(0, DMA_CHUNK, issue, 0, unroll=8)
        return carry

    lax.fori_loop(0, n_chunks, body, 0)
    for _ in range(min(DMA_LAG, n_chunks)):
        wait_chunk()


def _scatter_kernel(dest_ref, srcp_ref, srcs_ref, init_ref, out_ref, sem, *, n_p, n_s):
    del init_ref

    def wait_one():
        _row_copy(srcp_ref, 0, out_ref, 0, sem).wait()

    _permute_rows(n_p, lambda n: _row_copy(srcp_ref, n, out_ref, dest_ref[n], sem), wait_one)
    _permute_rows(n_s, lambda n: _row_copy(srcs_ref, n, out_ref, dest_ref[n_p + n], sem), wait_one)


def _scatter_rows(dest, rows_p, rows_s, n_pad):
    n_p, n_s = rows_p.shape[0], rows_s.shape[0]
    w = rows_p.shape[1]
    init = jnp.zeros((n_pad, w), rows_p.dtype)
    any_spec = pl.BlockSpec(memory_space=pl.ANY)
    return pl.pallas_call(
        functools.partial(_scatter_kernel, n_p=n_p, n_s=n_s),
        out_shape=jax.ShapeDtypeStruct((n_pad, w), rows_p.dtype),
        grid_spec=pltpu.PrefetchScalarGridSpec(
            num_scalar_prefetch=1,
            grid=(1,),
            in_specs=[any_spec, any_spec, any_spec],
            out_specs=any_spec,
            scratch_shapes=[pltpu.SemaphoreType.DMA(())],
        ),
        input_output_aliases={3: 0},
        compiler_params=pltpu.CompilerParams(dimension_semantics=("arbitrary",)),
        name="scatter_rows",
    )(dest, rows_p, rows_s, init)


def _gather_kernel(dest_ref, src_ref, outp_ref, outs_ref, sem, *, n_p, n_s):
    def wait_one():
        _row_copy(src_ref, 0, outp_ref, 0, sem).wait()

    _permute_rows(n_p, lambda n: _row_copy(src_ref, dest_ref[n], outp_ref, n, sem), wait_one)
    _permute_rows(n_s, lambda n: _row_copy(src_ref, dest_ref[n_p + n], outs_ref, n, sem), wait_one)


def _gather_rows(dest, rows, n_p, n_s):
    w = rows.shape[1]
    any_spec = pl.BlockSpec(memory_space=pl.ANY)
    return pl.pallas_call(
        functools.partial(_gather_kernel, n_p=n_p, n_s=n_s),
        out_shape=(jax.ShapeDtypeStruct((n_p, w), rows.dtype), jax.ShapeDtypeStruct((n_s, w), rows.dtype)),
        grid_spec=pltpu.PrefetchScalarGridSpec(
            num_scalar_prefetch=1,
            grid=(1,),
            in_specs=[any_spec],
            out_specs=(any_spec, any_spec),
            scratch_shapes=[pltpu.SemaphoreType.DMA(())],
        ),
        compiler_params=pltpu.CompilerParams(dimension_semantics=("arbitrary",)),
        name="gather_rows",
    )(dest, rows)


def _ffn_kernel(tg_ref, nu_ref, rows_ref, w1_ref, w3_ref, w2_ref, out_ref):
    i = pl.program_id(0)

    @pl.when(i < nu_ref[0])
    def _():
        x = rows_ref[:, 0:D_MODEL].astype(BF16)
        ext = rows_ref[:, D_MODEL:]
        lane = _iota(ext.shape, 1)
        base = EXT_EXPERT0 + EXPERTS_PER_GROUP * tg_ref[i]
        acc = jnp.zeros(out_ref.shape, F32)
        for e in range(EXPERTS_PER_GROUP):
            a = jnp.dot(x, w1_ref[e], preferred_element_type=F32)
            b = jnp.dot(x, w3_ref[e], preferred_element_type=F32)
            comb = jnp.sum(jnp.where(lane == base + e, ext, 0.0), axis=-1, keepdims=True)
            hid = (a * jax.nn.sigmoid(a)) * b * comb
            acc = acc + jnp.dot(hid.astype(BF16), w2_ref[e], preferred_element_type=F32)
        out_ref[...] = acc

    @pl.when(i >= nu_ref[0])
    def _():
        out_ref[...] = jnp.zeros(out_ref.shape, F32)


def _moe_ffn(tile_group, n_used, rows_sorted, w1, w3, w2, n_tiles):
    e, d, f = w1.shape[1:]
    wspec = lambda s1, s2: pl.BlockSpec((None, e, s1, s2), lambda i, tg, nu: (tg[i], 0, 0, 0))
    return pl.pallas_call(
        _ffn_kernel,
        out_shape=jax.ShapeDtypeStruct((n_tiles * MOE_TILE, d), F32),
        grid_spec=pltpu.PrefetchScalarGridSpec(
            num_scalar_prefetch=2,
            grid=(n_tiles,),
            in_specs=[
                pl.BlockSpec((MOE_TILE, ROW_W), lambda i, tg, nu: (i, 0)),
                wspec(d, f), wspec(d, f), wspec(f, d),
            ],
            out_specs=pl.BlockSpec((MOE_TILE, d), lambda i, tg, nu: (i, 0)),
        ),
        compiler_params=_cparams("arbitrary"),
        name="moe_ffn",
    )(tile_group, n_used, rows_sorted, w1, w3, w2)


def _ln2_kernel(x_ref, f_ref, m_ref, mn_ref, lng_ref, lnb_ref, x2_ref, hn_ref, *, alpha):
    m = m_ref[...]
    x2 = _layer_norm(alpha * x_ref[...] + _mod_chunk(m, 5) * f_ref[...], lng_ref[...], lnb_ref[...])
    x2_ref[...] = x2
    mn = mn_ref[...]
    hn_ref[...] = (x2 * (1.0 + _mod_chunk(mn, 1)) + _mod_chunk(mn, 0)).astype(hn_ref.dtype)


def _mod_spec(tile, d, per_token):
    if per_token:
        return pl.BlockSpec((None, tile, 6 * d), lambda bi, i: (bi, i, 0))
    return pl.BlockSpec((None, 1, 6 * d), lambda bi, i: (bi, 0, 0))


def _ln2(x, f, m, m_next, ln_g, ln_b, alpha, per_token, tile=512):
    b, s, d = x.shape
    tile = min(tile, s)
    mspec = _mod_spec(tile, d, per_token)
    tok = pl.BlockSpec((None, tile, d), lambda bi, i: (bi, i, 0))
    vec = pl.BlockSpec((1, d), lambda bi, i: (0, 0))
    return pl.pallas_call(
        functools.partial(_ln2_kernel, alpha=alpha),
        out_shape=(jax.ShapeDtypeStruct((b, s, d), F32), jax.ShapeDtypeStruct((b, s, d), BF16)),
        grid=(b, s // tile),
        in_specs=[tok, tok, mspec, mspec, vec, vec],
        out_specs=(tok, tok),
        compiler_params=_cparams("arbitrary", "arbitrary"),
        name="ln2",
    )(x, f, m, m_next, ln_g, ln_b)


def _rope_tables(pos):
    half = HEAD_DIM // 2
    inv = ROPE_THETA ** (-jnp.arange(half, dtype=F32) / half)
    ang = pos.astype(F32)[:, None] * inv[None, :]
    cos, sin = jnp.cos(ang), jnp.sin(ang)
    reps = LANES // HEAD_DIM
    return (jnp.tile(jnp.concatenate([cos, cos], axis=-1), (1, reps)),
            jnp.tile(jnp.concatenate([-sin, sin], axis=-1), (1, reps)))


def _qkv_kernel(h_ref, w_ref, cos_ref, sin_ref, qkv_ref, *tail_refs, tile, tail_first, tail_rows):
    c = pl.program_id(0)
    i = pl.program_id(2)
    acc = jnp.dot(h_ref[...], w_ref[...], preferred_element_type=F32)
    cos, sin = cos_ref[...], sin_ref[...]
    first_half = (_iota((tile, LANES), 1) & (HEAD_DIM - 1)) < (HEAD_DIM // 2)
    blocks = []
    for j in range(D_MODEL // LANES):
        blk = acc[:, j * LANES:(j + 1) * LANES]
        partner = jnp.where(first_half, pltpu.roll(blk, LANES - HEAD_DIM // 2, 1), pltpu.roll(blk, HEAD_DIM // 2, 1))
        blocks.append(blk * cos + partner * sin)
    is_v = (c - 3 * (c // 3)) == 2
    res = jnp.where(is_v, acc, jnp.concatenate(blocks, axis=-1))
    qkv_ref[...] = res.astype(qkv_ref.dtype)
    for t, tref in enumerate(tail_refs):
        grp, which = divmod(t, 2)

        @pl.when((c == 3 * grp + 1 + which) & (i >= tail_first[grp]))
        def _(tref=tref, grp=grp):
            tref[...] = res[tile - tail_rows[grp]:, :]


def _qkv_proj(h, w, cos, sin, keeps, tile=512):
    b, s, d = h.shape
    tile = min(tile, s)
    nt = s // tile
    n_chunks = w.shape[1] // d
    tail_rows = tuple(min(k, tile) for k in keeps)
    tail_blocks = tuple(k // r for k, r in zip(keeps, tail_rows))
    tail_first = tuple(nt - nb for nb in tail_blocks)
    out_shape = [jax.ShapeDtypeStruct((b, s, n_chunks * d), BF16)]
    out_specs = [pl.BlockSpec((None, tile, d), lambda c, bi, i: (bi, i, c))]
    for grp in range(N_DIL):
        for which in range(2):
            c_t = 3 * grp + 1 + which
            first, nb = tail_first[grp], tail_blocks[grp]

            def imap(c, bi, i, c_t=c_t, first=first, nb=nb):
                before, after = c < c_t, c > c_t
                bb = jnp.where(before, 0, jnp.where(after, b - 1, bi))
                ii = jnp.where(before, 0, jnp.where(after, nb - 1, jnp.maximum(i - first, 0)))
                return (bb, ii, 0)

            out_shape.append(jax.ShapeDtypeStruct((b, keeps[grp], d), F32))
            out_specs.append(pl.BlockSpec((None, tail_rows[grp], d), imap))
    kern = functools.partial(_qkv_kernel, tile=tile, tail_first=tail_first, tail_rows=tail_rows)
    return pl.pallas_call(
        kern,
        out_shape=tuple(out_shape),
        grid=(n_chunks, b, nt),
        in_specs=[
            pl.BlockSpec((None, tile, d), lambda c, bi, i: (bi, i, 0)),
            pl.BlockSpec((d, d), lambda c, bi, i: (0, c)),
            pl.BlockSpec((tile, LANES), lambda c, bi, i: (i, 0)),
            pl.BlockSpec((tile, LANES), lambda c, bi, i: (i, 0)),
        ],
        out_specs=tuple(out_specs),
        compiler_params=_cparams("arbitrary", "arbitrary", "arbitrary"),
        name="qkv_proj",
    )(h, w, cos, sin)


def _attn_prompt_kernel(q_ref, kp_ref, kc_ref, vp_ref, vc_ref, o_ref, lse_ref):
    i = pl.program_id(2)
    tq = ATT_BAND
    lane = _iota((tq, LANES), 1)
    low = lane < HEAD_DIM
    qi = _iota((2 * tq, 2 * tq), 0) & (tq - 1)
    kj = _iota((2 * tq, 2 * tq), 1)
    mask = ((kj < tq) & (kj >= qi) & (i > 0)) | ((kj >= tq) & ((kj - tq) <= qi))
    lse_all = jnp.zeros((tq, LANES), F32)
    zero = jnp.zeros((), q_ref.dtype)
    for hp in range(N_HEADS // 2):
        sl = slice(hp * LANES, (hp + 1) * LANES)
        q = q_ref[:, sl]
        q2 = jnp.concatenate([jnp.where(low, q, zero), jnp.where(low, zero, q)], axis=0)
        kw = jnp.concatenate([kp_ref[:, sl], kc_ref[:, sl]], axis=0)
        vw = jnp.concatenate([vp_ref[:, sl], vc_ref[:, sl]], axis=0)
        s = lax.dot_general(q2, kw, _NT, preferred_element_type=F32) * (HEAD_DIM ** -0.5)
        s = jnp.where(mask, s, -jnp.inf)
        mx = jnp.max(s, axis=-1, keepdims=True)
        p = jnp.exp(s - mx)
        den = jnp.sum(p, axis=-1, keepdims=True)
        o2 = jnp.dot(p.astype(vw.dtype), vw, preferred_element_type=F32) * (1.0 / den)
        o_ref[:, sl] = jnp.where(low, o2[:tq], o2[tq:]).astype(o_ref.dtype)
        lse2 = mx + jnp.log(den)
        lse_all = jnp.where(lane == 2 * hp, lse2[:tq], jnp.where(lane == 2 * hp + 1, lse2[tq:], lse_all))
    lse_ref[...] = lse_all


def _attn_prompt(qkv, grp):
    b, s, w9 = qkv.shape
    d = D_MODEL
    dil = DIL_GROUPS[grp][1]
    sub = s // dil
    tq = ATT_BAND
    n_chunks = w9 // d
    view = qkv.reshape(b, sub, dil * w9)
    col = lambda r, which: r * n_chunks + 3 * grp + which
    cur = lambda which: pl.BlockSpec((None, tq, d), lambda bi, r, i: (bi, i, col(r, which)))
    prev = lambda which: pl.BlockSpec((None, tq, d), lambda bi, r, i: (bi, jnp.maximum(i - 1, 0), col(r, which)))
    o, lse = pl.pallas_call(
        _attn_prompt_kernel,
        out_shape=(jax.ShapeDtypeStruct((b, sub, dil * d), BF16), jax.ShapeDtypeStruct((b, sub, dil * LANES), F32)),
        grid=(b, dil, sub // tq),
        in_specs=[cur(0), prev(1), cur(1), prev(2), cur(2)],
        out_specs=(
            pl.BlockSpec((None, tq, d), lambda bi, r, i: (bi, i, r)),
            pl.BlockSpec((None, tq, LANES), lambda bi, r, i: (bi, i, r)),
        ),
        compiler_params=_cparams("arbitrary", "arbitrary", "arbitrary"),
        name=f"attn_prompt_g{grp}",
    )(view, view, view, view, view)
    return o.reshape(b, s, d), lse.reshape(b, s, LANES)


def _attn_sample_kernel(q_ref, kn_ref, vn_ref, ck0_ref, cv0_ref, ck1_ref, cv1_ref, ck2_ref, cv2_ref, o_ref, *, n_new):
    d = D_MODEL
    ncol = LANES
    col_r = _iota((ncol, d), 0)
    own = (_iota((ncol, d), 1) >> 6) == (col_r & (N_HEADS - 1))
    new_r = _iota((LANES, d), 0)
    caches = ((ck0_ref, cv0_ref), (ck1_ref, cv1_ref), (ck2_ref, cv2_ref))
    scale = HEAD_DIM ** -0.5
    scores, values = [], []
    for grp, (_, dil) in enumerate(DIL_GROUPS):
        q = q_ref[:, 3 * grp * d:(3 * grp + 1) * d].astype(F32)
        qrows = jnp.zeros((ncol, d), F32)
        knb = jnp.zeros((LANES, d), F32)
        vnb = jnp.zeros((LANES, d), F32)
        for t in range(n_new):
            qrows = jnp.where((col_r >> 4) == t, q[t:t + 1, :], qrows)
            knb = jnp.where(new_r == t, kn_ref[grp, t:t + 1, :], knb)
            vnb = jnp.where(new_r == t, vn_ref[grp, t:t + 1, :], vnb)
        qbd = jnp.where(own, qrows, 0.0).astype(BF16)
        ck_ref, cv_ref = caches[grp]
        if dil == 1:
            kc = ck_ref[...].astype(BF16)
            vc = cv_ref[...].astype(BF16)
        else:
            kc = jnp.concatenate([ck_ref[:, t * d:(t + 1) * d].astype(BF16) for t in range(n_new)], axis=0)
            vc = jnp.concatenate([cv_ref[:, t * d:(t + 1) * d].astype(BF16) for t in range(n_new)], axis=0)
        n_cache = kc.shape[0]
        r_io = _iota((n_cache, ncol), 0)
        t_io = (_iota((n_cache, ncol), 1) >> 4) & (n_new - 1)
        ok_c = (r_io >= t_io) if dil == 1 else ((r_io >> 7) == t_io)
        s_c = lax.dot_general(kc, qbd, _NT, preferred_element_type=F32) * scale
        scores.append(jnp.where(ok_c, s_c, -jnp.inf))
        values.append(vc)
        rn = _iota((LANES, ncol), 0)
        tn = (_iota((LANES, ncol), 1) >> 4) & (n_new - 1)
        ok_n = (rn <= tn) if dil == 1 else (rn == tn)
        s_n = lax.dot_general(knb.astype(BF16), qbd, _NT, preferred_element_type=F32) * scale
        scores.append(jnp.where(ok_n, s_n, -jnp.inf))
        values.append(vnb.astype(BF16))
    mx = functools.reduce(jnp.maximum, [jnp.max(s, axis=0, keepdims=True) for s in scores])
    den = jnp.zeros((1, ncol), F32)
    acc = jnp.zeros((ncol, d), F32)
    for s, v in zip(scores, values):
        p = jnp.exp(s - mx)
        den = den + jnp.sum(p, axis=0, keepdims=True)
        acc = acc + jnp.dot(p.T.astype(BF16), v, preferred_element_type=F32)
    eye = _iota((ncol, ncol), 0) == _iota((ncol, ncol), 1)
    den_col = jnp.sum(jnp.where(eye, jnp.broadcast_to(den, (ncol, ncol)), 0.0), axis=-1, keepdims=True)
    acc = jnp.where(own, acc * (1.0 / den_col), 0.0)
    out = jnp.zeros((SUBLANES, d), F32)
    out_r = _iota((SUBLANES, d), 0)
    for t in range(n_new):
        head_sum = jnp.sum(acc[t * N_HEADS:(t + 1) * N_HEADS, :], axis=0, keepdims=True)
        out = jnp.where(out_r == t, head_sum, out)
    o_ref[...] = out[0:n_new, :]


def _attn_sample(qkv, k_new, v_new, caches):
    b, n_new, w9 = qkv.shape
    d = D_MODEL
    assert n_new & (n_new - 1) == 0 and n_new * N_HEADS <= LANES
    cache_args, cache_specs = [], []
    for (win, dil), (ck, cv) in zip(DIL_GROUPS, caches):
        assert ck.shape[1] == win and win // dil == ATT_BAND and (dil == 1 or dil >= n_new)
        for c in (ck, cv):
            if dil == 1:
                cache_args.append(c)
                cache_specs.append(pl.BlockSpec((None, win, d), lambda bi: (bi, 0, 0)))
            else:
                cache_args.append(c.reshape(b, win // dil, dil * d))
                cache_specs.append(pl.BlockSpec((None, win // dil, n_new * d), lambda bi: (bi, 0, 0)))
    new_spec = pl.BlockSpec((None, N_DIL, n_new, d), lambda bi: (bi, 0, 0, 0))
    return pl.pallas_call(
        functools.partial(_attn_sample_kernel, n_new=n_new),
        out_shape=jax.ShapeDtypeStruct((b, n_new, d), F32),
        grid=(b,),
        in_specs=[pl.BlockSpec((None, n_new, w9), lambda bi: (bi, 0, 0)), new_spec, new_spec] + cache_specs,
        out_specs=pl.BlockSpec((None, n_new, d), lambda bi: (bi, 0, 0)),
        compiler_params=_cparams("arbitrary"),
        name="attn_sample",
    )(qkv, k_new, v_new, *cache_args)


def _oproj_kernel(*refs, n_groups, n_lse, alpha):
    o_refs = refs[:n_groups]
    lse_refs = refs[n_groups:n_groups + n_lse]
    k = n_groups + n_lse
    x_ref, m_ref, wo_ref, hx_ref, lng_ref, lnb_ref, wr_ref, br_ref = refs[k:k + 8]
    x1_ref, hrow_ref, ext_ref = refs[k + 8:]
    if n_lse == 0:
        o = o_refs[0][...].astype(BF16)
    else:
        lses = [r[...] for r in lse_refs]
        mx = functools.reduce(jnp.maximum, lses)
        es = [jnp.exp(l - mx) for l in lses]
        inv = 1.0 / functools.reduce(lambda a, b: a + b, es)
        o = jnp.zeros(x_ref.shape, F32)
        hx = hx_ref[...]
        for e, oref in zip(es, o_refs):
            wgt = e * inv
            hi = wgt.astype(BF16)
            lo = (wgt - hi.astype(F32)).astype(BF16)
            wexp = jnp.dot(hi, hx, preferred_element_type=F32) + jnp.dot(lo, hx, preferred_element_type=F32)
            o = o + wexp * oref[...].astype(F32)
        o = o.astype(BF16)
    y = jnp.dot(o, wo_ref[...], preferred_element_type=F32)
    x1, h2, ext = _post_mixer(x_ref[...], y, m_ref[...], lng_ref, lnb_ref, wr_ref, br_ref, alpha)
    x1_ref[...] = x1
    hrow_ref[:, 0:D_MODEL] = h2
    hrow_ref[:, D_MODEL:] = ext
    ext_ref[...] = ext


def _out_proj(os_, lses, x, m, w_o, head_expand, ln_g, ln_b, wr, br, alpha, per_token, tile=512):
    b, s, d = x.shape
    tile = min(tile, s)
    nt = s // tile
    tok = pl.BlockSpec((None, tile, d), lambda bi, i: (bi, i, 0))
    lse_spec = pl.BlockSpec((None, tile, LANES), lambda bi, i: (bi, i, 0))
    full = lambda bi, i: (0, 0)
    return pl.pallas_call(
        functools.partial(_oproj_kernel, n_groups=len(os_), n_lse=len(lses), alpha=alpha),
        out_shape=(
            jax.ShapeDtypeStruct((b, s, d), F32),
            jax.ShapeDtypeStruct((b * s, ROW_W), F32),
            jax.ShapeDtypeStruct((b * s, EXT), F32),
        ),
        grid=(b, nt),
        in_specs=[tok] * len(os_) + [lse_spec] * len(lses) + [
            tok, _mod_spec(tile, d, per_token),
            pl.BlockSpec((d, d), full),
            pl.BlockSpec((LANES, d), full),
            pl.BlockSpec((1, d), full),
            pl.BlockSpec((1, d), full),
            pl.BlockSpec((d, EXT), full),
            pl.BlockSpec((1, EXT), full),
        ],
        out_specs=(
            tok,
            pl.BlockSpec((tile, ROW_W), lambda bi, i: (bi * nt + i, 0)),
            pl.BlockSpec((tile, EXT), lambda bi, i: (bi * nt + i, 0)),
        ),
        compiler_params=_cparams("arbitrary", "arbitrary"),
        name="out_proj",
    )(*os_, *lses, x, m, w_o, head_expand, ln_g, ln_b, wr, br)


def _router_matrix(w_group, b_group, w_er, b_er):
    d = w_group.shape[0]
    n_e = N_EXPERT_GROUPS * EXPERTS_PER_GROUP
    w_e = jnp.transpose(w_er, (1, 0, 2)).reshape(d, n_e)
    pad = EXT - N_EXPERT_GROUPS - n_e
    wr = jnp.concatenate([w_group, w_e, jnp.zeros((d, pad), F32)], axis=1)
    br = jnp.concatenate([b_group, b_er.reshape(n_e), jnp.zeros((pad,), F32)])[None, :]
    return wr, br


def _moe_layer(rows_p, rows_s, ext_p, ext_s, w1, w3, w2):
    n_p, n_s = rows_p.shape[0], rows_s.shape[0]
    dest, tile_group, n_used, n_tiles = _group_sort(ext_p, ext_s)
    sorted_rows = _scatter_rows(dest, rows_p, rows_s, n_tiles * MOE_TILE)
    f_sorted = _moe_ffn(tile_group, n_used, sorted_rows, w1, w3, w2, n_tiles)
    return _gather_rows(dest, f_sorted, n_p, n_s)


def kernel(x_prompt, x_sample, state_pool, cache_k_w128, cache_v_w128, cache_k_w512, cache_v_w512, cache_k_w2048, cache_v_w2048, c_prompt, c_sample, mod_w, mod_b, ln1_g, ln1_b, ln2_g, ln2_b, pool_w, pool_scale, attn_w_qkv, attn_w_o, moe_w_group, moe_b_group, moe_w_expert_router, moe_b_expert_router, moe_w1, moe_w3, moe_w2):
    depth = mod_w.shape[0]
    alpha = float((2.0 * depth) ** 0.25)
    bp, s, d = x_prompt.shape
    bs, t_new, _ = x_sample.shape
    n_s = bs * t_new
    kv_caches = ((cache_k_w128, cache_v_w128), (cache_k_w512, cache_v_w512), (cache_k_w2048, cache_v_w2048))

    c_all = jnp.concatenate([jnp.repeat(c_sample, t_new, axis=0), c_prompt,
                             jnp.zeros((SUBLANES - bp % SUBLANES, d), F32)], axis=0)
    m_all = _modulation(c_all, mod_w, mod_b)
    m_tok = [m_all[i, :n_s].reshape(1, n_s, 6 * d) for i in range(depth)]
    m_seq = [m_all[i, n_s:n_s + bp].reshape(bp, 1, 6 * d) for i in range(depth)]

    head_expand = jnp.asarray(np.arange(LANES)[:, None] == (np.arange(d)[None, :] // HEAD_DIM), BF16)
    cos_p, sin_p = _rope_tables(jnp.arange(s, dtype=jnp.int32))
    cos_s, sin_s = _rope_tables(PAST_LEN + (jnp.arange(n_s, dtype=jnp.int32) % t_new))

    xp, xs = x_prompt, x_sample.reshape(1, n_s, d)
    hp = hs = None
    pool_p, pool_s = [], []
    kv_p = [[] for _ in range(2 * N_DIL)]
    kv_s = [[] for _ in range(2 * N_DIL)]
    for i in range(depth):
        li = i // 2
        wr, br = _router_matrix(moe_w_group[i], moe_b_group[i], moe_w_expert_router[i], moe_b_expert_router[i])
        ln1g, ln1b = ln1_g[i][None, :], ln1_b[i][None, :]
        if i % 2 == 0:
            pw = pool_w[li].astype(BF16)
            ps = pool_scale[li][None, :]
            xp, rows_p, ext_p, tail = _pool_prompt(xp, m_seq[i], pw, ps, ln1g, ln1b, wr, br, alpha)
            x1s, rows_s, ext_s, znew = _pool_sample(
                state_pool[li].reshape(bs, POOL_BUF * d), xs.reshape(bs, t_new * d),
                m_tok[i].reshape(bs, t_new * 6 * d), pw, ps, ln1g, ln1b, wr, br, alpha)
            xs = x1s.reshape(1, n_s, d)
            rows_s = rows_s.reshape(n_s, ROW_W)
            ext_s = ext_s.reshape(n_s, EXT)
            pool_p.append(tail)
            pool_s.append(znew.reshape(bs, t_new, d))
        else:
            wqkv = attn_w_qkv[li].astype(BF16)
            wo = attn_w_o[li].astype(BF16)
            keeps = tuple(min(win, s) for win, _ in DIL_GROUPS)
            qkv_p, *tails_p = _qkv_proj(hp, wqkv, cos_p, sin_p, keeps)
            qkv_s, *tails_s = _qkv_proj(hs, wqkv, cos_s, sin_s, (n_s,) * N_DIL)
            for j in range(2 * N_DIL):
                kv_p[j].append(tails_p[j].reshape(bp, -1, N_HEADS, HEAD_DIM))
                kv_s[j].append(tails_s[j].reshape(bs, t_new, N_HEADS, HEAD_DIM))
            outs = [_attn_prompt(qkv_p, g) for g in range(N_DIL)]
            xp, rows_p, ext_p = _out_proj([o for o, _ in outs], [l for _, l in outs], xp, m_seq[i], wo, head_expand,
                                          ln1g, ln1b, wr, br, alpha, per_token=False)
            k_new = jnp.stack([tails_s[2 * g].reshape(bs, t_new, d) for g in range(N_DIL)], axis=1)
            v_new = jnp.stack([tails_s[2 * g + 1].reshape(bs, t_new, d) for g in range(N_DIL)], axis=1)
            caches = [(kc[li].reshape(bs, -1, d), vc[li].reshape(bs, -1, d)) for kc, vc in kv_caches]
            o_s = _attn_sample(qkv_s.reshape(bs, t_new, -1), k_new, v_new, caches)
            xs, rows_s, ext_s = _out_proj([o_s.reshape(1, n_s, d)], [], xs, m_tok[i], wo, head_expand,
                                          ln1g, ln1b, wr, br, alpha, per_token=True)
        f_p, f_s = _moe_layer(rows_p, rows_s, ext_p, ext_s,
                              moe_w1[i].astype(BF16), moe_w3[i].astype(BF16), moe_w2[i].astype(BF16))
        nxt = min(i + 1, depth - 1)
        ln2g, ln2b = ln2_g[i][None, :], ln2_b[i][None, :]
        xp, hp = _ln2(xp, f_p.reshape(bp, s, d), m_seq[i], m_seq[nxt], ln2g, ln2b, alpha, per_token=False)
        xs, hs = _ln2(xs, f_s.reshape(1, n_s, d), m_tok[i], m_tok[nxt], ln2g, ln2b, alpha, per_token=True)

    stack = lambda lst: jnp.stack(lst, axis=0)
    return (xp, xs.reshape(bs, t_new, d), stack(pool_p), stack(pool_s),
            *[stack(kv_p[j]) for j in range(2 * N_DIL)], *[stack(kv_s[j]) for j in range(2 * N_DIL)])
```

```python
import functools

import numpy as np
import jax
import jax.numpy as jnp
from jax import lax
from jax.experimental import pallas as pl
from jax.experimental.pallas import tpu as pltpu

F32 = jnp.float32
BF16 = jnp.bfloat16
HIGHEST = lax.Precision.HIGHEST

D_MODEL = 1024
POOL_WINDOWS = (2, 4, 8, 16)
N_POOL_GROUPS = len(POOL_WINDOWS)
POOL_GROUP = D_MODEL // N_POOL_GROUPS
POOL_BUF = max(POOL_WINDOWS) - 1
POOL_HALO = 16
HEAD_DIM = 64
N_HEADS = D_MODEL // HEAD_DIM
DIL_GROUPS = ((128, 1), (512, 4), (2048, 16))
N_DIL = len(DIL_GROUPS)
ATT_BAND = 128
ROPE_THETA = 10000.0
N_EXPERT_GROUPS = 4
EXPERTS_PER_GROUP = 8
PAST_LEN = 2048
LN_EPS = 1e-5

LANES = 128
SUBLANES = 8
VMEM_LIMIT_BYTES = 56 * 1024 * 1024

EXT = LANES
EXT_EXPERT0 = N_EXPERT_GROUPS
EXT_ONEHOT0 = 120
ROW_W = D_MODEL + EXT

MOE_TILE = 512
SORT_CHUNK = 512
ROW_TILE = 512

_NT = (((1,), (1,)), ((), ()))


def _cparams(*sem):
    return pltpu.CompilerParams(dimension_semantics=sem, vmem_limit_bytes=VMEM_LIMIT_BYTES)


def _iota(shape, dim):
    return lax.broadcasted_iota(jnp.int32, shape, dim)


def _mod_kernel(c_ref, w_ref, b_ref, o_ref):
    o_ref[...] = jnp.dot(c_ref[...], w_ref[...], precision=HIGHEST, preferred_element_type=F32) + b_ref[...]


def _modulation(c_all, mod_w, mod_b):
    n_layers, d, n_out = mod_w.shape
    rows = c_all.shape[0]
    tn = n_out // 4
    return pl.pallas_call(
        _mod_kernel,
        out_shape=jax.ShapeDtypeStruct((n_layers, rows, n_out), F32),
        grid=(n_layers, n_out // tn),
        in_specs=[
            pl.BlockSpec((rows, d), lambda l, j: (0, 0)),
            pl.BlockSpec((None, d, tn), lambda l, j: (l, 0, j)),
            pl.BlockSpec((None, 1, tn), lambda l, j: (l, 0, j)),
        ],
        out_specs=pl.BlockSpec((None, rows, tn), lambda l, j: (l, 0, j)),
        compiler_params=_cparams("arbitrary", "arbitrary"),
        name="modulation",
    )(c_all, mod_w, mod_b.reshape(n_layers, 1, n_out))


def _mod_chunk(m, j):
    return m[:, j * D_MODEL:(j + 1) * D_MODEL]


def _layer_norm(u, g, b):
    mu = jnp.mean(u, axis=-1, keepdims=True)
    uc = u - mu
    var = jnp.mean(uc * uc, axis=-1, keepdims=True)
    return uc * lax.rsqrt(var + LN_EPS) * g + b


def _route(h2, wr_ref, br_ref):
    logits = jnp.dot(h2, wr_ref[...], precision=HIGHEST, preferred_element_type=F32) + br_ref[...]
    lane_f = _iota(logits.shape, 1).astype(F32)
    neg = F32(-jnp.inf)
    big = F32(EXT)
    gl = jnp.where(lane_f < N_EXPERT_GROUPS, logits, neg)
    gmax = jnp.max(gl, axis=-1, keepdims=True)
    gidx = jnp.min(jnp.where(gl == gmax, lane_f, big), axis=-1, keepdims=True)
    g_p = 1.0 / jnp.sum(jnp.exp(gl - gmax), axis=-1, keepdims=True)
    e_lo = EXT_EXPERT0 + EXPERTS_PER_GROUP * gidx
    el = jnp.where((lane_f >= e_lo) & (lane_f < e_lo + EXPERTS_PER_GROUP), logits, neg)
    v1 = jnp.max(el, axis=-1, keepdims=True)
    i1 = jnp.min(jnp.where(el == v1, lane_f, big), axis=-1, keepdims=True)
    el2 = jnp.where(lane_f == i1, neg, el)
    v2 = jnp.max(el2, axis=-1, keepdims=True)
    i2 = jnp.min(jnp.where(el2 == v2, lane_f, big), axis=-1, keepdims=True)
    r = jnp.exp(v2 - v1)
    w1 = g_p / (1.0 + r)
    w2 = w1 * r
    ext = jnp.where(lane_f == i1, w1, 0.0) + jnp.where(lane_f == i2, w2, 0.0)
    return ext + jnp.where(lane_f == gidx + EXT_ONEHOT0, 1.0, 0.0)


def _post_mixer(x, y, m, lng_ref, lnb_ref, wr_ref, br_ref, alpha):
    x1 = _layer_norm(alpha * x + _mod_chunk(m, 2) * y, lng_ref[...], lnb_ref[...])
    h2 = x1 * (1.0 + _mod_chunk(m, 4)) + _mod_chunk(m, 3)
    return x1, h2, _route(h2, wr_ref, br_ref)


def _pool_prompt_kernel(x_ref, xprev_ref, m_ref, pw_ref, ps_ref, lng_ref, lnb_ref, wr_ref, br_ref,
                        x1_ref, hrow_ref, ext_ref, tail_ref, zbuf, *, tile, alpha):
    i = pl.program_id(1)
    m = m_ref[...]
    shift, scale = _mod_chunk(m, 0), _mod_chunk(m, 1)
    x = x_ref[...]
    z = x * (1.0 + scale) + shift
    zprev = xprev_ref[...] * (1.0 + scale) + shift
    zbuf[0:POOL_HALO, :] = jnp.where(i > 0, zprev, 0.0)
    zbuf[POOL_HALO:, :] = z
    pos = i * tile + _iota((tile, 1), 0)
    ys = []
    for g, w in enumerate(POOL_WINDOWS):
        cols = slice(g * POOL_GROUP, (g + 1) * POOL_GROUP)
        win = zbuf[pl.ds(POOL_HALO, tile), cols]
        for j in range(1, w):
            win = win + zbuf[pl.ds(POOL_HALO - j, tile), cols]
        cnt = jnp.minimum(pos + 1, w).astype(F32)
        d = win / cnt - z[:, cols]
        ys.append(jnp.dot(d.astype(BF16), pw_ref[g], preferred_element_type=F32))
    y = jnp.concatenate(ys, axis=-1) * ps_ref[...]
    x1, h2, ext = _post_mixer(x, y, m, lng_ref, lnb_ref, wr_ref, br_ref, alpha)
    x1_ref[...] = x1
    hrow_ref[:, 0:D_MODEL] = h2
    hrow_ref[:, D_MODEL:] = ext
    ext_ref[...] = ext

    @pl.when(i == pl.num_programs(1) - 1)
    def _():
        tail_ref[...] = zbuf[pl.ds(POOL_HALO + tile - POOL_BUF, POOL_BUF), :]


def _pool_prompt(x, m, pool_w, pool_scale, ln_g, ln_b, wr, br, alpha, tile=256):
    b, s, d = x.shape
    nt = s // tile
    halo_blocks = tile // POOL_HALO
    full = lambda bi, i: (0, 0)
    kern = functools.partial(_pool_prompt_kernel, tile=tile, alpha=alpha)
    return pl.pallas_call(
        kern,
        out_shape=(
            jax.ShapeDtypeStruct((b, s, d), F32),
            jax.ShapeDtypeStruct((b * s, ROW_W), F32),
            jax.ShapeDtypeStruct((b * s, EXT), F32),
            jax.ShapeDtypeStruct((b, POOL_BUF, d), F32),
        ),
        grid=(b, nt),
        in_specs=[
            pl.BlockSpec((None, tile, d), lambda bi, i: (bi, i, 0)),
            pl.BlockSpec((None, POOL_HALO, d), lambda bi, i: (bi, jnp.maximum(i * halo_blocks - 1, 0), 0)),
            pl.BlockSpec((None, 1, 6 * d), lambda bi, i: (bi, 0, 0)),
            pl.BlockSpec((N_POOL_GROUPS, POOL_GROUP, POOL_GROUP), lambda bi, i: (0, 0, 0)),
            pl.BlockSpec((1, d), full),
            pl.BlockSpec((1, d), full),
            pl.BlockSpec((1, d), full),
            pl.BlockSpec((d, EXT), full),
            pl.BlockSpec((1, EXT), full),
        ],
        out_specs=(
            pl.BlockSpec((None, tile, d), lambda bi, i: (bi, i, 0)),
            pl.BlockSpec((tile, ROW_W), lambda bi, i: (bi * nt + i, 0)),
            pl.BlockSpec((tile, EXT), lambda bi, i: (bi * nt + i, 0)),
            pl.BlockSpec((None, POOL_BUF, d), lambda bi, i: (bi, 0, 0)),
        ),
        scratch_shapes=[pltpu.VMEM((tile + POOL_HALO, d), F32)],
        compiler_params=_cparams("arbitrary", "arbitrary"),
        name="pool_prompt",
    )(x, x, m, pool_w, pool_scale, ln_g, ln_b, wr, br)


def _pool_sample_kernel(st_ref, x_ref, m_ref, pw_ref, ps_ref, lng_ref, lnb_ref, wr_ref, br_ref,
                        x1_ref, hrow_ref, ext_ref, znew_ref, *, n_new, alpha):
    d = D_MODEL
    rows = [st_ref[:, r * d:(r + 1) * d] for r in range(POOL_BUF)]
    xs, ms = [], []
    for t in range(n_new):
        m = m_ref[:, t * 6 * d:(t + 1) * 6 * d]
        x = x_ref[:, t * d:(t + 1) * d]
        z = x * (1.0 + _mod_chunk(m, 1)) + _mod_chunk(m, 0)
        znew_ref[:, t * d:(t + 1) * d] = z
        rows.append(z)
        xs.append(x)
        ms.append(m)
    for t in range(n_new):
        last = POOL_BUF + t
        ys = []
        for g, w in enumerate(POOL_WINDOWS):
            cols = slice(g * POOL_GROUP, (g + 1) * POOL_GROUP)
            win = rows[last][:, cols]
            for j in range(1, w):
                win = win + rows[last - j][:, cols]
            dgrp = win / F32(w) - rows[last][:, cols]
            ys.append(jnp.dot(dgrp.astype(BF16), pw_ref[g], preferred_element_type=F32))
        y = jnp.concatenate(ys, axis=-1) * ps_ref[...]
        x1, h2, ext = _post_mixer(xs[t], y, ms[t], lng_ref, lnb_ref, wr_ref, br_ref, alpha)
        x1_ref[:, t * d:(t + 1) * d] = x1
        hrow_ref[:, t * ROW_W:t * ROW_W + d] = h2
        hrow_ref[:, t * ROW_W + d:(t + 1) * ROW_W] = ext
        ext_ref[:, t * EXT:(t + 1) * EXT] = ext


def _pool_sample(state, x, m, pool_w, pool_scale, ln_g, ln_b, wr, br, alpha, bb=32):
    b = x.shape[0]
    d = D_MODEL
    n_new = x.shape[1] // d
    full = lambda i: (0, 0)
    kern = functools.partial(_pool_sample_kernel, n_new=n_new, alpha=alpha)
    return pl.pallas_call(
        kern,
        out_shape=(
            jax.ShapeDtypeStruct((b, n_new * d), F32),
            jax.ShapeDtypeStruct((b, n_new * ROW_W), F32),
            jax.ShapeDtypeStruct((b, n_new * EXT), F32),
            jax.ShapeDtypeStruct((b, n_new * d), F32),
        ),
        grid=(b // bb,),
        in_specs=[
            pl.BlockSpec((bb, POOL_BUF * d), lambda i: (i, 0)),
            pl.BlockSpec((bb, n_new * d), lambda i: (i, 0)),
            pl.BlockSpec((bb, n_new * 6 * d), lambda i: (i, 0)),
            pl.BlockSpec((N_POOL_GROUPS, POOL_GROUP, POOL_GROUP), lambda i: (0, 0, 0)),
            pl.BlockSpec((1, d), full),
            pl.BlockSpec((1, d), full),
            pl.BlockSpec((1, d), full),
            pl.BlockSpec((d, EXT), full),
            pl.BlockSpec((1, EXT), full),
        ],
        out_specs=(
            pl.BlockSpec((bb, n_new * d), lambda i: (i, 0)),
            pl.BlockSpec((bb, n_new * ROW_W), lambda i: (i, 0)),
            pl.BlockSpec((bb, n_new * EXT), lambda i: (i, 0)),
            pl.BlockSpec((bb, n_new * d), lambda i: (i, 0)),
        ),
        compiler_params=_cparams("arbitrary"),
        name="pool_sample",
    )(state, x, m, pool_w, pool_scale, ln_g, ln_b, wr, br)


def _sort_kernel(extp_ref, exts_ref, dest_ref, meta_ref, oh_ref, *, n_chunks_p, n_chunks_s):
    ch = SORT_CHUNK
    r_io = _iota((SUBLANES, EXT), 0)
    l_io = _iota((SUBLANES, EXT), 1)
    sel = jnp.where((l_io == r_io + EXT_ONEHOT0) & (r_io < N_EXPERT_GROUPS), 1.0, 0.0).astype(BF16)

    def count_from(ext_ref, chunk0):
        def body(c, cnt):
            ext = ext_ref[pl.ds(pl.multiple_of(c * ch, ch), ch), :]
            oh = lax.dot_general(sel, ext.astype(BF16), _NT, preferred_element_type=F32)
            oh_ref[chunk0 + c] = oh
            return cnt + jnp.sum(oh, axis=-1, keepdims=True)
        return body

    counts = lax.fori_loop(0, n_chunks_p, count_from(extp_ref, 0), jnp.zeros((SUBLANES, 1), F32))
    counts = lax.fori_loop(0, n_chunks_s, count_from(exts_ref, n_chunks_p), counts)
    padded = jnp.floor((counts + (MOE_TILE - 1)) * (1.0 / MOE_TILE)) * MOE_TILE
    row = _iota((SUBLANES, 1), 0)
    starts = jnp.zeros((SUBLANES, 1), F32)
    for g in range(1, N_EXPERT_GROUPS):
        starts = starts + jnp.where(row >= g, padded[g - 1:g, :], 0.0)
    tri = jnp.where(_iota((ch, ch), 0) < _iota((ch, ch), 1), 1.0, 0.0).astype(BF16)

    def dest_body(c, base):
        oh = oh_ref[c]
        pre = jnp.dot(oh.astype(BF16), tri, preferred_element_type=F32)
        dest = jnp.sum(oh * (base + pre), axis=0, keepdims=True)
        dest_ref[pl.ds(c, 1), :] = dest.astype(jnp.int32)
        return base + jnp.sum(oh, axis=-1, keepdims=True)

    lax.fori_loop(0, n_chunks_p + n_chunks_s, dest_body, starts)
    ends = starts + padded
    mrow = _iota((SUBLANES, EXT), 0)
    tile_lo = (_iota((SUBLANES, EXT), 1) * MOE_TILE).astype(F32)
    tgroup = jnp.sum(jnp.where((mrow < N_EXPERT_GROUPS - 1) & (tile_lo >= ends), 1.0, 0.0), axis=0, keepdims=True)
    n_used = jnp.sum(jnp.where(row < N_EXPERT_GROUPS, padded, 0.0), axis=0, keepdims=True) * (1.0 / MOE_TILE)
    meta = jnp.where(mrow == 0, tgroup, jnp.where(mrow == 1, n_used, 0.0))
    meta_ref[...] = meta.astype(jnp.int32)


def _group_sort(ext_p, ext_s):
    n_p, n_s = ext_p.shape[0], ext_s.shape[0]
    ncp, ncs = n_p // SORT_CHUNK, n_s // SORT_CHUNK
    n_tiles = (n_p + n_s) // MOE_TILE + N_EXPERT_GROUPS
    assert n_tiles <= EXT
    dest, meta = pl.pallas_call(
        functools.partial(_sort_kernel, n_chunks_p=ncp, n_chunks_s=ncs),
        out_shape=(
            jax.ShapeDtypeStruct((ncp + ncs, SORT_CHUNK), jnp.int32),
            jax.ShapeDtypeStruct((SUBLANES, EXT), jnp.int32),
        ),
        grid=(1,),
        in_specs=[pl.BlockSpec((n_p, EXT), lambda i: (0, 0)), pl.BlockSpec((n_s, EXT), lambda i: (0, 0))],
        out_specs=(
            pl.BlockSpec((ncp + ncs, SORT_CHUNK), lambda i: (0, 0)),
            pl.BlockSpec((SUBLANES, EXT), lambda i: (0, 0)),
        ),
        scratch_shapes=[pltpu.VMEM((ncp + ncs, SUBLANES, SORT_CHUNK), F32)],
        compiler_params=_cparams("arbitrary"),
        name="group_sort",
    )(ext_p, ext_s)
    return dest.reshape(-1), meta[0, :n_tiles], meta[1, :1], n_tiles


def _scatter_kernel(dest_ref, rp_ref, rs_ref, init_ref, out_ref, sem, *, tiles_p):
    del init_ref
    i = pl.program_id(0)
    base = i * ROW_TILE

    def run(src_ref):
        def issue(k, a):
            pltpu.make_async_copy(src_ref.at[pl.ds(k, 1)], out_ref.at[pl.ds(dest_ref[base + k], 1)], sem).start()
            return a

        def wait(k, a):
            pltpu.make_async_copy(src_ref.at[pl.ds(0, 1)], out_ref.at[pl.ds(0, 1)], sem).wait()
            return a

        lax.fori_loop(0, ROW_TILE, issue, 0, unroll=8)
        lax.fori_loop(0, ROW_TILE, wait, 0, unroll=8)

    @pl.when(i < tiles_p)
    def _():
        run(rp_ref)

    @pl.when(i >= tiles_p)
    def _():
        run(rs_ref)


def _scatter_rows(dest, rows_p, rows_s, n_pad):
    n_p, n_s = rows_p.shape[0], rows_s.shape[0]
    w = rows_p.shape[1]
    tiles_p, tiles_s = n_p // ROW_TILE, n_s // ROW_TILE
    init = jnp.zeros((n_pad, w), rows_p.dtype)
    any_spec = pl.BlockSpec(memory_space=pl.ANY)
    return pl.pallas_call(
        functools.partial(_scatter_kernel, tiles_p=tiles_p),
        out_shape=jax.ShapeDtypeStruct((n_pad, w), rows_p.dtype),
        grid_spec=pltpu.PrefetchScalarGridSpec(
            num_scalar_prefetch=1,
            grid=(tiles_p + tiles_s,),
            in_specs=[
                pl.BlockSpec((ROW_TILE, w), lambda i, dst: (jnp.minimum(i, tiles_p - 1), 0)),
                pl.BlockSpec((ROW_TILE, w), lambda i, dst: (jnp.maximum(i - tiles_p, 0), 0)),
                any_spec,
            ],
            out_specs=any_spec,
            scratch_shapes=[pltpu.SemaphoreType.DMA(())],
        ),
        input_output_aliases={3: 0},
        compiler_params=_cparams("arbitrary"),
        name="scatter_rows",
    )(dest, rows_p, rows_s, init)


def _ffn_kernel(tg_ref, nu_ref, rows_ref, w1_ref, w3_ref, w2_ref, out_ref):
    i = pl.program_id(0)

    @pl.when(i < nu_ref[0])
    def _():
        x = rows_ref[:, 0:D_MODEL].astype(BF16)
        ext = rows_ref[:, D_MODEL:]
        lane = _iota(ext.shape, 1)
        base = EXT_EXPERT0 + EXPERTS_PER_GROUP * tg_ref[i]
        acc = jnp.zeros(out_ref.shape, F32)
        for e in range(EXPERTS_PER_GROUP):
            a = jnp.dot(x, w1_ref[e], preferred_element_type=F32)
            b = jnp.dot(x, w3_ref[e], preferred_element_type=F32)
            comb = jnp.sum(jnp.where(lane == base + e, ext, 0.0), axis=-1, keepdims=True)
            hid = (a * jax.nn.sigmoid(a)) * b * comb
            acc = acc + jnp.dot(hid.astype(BF16), w2_ref[e], preferred_element_type=F32)
        out_ref[...] = acc

    @pl.when(i >= nu_ref[0])
    def _():
        out_ref[...] = jnp.zeros(out_ref.shape, F32)


def _moe_ffn(tile_group, n_used, rows_sorted, w1, w3, w2, n_tiles):
    e, d, f = w1.shape[1:]
    wspec = lambda s1, s2: pl.BlockSpec((None, e, s1, s2), lambda i, tg, nu: (tg[i], 0, 0, 0))
    return pl.pallas_call(
        _ffn_kernel,
        out_shape=jax.ShapeDtypeStruct((n_tiles * MOE_TILE, d), F32),
        grid_spec=pltpu.PrefetchScalarGridSpec(
            num_scalar_prefetch=2,
            grid=(n_tiles,),
            in_specs=[
                pl.BlockSpec((MOE_TILE, ROW_W), lambda i, tg, nu: (i, 0)),
                wspec(d, f), wspec(d, f), wspec(f, d),
            ],
            out_specs=pl.BlockSpec((MOE_TILE, d), lambda i, tg, nu: (i, 0)),
        ),
        compiler_params=_cparams("arbitrary"),
        name="moe_ffn",
    )(tile_group, n_used, rows_sorted, w1, w3, w2)


def _ln2_kernel(dest_ref, x_ref, fs_ref, m_ref, mn_ref, lng_ref, lnb_ref, x2_ref, hn_ref, fbuf, sem,
                *, alpha, tile, row0, steps_per_seq, n_steps):
    step = pl.program_id(0) * steps_per_seq + pl.program_id(1)
    slot = lax.rem(step, 2)

    def start_tile(st, sl):
        base = row0 + st * tile

        def issue(k, a):
            pltpu.make_async_copy(fs_ref.at[pl.ds(dest_ref[base + k], 1)], fbuf.at[sl, pl.ds(k, 1)], sem.at[sl]).start()
            return a

        lax.fori_loop(0, tile, issue, 0, unroll=8)

    @pl.when(step == 0)
    def _():
        start_tile(0, 0)

    @pl.when(step + 1 < n_steps)
    def _():
        start_tile(step + 1, 1 - slot)

    def wait(k, a):
        pltpu.make_async_copy(fs_ref.at[pl.ds(0, 1)], fbuf.at[slot, pl.ds(0, 1)], sem.at[slot]).wait()
        return a

    lax.fori_loop(0, tile, wait, 0, unroll=8)
    m = m_ref[...]
    x2 = _layer_norm(alpha * x_ref[...] + _mod_chunk(m, 5) * fbuf[slot], lng_ref[...], lnb_ref[...])
    x2_ref[...] = x2
    mn = mn_ref[...]
    hn_ref[...] = (x2 * (1.0 + _mod_chunk(mn, 1)) + _mod_chunk(mn, 0)).astype(hn_ref.dtype)


def _mod_spec(tile, d, per_token, n_prefetch=0):
    if per_token:
        return pl.BlockSpec((None, tile, 6 * d), lambda bi, i, *_: (bi, i, 0))
    return pl.BlockSpec((None, 1, 6 * d), lambda bi, i, *_: (bi, 0, 0))


def _ln2(dest, x, f_sorted, row0, m, m_next, ln_g, ln_b, alpha, per_token):
    b, s, d = x.shape
    tile = min(ROW_TILE, s)
    nt = s // tile
    mspec = _mod_spec(tile, d, per_token)
    tok = pl.BlockSpec((None, tile, d), lambda bi, i, dst: (bi, i, 0))
    vec = pl.BlockSpec((1, d), lambda bi, i, dst: (0, 0))
    kern = functools.partial(_ln2_kernel, alpha=alpha, tile=tile, row0=row0, steps_per_seq=nt, n_steps=b * nt)
    return pl.pallas_call(
        kern,
        out_shape=(jax.ShapeDtypeStruct((b, s, d), F32), jax.ShapeDtypeStruct((b, s, d), BF16)),
        grid_spec=pltpu.PrefetchScalarGridSpec(
            num_scalar_prefetch=1,
            grid=(b, nt),
            in_specs=[tok, pl.BlockSpec(memory_space=pl.ANY), mspec, mspec, vec, vec],
            out_specs=(tok, tok),
            scratch_shapes=[pltpu.VMEM((2, tile, d), F32), pltpu.SemaphoreType.DMA((2,))],
        ),
        compiler_params=_cparams("arbitrary", "arbitrary"),
        name="ln2",
    )(dest, x, f_sorted, m, m_next, ln_g, ln_b)


def _rope_tables(pos):
    half = HEAD_DIM // 2
    inv = ROPE_THETA ** (-jnp.arange(half, dtype=F32) / half)
    ang = pos.astype(F32)[:, None] * inv[None, :]
    cos, sin = jnp.cos(ang), jnp.sin(ang)
    reps = LANES // HEAD_DIM
    return (jnp.tile(jnp.concatenate([cos, cos], axis=-1), (1, reps)),
            jnp.tile(jnp.concatenate([-sin, sin], axis=-1), (1, reps)))


def _qkv_kernel(h_ref, w_ref, cos_ref, sin_ref, *refs, tile, grouped, tail_chunks, tail_first, tail_rows):
    n_main = N_DIL if grouped else 0
    main_refs = refs[:n_main]
    tail_refs = refs[n_main:n_main + len(tail_chunks)]
    c = pl.program_id(0)
    i = pl.program_id(2)
    acc = jnp.dot(h_ref[...], w_ref[...], preferred_element_type=F32)
    cos, sin = cos_ref[...], sin_ref[...]
    first_half = (_iota((tile, LANES), 1) & (HEAD_DIM - 1)) < (HEAD_DIM // 2)
    blocks = []
    for j in range(D_MODEL // LANES):
        blk = acc[:, j * LANES:(j + 1) * LANES]
        partner = jnp.where(first_half, pltpu.roll(blk, LANES - HEAD_DIM // 2, 1), pltpu.roll(blk, HEAD_DIM // 2, 1))
        blocks.append(blk * cos + partner * sin)
    is_v = (c - 3 * (c // 3)) == 2
    res = jnp.where(is_v, acc, jnp.concatenate(blocks, axis=-1))
    for t, tref in enumerate(tail_refs):
        @pl.when((c == tail_chunks[t]) & (i >= tail_first[t]))
        def _(tref=tref, t=t):
            tref[...] = res[tile - tail_rows[t]:, :]
    if grouped:
        rbuf = refs[-1]
        n_blk = D_MODEL // LANES
        for j in range(n_blk):
            rbuf[j] = res[:, j * LANES:(j + 1) * LANES]
        for grp, (_, dil) in enumerate(DIL_GROUPS):
            @pl.when((c >= 3 * grp) & (c < 3 * grp + 3))
            def _(grp=grp, dil=dil):
                n = tile // dil
                for r in range(dil):
                    if dil == 1:
                        rows = res
                    else:
                        rows = jnp.concatenate([rbuf[j, pl.ds(r, n, stride=dil), :] for j in range(n_blk)], axis=-1)
                    main_refs[grp][r] = rows.astype(BF16)


def _qkv_proj(h, w, cos, sin, tail_keep, grouped, tile=ROW_TILE):
    b, s, d = h.shape
    tile = min(tile, s)
    nt = s // tile
    n_chunks = w.shape[1] // d
    tail_chunks = tuple(sorted(tail_keep))
    tail_rows = tuple(min(tail_keep[c], tile) for c in tail_chunks)
    tail_blocks = tuple(tail_keep[c] // r for c, r in zip(tail_chunks, tail_rows))
    tail_first = tuple(nt - nb for nb in tail_blocks)
    out_shape, out_specs = [], []
    if grouped:
        for grp, (_, dil) in enumerate(DIL_GROUPS):
            def gmap(c, bi, i, grp=grp):
                before, after = c < 3 * grp, c >= 3 * grp + 3
                jj = jnp.where(before, 0, jnp.where(after, 2, c - 3 * grp))
                bb = jnp.where(before, 0, jnp.where(after, b - 1, bi))
                ii = jnp.where(before, 0, jnp.where(after, nt - 1, i))
                return (jj, bb, 0, ii, 0)

            out_shape.append(jax.ShapeDtypeStruct((3, b, dil, s // dil, d), BF16))
            out_specs.append(pl.BlockSpec((None, None, dil, tile // dil, d), gmap))
    for c_t, rows, first, nb in zip(tail_chunks, tail_rows, tail_first, tail_blocks):
        def tmap(c, bi, i, c_t=c_t, first=first, nb=nb):
            before, after = c < c_t, c > c_t
            bb = jnp.where(before, 0, jnp.where(after, b - 1, bi))
            ii = jnp.where(before, 0, jnp.where(after, nb - 1, jnp.maximum(i - first, 0)))
            return (bb, ii, 0)

        out_shape.append(jax.ShapeDtypeStruct((b, tail_keep[c_t], d), F32))
        out_specs.append(pl.BlockSpec((None, rows, d), tmap))
    kern = functools.partial(_qkv_kernel, tile=tile, grouped=grouped, tail_chunks=tail_chunks, tail_first=tail_first,
                             tail_rows=tail_rows)
    return pl.pallas_call(
        kern,
        out_shape=tuple(out_shape),
        grid=(n_chunks, b, nt),
        in_specs=[
            pl.BlockSpec((None, tile, d), lambda c, bi, i: (bi, i, 0)),
            pl.BlockSpec((d, d), lambda c, bi, i: (0, c)),
            pl.BlockSpec((tile, LANES), lambda c, bi, i: (i, 0)),
            pl.BlockSpec((tile, LANES), lambda c, bi, i: (i, 0)),
        ],
        out_specs=tuple(out_specs),
        scratch_shapes=[pltpu.VMEM((d // LANES, tile, LANES), F32)] if grouped else [],
        compiler_params=_cparams("arbitrary", "arbitrary", "arbitrary"),
        name="qkv_proj",
    )(h, w, cos, sin)


def _attn_prompt_kernel(q_ref, kp_ref, kc_ref, vp_ref, vc_ref, o_ref, lse_ref):
    i = pl.program_id(2)
    tq = ATT_BAND
    lane = _iota((tq, LANES), 1)
    low = lane < HEAD_DIM
    qi = _iota((2 * tq, 2 * tq), 0) & (tq - 1)
    kj = _iota((2 * tq, 2 * tq), 1)
    mask = ((kj < tq) & (kj >= qi) & (i > 0)) | ((kj >= tq) & ((kj - tq) <= qi))
    lse_all = jnp.zeros((tq, LANES), F32)
    zero = jnp.zeros((), q_ref.dtype)
    for hp in range(N_HEADS // 2):
        sl = slice(hp * LANES, (hp + 1) * LANES)
        q = q_ref[:, sl]
        q2 = jnp.concatenate([jnp.where(low, q, zero), jnp.where(low, zero, q)], axis=0)
        kw = jnp.concatenate([kp_ref[:, sl], kc_ref[:, sl]], axis=0)
        vw = jnp.concatenate([vp_ref[:, sl], vc_ref[:, sl]], axis=0)
        s = lax.dot_general(q2, kw, _NT, preferred_element_type=F32) * (HEAD_DIM ** -0.5)
        s = jnp.where(mask, s, -jnp.inf)
        mx = jnp.max(s, axis=-1, keepdims=True)
        p = jnp.exp(s - mx)
        den = jnp.sum(p, axis=-1, keepdims=True)
        o2 = jnp.dot(p.astype(vw.dtype), vw, preferred_element_type=F32) * (1.0 / den)
        o_ref[:, sl] = jnp.where(low, o2[:tq], o2[tq:]).astype(o_ref.dtype)
        lse2 = mx + jnp.log(den)
        lse_all = jnp.where(lane == 2 * hp, lse2[:tq], jnp.where(lane == 2 * hp + 1, lse2[tq:], lse_all))
    lse_ref[...] = lse_all


def _attn_prompt(qkv_g, grp):
    _, b, dil, sub, d = qkv_g.shape
    tq = ATT_BAND
    cur = lambda which: pl.BlockSpec((None, None, None, tq, d), lambda bi, r, i: (which, bi, r, i, 0))
    prev = lambda which: pl.BlockSpec((None, None, None, tq, d), lambda bi, r, i: (which, bi, r, jnp.maximum(i - 1, 0), 0))
    return pl.pallas_call(
        _attn_prompt_kernel,
        out_shape=(jax.ShapeDtypeStruct((b, dil, sub, d), BF16), jax.ShapeDtypeStruct((b, dil, sub, LANES), F32)),
        grid=(b, dil, sub // tq),
        in_specs=[cur(0), prev(1), cur(1), prev(2), cur(2)],
        out_specs=(
            pl.BlockSpec((None, None, tq, d), lambda bi, r, i: (bi, r, i, 0)),
            pl.BlockSpec((None, None, tq, LANES), lambda bi, r, i: (bi, r, i, 0)),
        ),
        compiler_params=_cparams("arbitrary", "arbitrary", "arbitrary"),
        name=f"attn_prompt_g{grp}",
    )(qkv_g, qkv_g, qkv_g, qkv_g, qkv_g)


def _sample_pairs(n_new):
    pairs = []
    for grp, (_, dil) in enumerate(DIL_GROUPS):
        for t in range(n_new):
            for t2 in range(t + 1):
                if (t - t2) % dil == 0:
                    pairs.append((grp, t, t2))
    return pairs


def _sample_patterns(n_new):
    lanes = np.arange(LANES)
    pairs = _sample_pairs(n_new)
    assert len(pairs) <= LANES and N_DIL * n_new <= LANES
    spread = []
    for grp, (_, dil) in reversed(list(enumerate(DIL_GROUPS))):
        if dil == 1:
            continue
        col = grp * n_new + lanes % dil
        spread.append(((lanes % dil) < n_new)[None, :] & (np.arange(LANES)[:, None] == col[None, :]))
    for t in range(n_new):
        spread.append(np.broadcast_to((np.arange(LANES) == t)[:, None], (LANES, LANES)))
    qn = np.zeros((LANES, LANES), bool)
    kn = np.zeros((LANES, LANES), bool)
    fold_n = np.zeros((LANES, LANES), bool)
    for p, (grp, t, t2) in enumerate(pairs):
        qn[grp * n_new + t, p] = True
        kn[grp * n_new + t2, p] = True
        fold_n[p, t] = True
    fold = []
    for grp, (_, dil) in reversed(list(enumerate(DIL_GROUPS))):
        if dil == 1:
            continue
        fold.append((lanes[:, None] % dil) == np.arange(LANES)[None, :])
        fold[-1] = fold[-1] & (np.arange(LANES)[None, :] < n_new)
    for t in range(n_new):
        fold.append(np.broadcast_to((np.arange(LANES) == t)[None, :], (LANES, LANES)))
    fold.append(fold_n)
    as_bf16 = lambda m: jnp.asarray(m.astype(np.float32), BF16)
    return as_bf16(np.concatenate(spread + [qn], axis=1)), as_bf16(kn), as_bf16(np.concatenate(fold, axis=0))


def _dot_exact01(x, p):
    hi = x.astype(BF16)
    r1 = x - hi.astype(F32)
    mid = r1.astype(BF16)
    lo = (r1 - mid.astype(F32)).astype(BF16)
    dot = lambda a: jnp.dot(a, p, preferred_element_type=F32)
    return dot(hi) + dot(mid) + dot(lo)


def _attn_sample_kernel(qkn_ref, spread_ref, kn_ref, fold_ref, k0_ref, v0_ref, k1_ref, v1_ref, k2_ref, v2_ref, o_ref,
                        *, n_new):
    rows = k0_ref.shape[0]
    hpb = rows // HEAD_DIM
    scale = HEAD_DIM ** -0.5
    neg = F32(-jnp.inf)
    pairs = _sample_pairs(n_new)

    def columns(kind):
        x = qkn_ref[kind]
        return jnp.concatenate([x, jnp.zeros((LANES - x.shape[0], rows), F32)], axis=0).T

    q_pat = _dot_exact01(columns(0), spread_ref[...])
    kn_pat = _dot_exact01(columns(1), kn_ref[...])
    vn_pat = _dot_exact01(columns(2), kn_ref[...])
    pat = lambda j: q_pat[:, j * LANES:(j + 1) * LANES]
    hrow = _iota((hpb, LANES), 0)
    lane = _iota((hpb, LANES), 1)

    def head_sum(prod):
        out = jnp.zeros((hpb, LANES), F32)
        for h in range(hpb):
            out = jnp.where(hrow == h, jnp.sum(prod[h * HEAD_DIM:(h + 1) * HEAD_DIM, :], axis=0, keepdims=True), out)
        return out

    def head_rows(p):
        return jnp.concatenate([jnp.broadcast_to(p[h:h + 1, :], (HEAD_DIM, LANES)) for h in range(hpb)], axis=0)

    dil_refs = ((k2_ref, v2_ref, DIL_GROUPS[2][1]), (k1_ref, v1_ref, DIL_GROUPS[1][1]))
    s_dil = []
    for j, (k_ref, _, dil) in enumerate(dil_refs):
        tiles = [head_sum(k_ref[:, lt * LANES:(lt + 1) * LANES] * pat(j)) * scale for lt in range(k_ref.shape[1] // LANES)]
        s_dil.append(tiles)
    n_dilp = len(dil_refs)
    k0 = k0_ref[...]
    s_0 = [head_sum(k0 * pat(n_dilp + t)) * scale for t in range(n_new)]
    s_n = head_sum(kn_pat * pat(n_dilp + n_new)) * scale
    pair_mask = []
    for t in range(n_new):
        mk = lane < 0
        for p, (_, pt, _) in enumerate(pairs):
            if pt == t:
                mk = mk | (lane == p)
        pair_mask.append(mk)
    m_t = []
    for t in range(n_new):
        mx = jnp.max(jnp.where(lane >= t, s_0[t], neg), axis=-1, keepdims=True)
        mx = jnp.maximum(mx, jnp.max(jnp.where(pair_mask[t], s_n, neg), axis=-1, keepdims=True))
        for (_, _, dil), tiles in zip(dil_refs, s_dil):
            sel = (lane & (dil - 1)) == t
            tm = functools.reduce(jnp.maximum, tiles)
            mx = jnp.maximum(mx, jnp.max(jnp.where(sel, tm, neg), axis=-1, keepdims=True))
        m_t.append(mx)
    den_t = [jnp.zeros((hpb, 1), F32) for _ in range(n_new)]
    folded = []
    for (_, v_ref, dil), tiles in zip(dil_refs, s_dil):
        res = lane & (dil - 1)
        m_lane = functools.reduce(lambda a, b: a + b, [jnp.where(res == t, m_t[t], 0.0) for t in range(n_new)])
        acc = jnp.zeros((rows, LANES), F32)
        psum = jnp.zeros((hpb, LANES), F32)
        for lt, s in enumerate(tiles):
            p = jnp.where(res < n_new, jnp.exp(s - m_lane), 0.0)
            psum = psum + p
            acc = acc + v_ref[:, lt * LANES:(lt + 1) * LANES] * head_rows(p)
        for t in range(n_new):
            den_t[t] = den_t[t] + jnp.sum(jnp.where(res == t, psum, 0.0), axis=-1, keepdims=True)
        folded.append(acc)
    v0 = v0_ref[...]
    for t in range(n_new):
        p = jnp.where(lane >= t, jnp.exp(s_0[t] - m_t[t]), 0.0)
        den_t[t] = den_t[t] + jnp.sum(p, axis=-1, keepdims=True)
        folded.append(v0 * head_rows(p))
    m_pair = functools.reduce(lambda a, b: a + b, [jnp.where(pair_mask[t], m_t[t], 0.0) for t in range(n_new)])
    any_pair = functools.reduce(lambda a, b: a | b, pair_mask)
    p_n = jnp.where(any_pair, jnp.exp(s_n - m_pair), 0.0)
    for t in range(n_new):
        den_t[t] = den_t[t] + jnp.sum(jnp.where(pair_mask[t], p_n, 0.0), axis=-1, keepdims=True)
    folded.append(vn_pat * head_rows(p_n))
    out_t = _dot_exact01(jnp.concatenate(folded, axis=-1), fold_ref[...])
    den = functools.reduce(lambda a, b: a + b, [jnp.where(lane == t, den_t[t], 0.0) for t in range(n_new)])
    den = jnp.where(lane < n_new, den, 1.0)
    out_t = out_t * head_rows(1.0 / den)
    o_ref[...] = out_t.T[0:n_new, :]


def _attn_sample(qkn, caches_t, n_new, rows=512):
    b, d = qkn.shape[0], qkn.shape[-1]
    spread, kn, fold = _sample_patterns(n_new)
    cache_args, cache_specs = [], []
    for (win, dil), (ck, cv) in zip(DIL_GROUPS, caches_t):
        assert ck.shape == (b, d, win) and win // dil == ATT_BAND and (dil == 1 or dil >= n_new)
        for c in (ck, cv):
            cache_args.append(c)
            cache_specs.append(pl.BlockSpec((None, rows, win), lambda bi, hh: (bi, hh, 0)))
    const = lambda a: pl.BlockSpec(a.shape, lambda bi, hh: (0, 0))
    return pl.pallas_call(
        functools.partial(_attn_sample_kernel, n_new=n_new),
        out_shape=jax.ShapeDtypeStruct((b, n_new, d), F32),
        grid=(b, d // rows),
        in_specs=[pl.BlockSpec((None, 3, qkn.shape[2], rows), lambda bi, hh: (bi, 0, 0, hh)),
                  const(spread), const(kn), const(fold)] + cache_specs,
        out_specs=pl.BlockSpec((None, n_new, rows), lambda bi, hh: (bi, 0, hh)),
        compiler_params=_cparams("arbitrary", "arbitrary"),
        name="attn_sample",
    )(qkn, spread, kn, fold, *cache_args)


def _oproj_kernel(*refs, dils, n_lse, alpha, tile):
    n_groups = len(dils)
    o_refs = refs[:n_groups]
    lse_refs = refs[n_groups:n_groups + n_lse]
    k = n_groups + n_lse
    x_ref, m_ref, wo_ref, hx_ref, lng_ref, lnb_ref, wr_ref, br_ref = refs[k:k + 8]
    x1_ref, hrow_ref, ext_ref = refs[k + 8:k + 11]
    obuf, lbuf = refs[k + 11:]

    def natural(ref, buf, g, dil):
        if dil == 1:
            return ref[0].astype(F32)
        n_blk = ref.shape[-1] // LANES
        for r in range(dil):
            rows = ref[r].astype(F32)
            for j in range(n_blk):
                buf[g * n_blk + j, pl.ds(r, tile // dil, stride=dil), :] = rows[:, j * LANES:(j + 1) * LANES]
        return jnp.concatenate([buf[g * n_blk + j] for j in range(n_blk)], axis=-1)

    os_ = [natural(o_refs[g], obuf, g, dils[g]) for g in range(n_groups)]
    if n_lse == 0:
        o = os_[0].astype(BF16)
    else:
        lses = [natural(lse_refs[g], lbuf, g, dils[g]) for g in range(n_groups)]
        mx = functools.reduce(jnp.maximum, lses)
        es = [jnp.exp(l - mx) for l in lses]
        inv = 1.0 / functools.reduce(lambda a, b: a + b, es)
        o = jnp.zeros(x_ref.shape, F32)
        hx = hx_ref[...]
        for e, og in zip(es, os_):
            wgt = e * inv
            hi = wgt.astype(BF16)
            lo = (wgt - hi.astype(F32)).astype(BF16)
            wexp = jnp.dot(hi, hx, preferred_element_type=F32) + jnp.dot(lo, hx, preferred_element_type=F32)
            o = o + wexp * og
        o = o.astype(BF16)
    y = jnp.dot(o, wo_ref[...], preferred_element_type=F32)
    x1, h2, ext = _post_mixer(x_ref[...], y, m_ref[...], lng_ref, lnb_ref, wr_ref, br_ref, alpha)
    x1_ref[...] = x1
    hrow_ref[:, 0:D_MODEL] = h2
    hrow_ref[:, D_MODEL:] = ext
    ext_ref[...] = ext


def _out_proj(os_, lses, x, m, w_o, head_expand, ln_g, ln_b, wr, br, alpha, per_token):
    b, s, d = x.shape
    tile = min(ROW_TILE, s)
    nt = s // tile
    dils = tuple(o.shape[1] for o in os_)
    tok = pl.BlockSpec((None, tile, d), lambda bi, i: (bi, i, 0))
    grouped = lambda dil, w: pl.BlockSpec((None, dil, tile // dil, w), lambda bi, i: (bi, 0, i, 0))
    full = lambda bi, i: (0, 0)
    return pl.pallas_call(
        functools.partial(_oproj_kernel, dils=dils, n_lse=len(lses), alpha=alpha, tile=tile),
        out_shape=(
            jax.ShapeDtypeStruct((b, s, d), F32),
            jax.ShapeDtypeStruct((b * s, ROW_W), F32),
            jax.ShapeDtypeStruct((b * s, EXT), F32),
        ),
        grid=(b, nt),
        in_specs=[grouped(dil, d) for dil in dils] + [grouped(dil, LANES) for dil in dils[:len(lses)]] + [
            tok, _mod_spec(tile, d, per_token),
            pl.BlockSpec((d, d), full),
            pl.BlockSpec((LANES, d), full),
            pl.BlockSpec((1, d), full),
            pl.BlockSpec((1, d), full),
            pl.BlockSpec((d, EXT), full),
            pl.BlockSpec((1, EXT), full),
        ],
        out_specs=(
            tok,
            pl.BlockSpec((tile, ROW_W), lambda bi, i: (bi * nt + i, 0)),
            pl.BlockSpec((tile, EXT), lambda bi, i: (bi * nt + i, 0)),
        ),
        scratch_shapes=[pltpu.VMEM((len(dils) * d // LANES, tile, LANES), F32), pltpu.VMEM((len(dils), tile, LANES), F32)],
        compiler_params=_cparams("arbitrary", "arbitrary"),
        name="out_proj",
    )(*os_, *lses, x, m, w_o, head_expand, ln_g, ln_b, wr, br)


def _router_matrix(w_group, b_group, w_er, b_er):
    d = w_group.shape[0]
    n_e = N_EXPERT_GROUPS * EXPERTS_PER_GROUP
    w_e = jnp.transpose(w_er, (1, 0, 2)).reshape(d, n_e)
    pad = EXT - N_EXPERT_GROUPS - n_e
    wr = jnp.concatenate([w_group, w_e, jnp.zeros((d, pad), F32)], axis=1)
    br = jnp.concatenate([b_group, b_er.reshape(n_e), jnp.zeros((pad,), F32)])[None, :]
    return wr, br


def _moe_experts(rows_p, rows_s, ext_p, ext_s, w1, w3, w2):
    dest, tile_group, n_used, n_tiles = _group_sort(ext_p, ext_s)
    sorted_rows = _scatter_rows(dest, rows_p, rows_s, n_tiles * MOE_TILE)
    return dest, _moe_ffn(tile_group, n_used, sorted_rows, w1, w3, w2, n_tiles)


def kernel(x_prompt, x_sample, state_pool, cache_k_w128, cache_v_w128, cache_k_w512, cache_v_w512, cache_k_w2048, cache_v_w2048, c_prompt, c_sample, mod_w, mod_b, ln1_g, ln1_b, ln2_g, ln2_b, pool_w, pool_scale, attn_w_qkv, attn_w_o, moe_w_group, moe_b_group, moe_w_expert_router, moe_b_expert_router, moe_w1, moe_w3, moe_w2):
    depth = mod_w.shape[0]
    alpha = float((2.0 * depth) ** 0.25)
    bp, s, d = x_prompt.shape
    bs, t_new, _ = x_sample.shape
    n_p, n_s = bp * s, bs * t_new
    kv_caches = ((cache_k_w128, cache_v_w128), (cache_k_w512, cache_v_w512), (cache_k_w2048, cache_v_w2048))

    c_all = jnp.concatenate([jnp.repeat(c_sample, t_new, axis=0), c_prompt,
                             jnp.zeros((SUBLANES - bp % SUBLANES, d), F32)], axis=0)
    m_all = _modulation(c_all, mod_w, mod_b)
    m_tok = [m_all[i, :n_s].reshape(1, n_s, 6 * d) for i in range(depth)]
    m_seq = [m_all[i, n_s:n_s + bp].reshape(bp, 1, 6 * d) for i in range(depth)]

    head_expand = jnp.asarray(np.arange(LANES)[:, None] == (np.arange(d)[None, :] // HEAD_DIM), BF16)
    cos_p, sin_p = _rope_tables(jnp.arange(s, dtype=jnp.int32))
    cos_s, sin_s = _rope_tables(PAST_LEN + (jnp.arange(n_s, dtype=jnp.int32) % t_new))

    xp, xs = x_prompt, x_sample.reshape(1, n_s, d)
    hp = hs = None
    pool_p, pool_s = [], []
    kv_p = [[] for _ in range(2 * N_DIL)]
    kv_s = [[] for _ in range(2 * N_DIL)]
    for i in range(depth):
        li = i // 2
        wr, br = _router_matrix(moe_w_group[i], moe_b_group[i], moe_w_expert_router[i], moe_b_expert_router[i])
        ln1g, ln1b = ln1_g[i][None, :], ln1_b[i][None, :]
        if i % 2 == 0:
            pw = pool_w[li].astype(BF16)
            ps = pool_scale[li][None, :]
            xp, rows_p, ext_p, tail = _pool_prompt(xp, m_seq[i], pw, ps, ln1g, ln1b, wr, br, alpha)
            x1s, rows_s, ext_s, znew = _pool_sample(
                state_pool[li].reshape(bs, POOL_BUF * d), xs.reshape(bs, t_new * d),
                m_tok[i].reshape(bs, t_new * 6 * d), pw, ps, ln1g, ln1b, wr, br, alpha)
            xs = x1s.reshape(1, n_s, d)
            rows_s = rows_s.reshape(n_s, ROW_W)
            ext_s = ext_s.reshape(n_s, EXT)
            pool_p.append(tail)
            pool_s.append(znew.reshape(bs, t_new, d))
        else:
            wqkv = attn_w_qkv[li].astype(BF16)
            wo = attn_w_o[li].astype(BF16)
            kv_chunks = [3 * g + j for g in range(N_DIL) for j in (1, 2)]
            keep_p = {3 * g + j: min(win, s) for g, (win, _) in enumerate(DIL_GROUPS) for j in (1, 2)}
            *qkv_groups, = _qkv_proj(hp, wqkv, cos_p, sin_p, keep_p, grouped=True)
            tails_p = qkv_groups[N_DIL:]
            rows_all = _qkv_proj(hs, wqkv, cos_s, sin_s, {c: n_s for c in range(3 * N_DIL)}, grouped=False)
            for j, c in enumerate(kv_chunks):
                kv_p[j].append(tails_p[j].reshape(bp, -1, N_HEADS, HEAD_DIM))
                kv_s[j].append(rows_all[c].reshape(bs, t_new, N_HEADS, HEAD_DIM))
            outs = [_attn_prompt(qkv_groups[g], g) for g in range(N_DIL)]
            xp, rows_p, ext_p = _out_proj([o for o, _ in outs], [l for _, l in outs], xp, m_seq[i], wo, head_expand,
                                          ln1g, ln1b, wr, br, alpha, per_token=False)
            kinds = [jnp.concatenate([rows_all[3 * g + j].reshape(bs, t_new, d) for g in range(N_DIL)], axis=1)
                     for j in range(3)]
            qkn = jnp.pad(jnp.stack(kinds, axis=1), ((0, 0), (0, 0), (0, 2 * SUBLANES - N_DIL * t_new), (0, 0)))
            caches_t = [tuple(jnp.transpose(c[li], (0, 2, 3, 1)).reshape(bs, d, c.shape[2]) for c in kv)
                        for kv in kv_caches]
            o_s = _attn_sample(qkn, caches_t, t_new)
            xs, rows_s, ext_s = _out_proj([o_s.reshape(1, 1, n_s, d)], [], xs, m_tok[i], wo, head_expand,
                                          ln1g, ln1b, wr, br, alpha, per_token=True)
        dest, f_sorted = _moe_experts(rows_p, rows_s, ext_p, ext_s,
                                      moe_w1[i].astype(BF16), moe_w3[i].astype(BF16), moe_w2[i].astype(BF16))
        nxt = min(i + 1, depth - 1)
        ln2g, ln2b = ln2_g[i][None, :], ln2_b[i][None, :]
        xp, hp = _ln2(dest, xp, f_sorted, 0, m_seq[i], m_seq[nxt], ln2g, ln2b, alpha, per_token=False)
        xs, hs = _ln2(dest, xs, f_sorted, n_p, m_tok[i], m_tok[nxt], ln2g, ln2b, alpha, per_token=True)

    stack = lambda lst: jnp.stack(lst, axis=0)
    return (xp, xs.reshape(bs, t_new, d), stack(pool_p), stack(pool_s),
            *[stack(kv_p[j]) for j in range(2 * N_DIL)], *[stack(kv_s[j]) for j in range(2 * N_DIL)])
```

```python
import functools

import numpy as np
import jax
import jax.numpy as jnp
from jax import lax
from jax.experimental import pallas as pl
from jax.experimental.pallas import tpu as pltpu

F32 = jnp.float32
BF16 = jnp.bfloat16
HIGHEST = lax.Precision.HIGHEST

D_MODEL = 1024
POOL_WINDOWS = (2, 4, 8, 16)
N_POOL_GROUPS = len(POOL_WINDOWS)
POOL_GROUP = D_MODEL // N_POOL_GROUPS
POOL_BUF = max(POOL_WINDOWS) - 1
POOL_HALO = 16
HEAD_DIM = 64
N_HEADS = D_MODEL // HEAD_DIM
DIL_GROUPS = ((128, 1), (512, 4), (2048, 16))
N_DIL = len(DIL_GROUPS)
ATT_BAND = 128
ROPE_THETA = 10000.0
N_EXPERT_GROUPS = 4
EXPERTS_PER_GROUP = 8
PAST_LEN = 2048
LN_EPS = 1e-5

LANES = 128
SUBLANES = 8
VMEM_LIMIT_BYTES = 56 * 1024 * 1024

EXT = LANES
EXT_EXPERT0 = N_EXPERT_GROUPS
EXT_ONEHOT0 = 120
ROW_W = D_MODEL + EXT

MOE_TILE = 512
SORT_CHUNK = 512
ROW_TILE = 512
N_DMA_PRIORITIES = 2

_NT = (((1,), (1,)), ((), ()))


def _cparams(*sem):
    return pltpu.CompilerParams(dimension_semantics=sem, vmem_limit_bytes=VMEM_LIMIT_BYTES)


def _iota(shape, dim):
    return lax.broadcasted_iota(jnp.int32, shape, dim)


def _mod_kernel(c_ref, w_ref, b_ref, o_ref):
    o_ref[...] = jnp.dot(c_ref[...], w_ref[...], precision=HIGHEST, preferred_element_type=F32) + b_ref[...]


def _modulation(c_all, mod_w, mod_b):
    n_layers, d, n_out = mod_w.shape
    rows = c_all.shape[0]
    tn = n_out // 4
    return pl.pallas_call(
        _mod_kernel,
        out_shape=jax.ShapeDtypeStruct((n_layers, rows, n_out), F32),
        grid=(n_layers, n_out // tn),
        in_specs=[
            pl.BlockSpec((rows, d), lambda l, j: (0, 0)),
            pl.BlockSpec((None, d, tn), lambda l, j: (l, 0, j)),
            pl.BlockSpec((None, 1, tn), lambda l, j: (l, 0, j)),
        ],
        out_specs=pl.BlockSpec((None, rows, tn), lambda l, j: (l, 0, j)),
        compiler_params=_cparams("arbitrary", "arbitrary"),
        name="modulation",
    )(c_all, mod_w, mod_b.reshape(n_layers, 1, n_out))


def _mod_chunk(m, j):
    return m[:, j * D_MODEL:(j + 1) * D_MODEL]


def _layer_norm(u, g, b):
    mu = jnp.mean(u, axis=-1, keepdims=True)
    uc = u - mu
    var = jnp.mean(uc * uc, axis=-1, keepdims=True)
    return uc * lax.rsqrt(var + LN_EPS) * g + b


def _route(h2, wr_ref, br_ref):
    h_hi = h2.astype(BF16)
    h_lo = (h2 - h_hi.astype(F32)).astype(BF16)
    dot = lambda a, b: jnp.dot(a, b, preferred_element_type=F32)
    logits = dot(h_hi, wr_ref[0]) + dot(h_lo, wr_ref[0]) + dot(h_hi, wr_ref[1]) + br_ref[...]
    lane_f = _iota(logits.shape, 1).astype(F32)
    neg = F32(-jnp.inf)
    big = F32(EXT)
    gl = jnp.where(lane_f < N_EXPERT_GROUPS, logits, neg)
    gmax = jnp.max(gl, axis=-1, keepdims=True)
    gidx = jnp.min(jnp.where(gl == gmax, lane_f, big), axis=-1, keepdims=True)
    g_p = 1.0 / jnp.sum(jnp.exp(gl - gmax), axis=-1, keepdims=True)
    e_lo = EXT_EXPERT0 + EXPERTS_PER_GROUP * gidx
    el = jnp.where((lane_f >= e_lo) & (lane_f < e_lo + EXPERTS_PER_GROUP), logits, neg)
    v1 = jnp.max(el, axis=-1, keepdims=True)
    i1 = jnp.min(jnp.where(el == v1, lane_f, big), axis=-1, keepdims=True)
    el2 = jnp.where(lane_f == i1, neg, el)
    v2 = jnp.max(el2, axis=-1, keepdims=True)
    i2 = jnp.min(jnp.where(el2 == v2, lane_f, big), axis=-1, keepdims=True)
    r = jnp.exp(v2 - v1)
    w1 = g_p / (1.0 + r)
    w2 = w1 * r
    ext = jnp.where(lane_f == i1, w1, 0.0) + jnp.where(lane_f == i2, w2, 0.0)
    return ext + jnp.where(lane_f == gidx + EXT_ONEHOT0, 1.0, 0.0)


def _post_mixer(x, y, m, lng_ref, lnb_ref, wr_ref, br_ref, alpha):
    x1 = _layer_norm(alpha * x + _mod_chunk(m, 2) * y, lng_ref[...], lnb_ref[...])
    h2 = x1 * (1.0 + _mod_chunk(m, 4)) + _mod_chunk(m, 3)
    return x1, h2, _route(h2, wr_ref, br_ref)


def _pool_prompt_kernel(x_ref, xprev_ref, m_ref, pw_ref, ps_ref, lng_ref, lnb_ref, wr_ref, br_ref,
                        x1_ref, hrow_ref, ext_ref, tail_ref, zbuf, *, tile, alpha):
    i = pl.program_id(1)
    m = m_ref[...]
    shift, scale = _mod_chunk(m, 0), _mod_chunk(m, 1)
    x = x_ref[...]
    z = x * (1.0 + scale) + shift
    zprev = xprev_ref[...] * (1.0 + scale) + shift
    zbuf[0:POOL_HALO, :] = jnp.where(i > 0, zprev, 0.0)
    zbuf[POOL_HALO:, :] = z
    pos = i * tile + _iota((tile, 1), 0)
    ys = []
    for g, w in enumerate(POOL_WINDOWS):
        cols = slice(g * POOL_GROUP, (g + 1) * POOL_GROUP)
        win = zbuf[pl.ds(POOL_HALO, tile), cols]
        for j in range(1, w):
            win = win + zbuf[pl.ds(POOL_HALO - j, tile), cols]
        cnt = jnp.minimum(pos + 1, w).astype(F32)
        d = win / cnt - z[:, cols]
        ys.append(jnp.dot(d.astype(BF16), pw_ref[g], preferred_element_type=F32))
    y = jnp.concatenate(ys, axis=-1) * ps_ref[...]
    x1, h2, ext = _post_mixer(x, y, m, lng_ref, lnb_ref, wr_ref, br_ref, alpha)
    x1_ref[...] = x1
    hrow_ref[:, 0:D_MODEL] = h2
    hrow_ref[:, D_MODEL:] = ext
    ext_ref[...] = ext

    @pl.when(i == pl.num_programs(1) - 1)
    def _():
        tail_ref[...] = zbuf[pl.ds(POOL_HALO + tile - POOL_BUF, POOL_BUF), :]


def _pool_prompt(x, m, pool_w, pool_scale, ln_g, ln_b, wr, br, alpha, tile=256):
    b, s, d = x.shape
    nt = s // tile
    halo_blocks = tile // POOL_HALO
    full = lambda bi, i: (0, 0)
    kern = functools.partial(_pool_prompt_kernel, tile=tile, alpha=alpha)
    return pl.pallas_call(
        kern,
        out_shape=(
            jax.ShapeDtypeStruct((b, s, d), F32),
            jax.ShapeDtypeStruct((b * s, ROW_W), F32),
            jax.ShapeDtypeStruct((b * s, EXT), F32),
            jax.ShapeDtypeStruct((b, POOL_BUF, d), F32),
        ),
        grid=(b, nt),
        in_specs=[
            pl.BlockSpec((None, tile, d), lambda bi, i: (bi, i, 0)),
            pl.BlockSpec((None, POOL_HALO, d), lambda bi, i: (bi, jnp.maximum(i * halo_blocks - 1, 0), 0)),
            pl.BlockSpec((None, 1, 6 * d), lambda bi, i: (bi, 0, 0)),
            pl.BlockSpec((N_POOL_GROUPS, POOL_GROUP, POOL_GROUP), lambda bi, i: (0, 0, 0)),
            pl.BlockSpec((1, d), full),
            pl.BlockSpec((1, d), full),
            pl.BlockSpec((1, d), full),
            pl.BlockSpec((2, d, EXT), lambda *_: (0, 0, 0)),
            pl.BlockSpec((1, EXT), full),
        ],
        out_specs=(
            pl.BlockSpec((None, tile, d), lambda bi, i: (bi, i, 0)),
            pl.BlockSpec((tile, ROW_W), lambda bi, i: (bi * nt + i, 0)),
            pl.BlockSpec((tile, EXT), lambda bi, i: (bi * nt + i, 0)),
            pl.BlockSpec((None, POOL_BUF, d), lambda bi, i: (bi, 0, 0)),
        ),
        scratch_shapes=[pltpu.VMEM((tile + POOL_HALO, d), F32)],
        compiler_params=_cparams("arbitrary", "arbitrary"),
        name="pool_prompt",
    )(x, x, m, pool_w, pool_scale, ln_g, ln_b, wr, br)


def _pool_sample_kernel(st_ref, x_ref, m_ref, pw_ref, ps_ref, lng_ref, lnb_ref, wr_ref, br_ref,
                        x1_ref, hrow_ref, ext_ref, znew_ref, *, n_new, alpha):
    d = D_MODEL
    rows = [st_ref[:, r * d:(r + 1) * d] for r in range(POOL_BUF)]
    xs = []
    m = m_ref[...]
    for t in range(n_new):
        x = x_ref[:, t * d:(t + 1) * d]
        z = x * (1.0 + _mod_chunk(m, 1)) + _mod_chunk(m, 0)
        znew_ref[:, t * d:(t + 1) * d] = z
        rows.append(z)
        xs.append(x)
    for t in range(n_new):
        last = POOL_BUF + t
        ys = []
        for g, w in enumerate(POOL_WINDOWS):
            cols = slice(g * POOL_GROUP, (g + 1) * POOL_GROUP)
            win = rows[last][:, cols]
            for j in range(1, w):
                win = win + rows[last - j][:, cols]
            dgrp = win / F32(w) - rows[last][:, cols]
            ys.append(jnp.dot(dgrp.astype(BF16), pw_ref[g], preferred_element_type=F32))
        y = jnp.concatenate(ys, axis=-1) * ps_ref[...]
        x1, h2, ext = _post_mixer(xs[t], y, m, lng_ref, lnb_ref, wr_ref, br_ref, alpha)
        x1_ref[:, t * d:(t + 1) * d] = x1
        hrow_ref[:, t * ROW_W:t * ROW_W + d] = h2
        hrow_ref[:, t * ROW_W + d:(t + 1) * ROW_W] = ext
        ext_ref[:, t * EXT:(t + 1) * EXT] = ext


def _pool_sample(state, x, m, pool_w, pool_scale, ln_g, ln_b, wr, br, alpha, bb=32):
    b = x.shape[0]
    d = D_MODEL
    n_new = x.shape[1] // d
    full = lambda i: (0, 0)
    kern = functools.partial(_pool_sample_kernel, n_new=n_new, alpha=alpha)
    return pl.pallas_call(
        kern,
        out_shape=(
            jax.ShapeDtypeStruct((b, n_new * d), F32),
            jax.ShapeDtypeStruct((b, n_new * ROW_W), F32),
            jax.ShapeDtypeStruct((b, n_new * EXT), F32),
            jax.ShapeDtypeStruct((b, n_new * d), F32),
        ),
        grid=(b // bb,),
        in_specs=[
            pl.BlockSpec((bb, POOL_BUF * d), lambda i: (i, 0)),
            pl.BlockSpec((bb, n_new * d), lambda i: (i, 0)),
            pl.BlockSpec((bb, 6 * d), lambda i: (i, 0)),
            pl.BlockSpec((N_POOL_GROUPS, POOL_GROUP, POOL_GROUP), lambda i: (0, 0, 0)),
            pl.BlockSpec((1, d), full),
            pl.BlockSpec((1, d), full),
            pl.BlockSpec((1, d), full),
            pl.BlockSpec((2, d, EXT), lambda *_: (0, 0, 0)),
            pl.BlockSpec((1, EXT), full),
        ],
        out_specs=(
            pl.BlockSpec((bb, n_new * d), lambda i: (i, 0)),
            pl.BlockSpec((bb, n_new * ROW_W), lambda i: (i, 0)),
            pl.BlockSpec((bb, n_new * EXT), lambda i: (i, 0)),
            pl.BlockSpec((bb, n_new * d), lambda i: (i, 0)),
        ),
        compiler_params=_cparams("arbitrary"),
        name="pool_sample",
    )(state, x, m, pool_w, pool_scale, ln_g, ln_b, wr, br)


def _sort_kernel(extp_ref, exts_ref, dest_ref, meta_ref, oh_ref, *, n_chunks_p, n_chunks_s):
    ch = SORT_CHUNK
    r_io = _iota((SUBLANES, EXT), 0)
    l_io = _iota((SUBLANES, EXT), 1)
    sel = jnp.where((l_io == r_io + EXT_ONEHOT0) & (r_io < N_EXPERT_GROUPS), 1.0, 0.0).astype(BF16)

    def count_from(ext_ref, chunk0):
        def body(c, cnt):
            ext = ext_ref[pl.ds(pl.multiple_of(c * ch, ch), ch), :]
            oh = lax.dot_general(sel, ext.astype(BF16), _NT, preferred_element_type=F32)
            oh_ref[chunk0 + c] = oh
            return cnt + jnp.sum(oh, axis=-1, keepdims=True)
        return body

    counts = lax.fori_loop(0, n_chunks_p, count_from(extp_ref, 0), jnp.zeros((SUBLANES, 1), F32))
    counts = lax.fori_loop(0, n_chunks_s, count_from(exts_ref, n_chunks_p), counts)
    padded = jnp.floor((counts + (MOE_TILE - 1)) * (1.0 / MOE_TILE)) * MOE_TILE
    row = _iota((SUBLANES, 1), 0)
    starts = jnp.zeros((SUBLANES, 1), F32)
    for g in range(1, N_EXPERT_GROUPS):
        starts = starts + jnp.where(row >= g, padded[g - 1:g, :], 0.0)
    tri = jnp.where(_iota((ch, ch), 0) < _iota((ch, ch), 1), 1.0, 0.0).astype(BF16)

    def dest_body(c, base):
        oh = oh_ref[c]
        pre = jnp.dot(oh.astype(BF16), tri, preferred_element_type=F32)
        dest = jnp.sum(oh * (base + pre), axis=0, keepdims=True)
        dest_ref[pl.ds(c, 1), :] = dest.astype(jnp.int32)
        return base + jnp.sum(oh, axis=-1, keepdims=True)

    lax.fori_loop(0, n_chunks_p + n_chunks_s, dest_body, starts)
    ends = starts + padded
    mrow = _iota((SUBLANES, EXT), 0)
    tile_lo = (_iota((SUBLANES, EXT), 1) * MOE_TILE).astype(F32)
    tgroup = jnp.sum(jnp.where((mrow < N_EXPERT_GROUPS - 1) & (tile_lo >= ends), 1.0, 0.0), axis=0, keepdims=True)
    n_used = jnp.sum(jnp.where(row < N_EXPERT_GROUPS, padded, 0.0), axis=0, keepdims=True) * (1.0 / MOE_TILE)
    meta = jnp.where(mrow == 0, tgroup, jnp.where(mrow == 1, n_used, 0.0))
    meta_ref[...] = meta.astype(jnp.int32)


def _group_sort(ext_p, ext_s):
    n_p, n_s = ext_p.shape[0], ext_s.shape[0]
    ncp, ncs = n_p // SORT_CHUNK, n_s // SORT_CHUNK
    n_tiles = (n_p + n_s) // MOE_TILE + N_EXPERT_GROUPS
    assert n_tiles <= EXT
    dest, meta = pl.pallas_call(
        functools.partial(_sort_kernel, n_chunks_p=ncp, n_chunks_s=ncs),
        out_shape=(
            jax.ShapeDtypeStruct((ncp + ncs, SORT_CHUNK), jnp.int32),
            jax.ShapeDtypeStruct((SUBLANES, EXT), jnp.int32),
        ),
        grid=(1,),
        in_specs=[pl.BlockSpec((n_p, EXT), lambda i: (0, 0)), pl.BlockSpec((n_s, EXT), lambda i: (0, 0))],
        out_specs=(
            pl.BlockSpec((ncp + ncs, SORT_CHUNK), lambda i: (0, 0)),
            pl.BlockSpec((SUBLANES, EXT), lambda i: (0, 0)),
        ),
        scratch_shapes=[pltpu.VMEM((ncp + ncs, SUBLANES, SORT_CHUNK), F32)],
        compiler_params=_cparams("arbitrary"),
        name="group_sort",
    )(ext_p, ext_s)
    return dest.reshape(-1), meta[0, :n_tiles], meta[1, :1], n_tiles


def _scatter_kernel(dest_ref, rp_ref, rs_ref, init_ref, out_ref, sem, *, tiles_p):
    del init_ref
    i = pl.program_id(0)
    base = i * ROW_TILE

    def run(src_ref):
        def issue(k2, a):
            for prio in range(N_DMA_PRIORITIES):
                k = N_DMA_PRIORITIES * k2 + prio
                pltpu.make_async_copy(src_ref.at[pl.ds(k, 1)], out_ref.at[pl.ds(dest_ref[base + k], 1)],
                                      sem).start(priority=prio)
            return a

        def wait(k, a):
            pltpu.make_async_copy(src_ref.at[pl.ds(0, 1)], out_ref.at[pl.ds(0, 1)], sem).wait()
            return a

        lax.fori_loop(0, ROW_TILE // N_DMA_PRIORITIES, issue, 0, unroll=4)
        lax.fori_loop(0, ROW_TILE, wait, 0, unroll=8)

    @pl.when(i < tiles_p)
    def _():
        run(rp_ref)

    @pl.when(i >= tiles_p)
    def _():
        run(rs_ref)


def _scatter_rows(dest, rows_p, rows_s, n_pad):
    n_p, n_s = rows_p.shape[0], rows_s.shape[0]
    w = rows_p.shape[1]
    tiles_p, tiles_s = n_p // ROW_TILE, n_s // ROW_TILE
    init = jnp.zeros((n_pad, w), rows_p.dtype)
    any_spec = pl.BlockSpec(memory_space=pl.ANY)
    return pl.pallas_call(
        functools.partial(_scatter_kernel, tiles_p=tiles_p),
        out_shape=jax.ShapeDtypeStruct((n_pad, w), rows_p.dtype),
        grid_spec=pltpu.PrefetchScalarGridSpec(
            num_scalar_prefetch=1,
            grid=(tiles_p + tiles_s,),
            in_specs=[
                pl.BlockSpec((ROW_TILE, w), lambda i, dst: (jnp.minimum(i, tiles_p - 1), 0)),
                pl.BlockSpec((ROW_TILE, w), lambda i, dst: (jnp.maximum(i - tiles_p, 0), 0)),
                any_spec,
            ],
            out_specs=any_spec,
            scratch_shapes=[pltpu.SemaphoreType.DMA(())],
        ),
        input_output_aliases={3: 0},
        compiler_params=_cparams("arbitrary"),
        name="scatter_rows",
    )(dest, rows_p, rows_s, init)


def _ffn_kernel(tg_ref, nu_ref, rows_ref, w1_ref, w3_ref, w2_ref, out_ref):
    i = pl.program_id(0)

    @pl.when(i < nu_ref[0])
    def _():
        x = rows_ref[:, 0:D_MODEL].astype(BF16)
        ext = rows_ref[:, D_MODEL:]
        lane = _iota(ext.shape, 1)
        base = EXT_EXPERT0 + EXPERTS_PER_GROUP * tg_ref[i]
        acc = jnp.zeros(out_ref.shape, F32)
        for e in range(EXPERTS_PER_GROUP):
            a = jnp.dot(x, w1_ref[e], preferred_element_type=F32)
            b = jnp.dot(x, w3_ref[e], preferred_element_type=F32)
            comb = jnp.sum(jnp.where(lane == base + e, ext, 0.0), axis=-1, keepdims=True)
            hid = (a * jax.nn.sigmoid(a)) * b * comb
            acc = acc + jnp.dot(hid.astype(BF16), w2_ref[e], preferred_element_type=F32)
        out_ref[...] = acc

    @pl.when(i >= nu_ref[0])
    def _():
        out_ref[...] = jnp.zeros(out_ref.shape, F32)


def _moe_ffn(tile_group, n_used, rows_sorted, w1, w3, w2, n_tiles):
    e, d, f = w1.shape[1:]
    wspec = lambda s1, s2: pl.BlockSpec((None, e, s1, s2), lambda i, tg, nu: (tg[i], 0, 0, 0))
    return pl.pallas_call(
        _ffn_kernel,
        out_shape=jax.ShapeDtypeStruct((n_tiles * MOE_TILE, d), F32),
        grid_spec=pltpu.PrefetchScalarGridSpec(
            num_scalar_prefetch=2,
            grid=(n_tiles,),
            in_specs=[
                pl.BlockSpec((MOE_TILE, ROW_W), lambda i, tg, nu: (i, 0)),
                wspec(d, f), wspec(d, f), wspec(f, d),
            ],
            out_specs=pl.BlockSpec((MOE_TILE, d), lambda i, tg, nu: (i, 0)),
        ),
        compiler_params=_cparams("arbitrary"),
        name="moe_ffn",
    )(tile_group, n_used, rows_sorted, w1, w3, w2)


def _ln2_kernel(dest_ref, x_ref, fs_ref, m_ref, mn_ref, lng_ref, lnb_ref, x2_ref, hn_ref, *rest,
                alpha, tile, row0, steps_per_seq, n_steps, dils):
    grouped_refs = rest[:len(dils)]
    fbuf, sem = rest[len(dils):len(dils) + 2]
    step = pl.program_id(0) * steps_per_seq + pl.program_id(1)
    slot = lax.rem(step, 2)

    def start_tile(st, sl):
        base = row0 + st * tile

        def issue(k2, a):
            for prio in range(N_DMA_PRIORITIES):
                k = N_DMA_PRIORITIES * k2 + prio
                pltpu.make_async_copy(fs_ref.at[pl.ds(dest_ref[base + k], 1)], fbuf.at[sl, pl.ds(k, 1)],
                                      sem.at[sl]).start(priority=prio)
            return a

        lax.fori_loop(0, tile // N_DMA_PRIORITIES, issue, 0, unroll=4)

    @pl.when(step == 0)
    def _():
        start_tile(0, 0)

    @pl.when(step + 1 < n_steps)
    def _():
        start_tile(step + 1, 1 - slot)

    def wait(k, a):
        pltpu.make_async_copy(fs_ref.at[pl.ds(0, 1)], fbuf.at[slot, pl.ds(0, 1)], sem.at[slot]).wait()
        return a

    lax.fori_loop(0, tile, wait, 0, unroll=8)
    m = m_ref[...]
    x2 = _layer_norm(alpha * x_ref[...] + _mod_chunk(m, 5) * fbuf[slot], lng_ref[...], lnb_ref[...])
    x2_ref[...] = x2
    mn = mn_ref[...]
    hn = x2 * (1.0 + _mod_chunk(mn, 1)) + _mod_chunk(mn, 0)
    hn_ref[...] = hn.astype(hn_ref.dtype)
    if dils:
        hbuf = rest[-1]
        n_blk = D_MODEL // LANES
        for j in range(n_blk):
            hbuf[j] = hn[:, j * LANES:(j + 1) * LANES]
        for gref, dil in zip(grouped_refs, dils):
            for r in range(dil):
                rows = jnp.concatenate([hbuf[j, pl.ds(r, tile // dil, stride=dil), :] for j in range(n_blk)], axis=-1)
                gref[r] = rows.astype(gref.dtype)


def _mod_spec(tile, d, per_token):
    if per_token:
        return pl.BlockSpec((None, tile, 6 * d), lambda bi, i, *_: (bi, i, 0))
    return pl.BlockSpec((None, 1, 6 * d), lambda bi, i, *_: (bi, 0, 0))


def _ln2(dest, x, f_sorted, row0, m, m_next, ln_g, ln_b, alpha, per_token, dils=()):
    b, s, d = x.shape
    tile = min(ROW_TILE, s)
    nt = s // tile
    mspec = _mod_spec(tile, d, per_token)
    tok = pl.BlockSpec((None, tile, d), lambda bi, i, dst: (bi, i, 0))
    vec = pl.BlockSpec((1, d), lambda bi, i, dst: (0, 0))
    kern = functools.partial(_ln2_kernel, alpha=alpha, tile=tile, row0=row0, steps_per_seq=nt, n_steps=b * nt,
                             dils=tuple(dils))
    grouped_shapes = tuple(jax.ShapeDtypeStruct((b, dil, s // dil, d), BF16) for dil in dils)
    grouped_specs = tuple(pl.BlockSpec((None, dil, tile // dil, d), lambda bi, i, dst: (bi, 0, i, 0)) for dil in dils)
    scratch = [pltpu.VMEM((2, tile, d), F32), pltpu.SemaphoreType.DMA((2,))]
    if dils:
        scratch.append(pltpu.VMEM((d // LANES, tile, LANES), F32))
    return pl.pallas_call(
        kern,
        out_shape=(jax.ShapeDtypeStruct((b, s, d), F32), jax.ShapeDtypeStruct((b, s, d), BF16)) + grouped_shapes,
        grid_spec=pltpu.PrefetchScalarGridSpec(
            num_scalar_prefetch=1,
            grid=(b, nt),
            in_specs=[tok, pl.BlockSpec(memory_space=pl.ANY), mspec, mspec, vec, vec],
            out_specs=(tok, tok) + grouped_specs,
            scratch_shapes=scratch,
        ),
        compiler_params=_cparams("arbitrary", "arbitrary"),
        name="ln2",
    )(dest, x, f_sorted, m, m_next, ln_g, ln_b)


def _rope_tables(pos):
    half = HEAD_DIM // 2
    inv = ROPE_THETA ** (-jnp.arange(half, dtype=F32) / half)
    ang = pos.astype(F32)[:, None] * inv[None, :]
    cos, sin = jnp.cos(ang), jnp.sin(ang)
    reps = LANES // HEAD_DIM
    return (jnp.tile(jnp.concatenate([cos, cos], axis=-1), (1, reps)),
            jnp.tile(jnp.concatenate([-sin, sin], axis=-1), (1, reps)))


def _qkv_kernel(*refs, tile, grouped, tail_chunks, tail_first, tail_rows, tiles_per_res):
    h_refs = refs[0:N_DIL]
    w_ref = refs[N_DIL]
    cos_refs = refs[N_DIL + 1:2 * N_DIL + 1]
    sin_refs = refs[2 * N_DIL + 1:3 * N_DIL + 1]
    outs = refs[3 * N_DIL + 1:]
    main_refs = outs[:N_DIL] if grouped else ()
    tail_refs = outs[len(main_refs):]
    c = pl.program_id(0)
    i = pl.program_id(2)
    first_half = (_iota((tile, LANES), 1) & (HEAD_DIM - 1)) < (HEAD_DIM // 2)

    def rope(acc, cos, sin):
        blocks = []
        for j in range(D_MODEL // LANES):
            blk = acc[:, j * LANES:(j + 1) * LANES]
            partner = jnp.where(first_half, pltpu.roll(blk, LANES - HEAD_DIM // 2, 1), pltpu.roll(blk, HEAD_DIM // 2, 1))
            blocks.append(blk * cos + partner * sin)
        return jnp.concatenate(blocks, axis=-1)

    for grp in range(N_DIL):
        ii = i & (tiles_per_res[grp] - 1)

        def emit(res, grp=grp, ii=ii):
            if grouped:
                main_refs[grp][...] = res.astype(BF16)
            for t, tref in enumerate(tail_refs):
                if tail_chunks[t] // 3 == grp:
                    @pl.when((c == tail_chunks[t]) & (ii >= tail_first[t]))
                    def _(tref=tref, t=t):
                        tref[...] = res[tile - tail_rows[t]:, :]

        @pl.when((c >= 3 * grp) & (c < 3 * grp + 2))
        def _(grp=grp, emit=emit):
            acc = jnp.dot(h_refs[grp][...], w_ref[...], preferred_element_type=F32)
            emit(rope(acc, cos_refs[grp][...], sin_refs[grp][...]))

        @pl.when(c == 3 * grp + 2)
        def _(grp=grp, emit=emit):
            emit(jnp.dot(h_refs[grp][...], w_ref[...], preferred_element_type=F32))


def _qkv_proj(hs, w, tables, tail_keep, grouped):
    b, d = hs[0].shape[0], hs[0].shape[-1]
    dils = tuple(h.shape[1] for h in hs)
    subs = tuple(h.shape[2] for h in hs)
    tile = min(ROW_TILE, min(subs))
    nt = dils[0] * subs[0] // tile
    tpr = tuple(sub // tile for sub in subs)
    assert all(t & (t - 1) == 0 and dil * t == nt for t, dil in zip(tpr, dils))
    n_chunks = w.shape[1] // d
    tail_chunks = tuple(sorted(tail_keep))
    tail_rows = tuple(min(tail_keep[c], tile) for c in tail_chunks)
    tail_blocks = tuple(tail_keep[c] // r for c, r in zip(tail_chunks, tail_rows))
    tail_first = tuple(tpr[c // 3] - nb for c, nb in zip(tail_chunks, tail_blocks))

    def parked(lo, hi, own, last):
        def imap(c, bi, i):
            before, after = c < lo, c >= hi
            return tuple(jnp.where(before, 0, jnp.where(after, l, o)) for o, l in zip(own(c, bi, i), last))
        return imap

    in_specs, cos_specs, sin_specs = [], [], []
    out_shape, out_specs = [], []
    for grp in range(N_DIL):
        dil, t = dils[grp], tpr[grp]
        lo, hi = 3 * grp, 3 * grp + 3
        in_specs.append(pl.BlockSpec((None, None, tile, d), parked(
            lo, hi, lambda c, bi, i, t=t: (bi, i // t, i % t, 0), (b - 1, dil - 1, t - 1, 0))))
        tab = pl.BlockSpec((None, tile, LANES), parked(
            lo, hi, lambda c, bi, i, t=t: (i // t, i % t, 0), (dil - 1, t - 1, 0)))
        cos_specs.append(tab)
        sin_specs.append(tab)
        if grouped:
            out_shape.append(jax.ShapeDtypeStruct((3, b, dil, subs[grp], d), BF16))
            out_specs.append(pl.BlockSpec((None, None, None, tile, d), parked(
                lo, hi, lambda c, bi, i, t=t, lo=lo: (c - lo, bi, i // t, i % t, 0), (2, b - 1, dil - 1, t - 1, 0))))
    for c_t, rows, first, nb in zip(tail_chunks, tail_rows, tail_first, tail_blocks):
        dil, t = dils[c_t // 3], tpr[c_t // 3]
        out_shape.append(jax.ShapeDtypeStruct((b, dil, tail_keep[c_t], d), F32))
        out_specs.append(pl.BlockSpec((None, None, rows, d), parked(
            c_t, c_t + 1, lambda c, bi, i, t=t, first=first: (bi, i // t, jnp.maximum(i % t - first, 0), 0),
            (b - 1, dil - 1, nb - 1, 0))))
    kern = functools.partial(_qkv_kernel, tile=tile, grouped=grouped, tail_chunks=tail_chunks, tail_first=tail_first,
                             tail_rows=tail_rows, tiles_per_res=tpr)
    return pl.pallas_call(
        kern,
        out_shape=tuple(out_shape),
        grid=(n_chunks, b, nt),
        in_specs=in_specs + [pl.BlockSpec((d, d), lambda c, bi, i: (0, c))] + cos_specs + sin_specs,
        out_specs=tuple(out_specs),
        compiler_params=_cparams("arbitrary", "arbitrary", "arbitrary"),
        name="qkv_proj",
    )(*hs, w, *[t[0] for t in tables], *[t[1] for t in tables])


def _attn_prompt_kernel(q_ref, kp_ref, kc_ref, vp_ref, vc_ref, o_ref, lse_ref):
    i = pl.program_id(2)
    tq = ATT_BAND
    lane = _iota((tq, LANES), 1)
    low = lane < HEAD_DIM
    qi = _iota((2 * tq, 2 * tq), 0) & (tq - 1)
    kj = _iota((2 * tq, 2 * tq), 1)
    mask = ((kj < tq) & (kj >= qi) & (i > 0)) | ((kj >= tq) & ((kj - tq) <= qi))
    lse_all = jnp.zeros((tq, LANES), F32)
    zero = jnp.zeros((), q_ref.dtype)
    for hp in range(N_HEADS // 2):
        sl = slice(hp * LANES, (hp + 1) * LANES)
        q = q_ref[:, sl]
        q2 = jnp.concatenate([jnp.where(low, q, zero), jnp.where(low, zero, q)], axis=0)
        kw = jnp.concatenate([kp_ref[:, sl], kc_ref[:, sl]], axis=0)
        vw = jnp.concatenate([vp_ref[:, sl], vc_ref[:, sl]], axis=0)
        s = lax.dot_general(q2, kw, _NT, preferred_element_type=F32) * (HEAD_DIM ** -0.5)
        s = jnp.where(mask, s, -jnp.inf)
        mx = jnp.max(s, axis=-1, keepdims=True)
        p = jnp.exp(s - mx)
        den = jnp.sum(p, axis=-1, keepdims=True)
        o2 = jnp.dot(p.astype(vw.dtype), vw, preferred_element_type=F32) * (1.0 / den)
        o_ref[:, sl] = jnp.where(low, o2[:tq], o2[tq:]).astype(o_ref.dtype)
        lse2 = mx + jnp.log(den)
        lse_all = jnp.where(lane == 2 * hp, lse2[:tq], jnp.where(lane == 2 * hp + 1, lse2[tq:], lse_all))
    lse_ref[...] = lse_all


def _attn_prompt(qkv_g, grp):
    _, b, dil, sub, d = qkv_g.shape
    tq = ATT_BAND
    cur = lambda which: pl.BlockSpec((None, None, None, tq, d), lambda bi, r, i: (which, bi, r, i, 0))
    prev = lambda which: pl.BlockSpec((None, None, None, tq, d), lambda bi, r, i: (which, bi, r, jnp.maximum(i - 1, 0), 0))
    return pl.pallas_call(
        _attn_prompt_kernel,
        out_shape=(jax.ShapeDtypeStruct((b, dil, sub, d), BF16), jax.ShapeDtypeStruct((b, dil, sub, LANES), F32)),
        grid=(b, dil, sub // tq),
        in_specs=[cur(0), prev(1), cur(1), prev(2), cur(2)],
        out_specs=(
            pl.BlockSpec((None, None, tq, d), lambda bi, r, i: (bi, r, i, 0)),
            pl.BlockSpec((None, None, tq, LANES), lambda bi, r, i: (bi, r, i, 0)),
        ),
        compiler_params=_cparams("arbitrary", "arbitrary", "arbitrary"),
        name=f"attn_prompt_g{grp}",
    )(qkv_g, qkv_g, qkv_g, qkv_g, qkv_g)


def _sample_pairs(n_new):
    pairs = []
    for grp, (_, dil) in enumerate(DIL_GROUPS):
        for t in range(n_new):
            for t2 in range(t + 1):
                if (t - t2) % dil == 0:
                    pairs.append((grp, t, t2))
    return pairs


def _sample_patterns(n_new):
    lanes = np.arange(LANES)
    pairs = _sample_pairs(n_new)
    assert len(pairs) <= LANES and N_DIL * n_new <= LANES
    spread = []
    for grp, (_, dil) in reversed(list(enumerate(DIL_GROUPS))):
        if dil == 1:
            continue
        col = grp * n_new + lanes % dil
        spread.append(((lanes % dil) < n_new)[None, :] & (np.arange(LANES)[:, None] == col[None, :]))
    for t in range(n_new):
        spread.append(np.broadcast_to((np.arange(LANES) == t)[:, None], (LANES, LANES)))
    qn = np.zeros((LANES, LANES), bool)
    kn = np.zeros((LANES, LANES), bool)
    fold_n = np.zeros((LANES, LANES), bool)
    for p, (grp, t, t2) in enumerate(pairs):
        qn[grp * n_new + t, p] = True
        kn[grp * n_new + t2, p] = True
        fold_n[p, t] = True
    fold = []
    for grp, (_, dil) in reversed(list(enumerate(DIL_GROUPS))):
        if dil == 1:
            continue
        fold.append((lanes[:, None] % dil) == np.arange(LANES)[None, :])
        fold[-1] = fold[-1] & (np.arange(LANES)[None, :] < n_new)
    for t in range(n_new):
        fold.append(np.broadcast_to((np.arange(LANES) == t)[None, :], (LANES, LANES)))
    fold.append(fold_n)
    as_bf16 = lambda m: jnp.asarray(m.astype(np.float32), BF16)
    return as_bf16(np.concatenate(spread + [qn], axis=1)), as_bf16(kn), as_bf16(np.concatenate(fold, axis=0))


def _dot_exact01(x, p):
    hi = x.astype(BF16)
    r1 = x - hi.astype(F32)
    mid = r1.astype(BF16)
    lo = (r1 - mid.astype(F32)).astype(BF16)
    dot = lambda a: jnp.dot(a, p, preferred_element_type=F32)
    return dot(hi) + dot(mid) + dot(lo)


def _attn_sample_kernel(qkn_ref, spread_ref, kn_ref, fold_ref, k0_ref, v0_ref, k1_ref, v1_ref, k2_ref, v2_ref, o_ref,
                        *, n_new):
    rows = k0_ref.shape[0]
    hpb = rows // HEAD_DIM
    scale = HEAD_DIM ** -0.5
    neg = F32(-jnp.inf)
    pairs = _sample_pairs(n_new)

    def columns(kind):
        x = qkn_ref[kind]
        return jnp.concatenate([x, jnp.zeros((LANES - x.shape[0], rows), F32)], axis=0).T

    q_pat = _dot_exact01(columns(0), spread_ref[...])
    kn_pat = _dot_exact01(columns(1), kn_ref[...])
    vn_pat = _dot_exact01(columns(2), kn_ref[...])
    pat = lambda j: q_pat[:, j * LANES:(j + 1) * LANES]
    hrow = _iota((hpb, LANES), 0)
    lane = _iota((hpb, LANES), 1)

    def head_sum(prod):
        out = jnp.zeros((hpb, LANES), F32)
        for h in range(hpb):
            out = jnp.where(hrow == h, jnp.sum(prod[h * HEAD_DIM:(h + 1) * HEAD_DIM, :], axis=0, keepdims=True), out)
        return out

    def head_rows(p):
        return jnp.concatenate([jnp.broadcast_to(p[h:h + 1, :], (HEAD_DIM, LANES)) for h in range(hpb)], axis=0)

    dil_refs = ((k2_ref, v2_ref, DIL_GROUPS[2][1]), (k1_ref, v1_ref, DIL_GROUPS[1][1]))
    s_dil = []
    for j, (k_ref, _, dil) in enumerate(dil_refs):
        tiles = [head_sum(k_ref[:, lt * LANES:(lt + 1) * LANES] * pat(j)) * scale for lt in range(k_ref.shape[1] // LANES)]
        s_dil.append(tiles)
    n_dilp = len(dil_refs)
    k0 = k0_ref[...]
    s_0 = [head_sum(k0 * pat(n_dilp + t)) * scale for t in range(n_new)]
    s_n = head_sum(kn_pat * pat(n_dilp + n_new)) * scale
    pair_mask = []
    for t in range(n_new):
        mk = lane < 0
        for p, (_, pt, _) in enumerate(pairs):
            if pt == t:
                mk = mk | (lane == p)
        pair_mask.append(mk)
    m_t = []
    for t in range(n_new):
        mx = jnp.max(jnp.where(lane >= t, s_0[t], neg), axis=-1, keepdims=True)
        mx = jnp.maximum(mx, jnp.max(jnp.where(pair_mask[t], s_n, neg), axis=-1, keepdims=True))
        for (_, _, dil), tiles in zip(dil_refs, s_dil):
            sel = (lane & (dil - 1)) == t
            tm = functools.reduce(jnp.maximum, tiles)
            mx = jnp.maximum(mx, jnp.max(jnp.where(sel, tm, neg), axis=-1, keepdims=True))
        m_t.append(mx)
    den_t = [jnp.zeros((hpb, 1), F32) for _ in range(n_new)]
    folded = []
    for (_, v_ref, dil), tiles in zip(dil_refs, s_dil):
        res = lane & (dil - 1)
        m_lane = functools.reduce(lambda a, b: a + b, [jnp.where(res == t, m_t[t], 0.0) for t in range(n_new)])
        acc = jnp.zeros((rows, LANES), F32)
        psum = jnp.zeros((hpb, LANES), F32)
        for lt, s in enumerate(tiles):
            p = jnp.where(res < n_new, jnp.exp(s - m_lane), 0.0)
            psum = psum + p
            acc = acc + v_ref[:, lt * LANES:(lt + 1) * LANES] * head_rows(p)
        for t in range(n_new):
            den_t[t] = den_t[t] + jnp.sum(jnp.where(res == t, psum, 0.0), axis=-1, keepdims=True)
        folded.append(acc)
    v0 = v0_ref[...]
    for t in range(n_new):
        p = jnp.where(lane >= t, jnp.exp(s_0[t] - m_t[t]), 0.0)
        den_t[t] = den_t[t] + jnp.sum(p, axis=-1, keepdims=True)
        folded.append(v0 * head_rows(p))
    m_pair = functools.reduce(lambda a, b: a + b, [jnp.where(pair_mask[t], m_t[t], 0.0) for t in range(n_new)])
    any_pair = functools.reduce(lambda a, b: a | b, pair_mask)
    p_n = jnp.where(any_pair, jnp.exp(s_n - m_pair), 0.0)
    for t in range(n_new):
        den_t[t] = den_t[t] + jnp.sum(jnp.where(pair_mask[t], p_n, 0.0), axis=-1, keepdims=True)
    folded.append(vn_pat * head_rows(p_n))
    out_t = _dot_exact01(jnp.concatenate(folded, axis=-1), fold_ref[...])
    den = functools.reduce(lambda a, b: a + b, [jnp.where(lane == t, den_t[t], 0.0) for t in range(n_new)])
    den = jnp.where(lane < n_new, den, 1.0)
    out_t = out_t * head_rows(1.0 / den)
    o_ref[...] = out_t.T[0:n_new, :]


def _attn_sample(qkn, caches_t, n_new, rows=512):
    b, d = qkn.shape[0], qkn.shape[-1]
    spread, kn, fold = _sample_patterns(n_new)
    cache_args, cache_specs = [], []
    for (win, dil), (ck, cv) in zip(DIL_GROUPS, caches_t):
        assert ck.shape == (b, d, win) and win // dil == ATT_BAND and (dil == 1 or dil >= n_new)
        for c in (ck, cv):
            cache_args.append(c)
            cache_specs.append(pl.BlockSpec((None, rows, win), lambda bi, hh: (bi, hh, 0)))
    const = lambda a: pl.BlockSpec(a.shape, lambda bi, hh: (0, 0))
    return pl.pallas_call(
        functools.partial(_attn_sample_kernel, n_new=n_new),
        out_shape=jax.ShapeDtypeStruct((b, n_new, d), F32),
        grid=(b, d // rows),
        in_specs=[pl.BlockSpec((None, 3, qkn.shape[2], rows), lambda bi, hh: (bi, 0, 0, hh)),
                  const(spread), const(kn), const(fold)] + cache_specs,
        out_specs=pl.BlockSpec((None, n_new, rows), lambda bi, hh: (bi, 0, hh)),
        compiler_params=_cparams("arbitrary", "arbitrary"),
        name="attn_sample",
    )(qkn, spread, kn, fold, *cache_args)


def _oproj_kernel(*refs, dils, n_lse, alpha, tile):
    n_groups = len(dils)
    o_refs = refs[:n_groups]
    lse_refs = refs[n_groups:n_groups + n_lse]
    k = n_groups + n_lse
    x_ref, m_ref, wo_ref, hx_ref, lng_ref, lnb_ref, wr_ref, br_ref = refs[k:k + 8]
    x1_ref, hrow_ref, ext_ref = refs[k + 8:k + 11]
    obuf, lbuf = refs[k + 11:]

    def natural(ref, buf, g, dil):
        if dil == 1:
            return ref[0].astype(F32)
        n_blk = ref.shape[-1] // LANES
        for r in range(dil):
            rows = ref[r].astype(F32)
            for j in range(n_blk):
                buf[g * n_blk + j, pl.ds(r, tile // dil, stride=dil), :] = rows[:, j * LANES:(j + 1) * LANES]
        return jnp.concatenate([buf[g * n_blk + j] for j in range(n_blk)], axis=-1)

    os_ = [natural(o_refs[g], obuf, g, dils[g]) for g in range(n_groups)]
    if n_lse == 0:
        o = os_[0].astype(BF16)
    else:
        lses = [natural(lse_refs[g], lbuf, g, dils[g]) for g in range(n_groups)]
        mx = functools.reduce(jnp.maximum, lses)
        es = [jnp.exp(l - mx) for l in lses]
        inv = 1.0 / functools.reduce(lambda a, b: a + b, es)
        o = jnp.zeros(x_ref.shape, F32)
        hx = hx_ref[...]
        for e, og in zip(es, os_):
            wgt = e * inv
            hi = wgt.astype(BF16)
            lo = (wgt - hi.astype(F32)).astype(BF16)
            wexp = jnp.dot(hi, hx, preferred_element_type=F32) + jnp.dot(lo, hx, preferred_element_type=F32)
            o = o + wexp * og
        o = o.astype(BF16)
    y = jnp.dot(o, wo_ref[...], preferred_element_type=F32)
    x1, h2, ext = _post_mixer(x_ref[...], y, m_ref[...], lng_ref, lnb_ref, wr_ref, br_ref, alpha)
    x1_ref[...] = x1
    hrow_ref[:, 0:D_MODEL] = h2
    hrow_ref[:, D_MODEL:] = ext
    ext_ref[...] = ext


def _out_proj(os_, lses, x, m, w_o, head_expand, ln_g, ln_b, wr, br, alpha, per_token):
    b, s, d = x.shape
    tile = min(ROW_TILE, s)
    nt = s // tile
    dils = tuple(o.shape[1] for o in os_)
    tok = pl.BlockSpec((None, tile, d), lambda bi, i: (bi, i, 0))
    grouped = lambda dil, w: pl.BlockSpec((None, dil, tile // dil, w), lambda bi, i: (bi, 0, i, 0))
    full = lambda bi, i: (0, 0)
    return pl.pallas_call(
        functools.partial(_oproj_kernel, dils=dils, n_lse=len(lses), alpha=alpha, tile=tile),
        out_shape=(
            jax.ShapeDtypeStruct((b, s, d), F32),
            jax.ShapeDtypeStruct((b * s, ROW_W), F32),
            jax.ShapeDtypeStruct((b * s, EXT), F32),
        ),
        grid=(b, nt),
        in_specs=[grouped(dil, d) for dil in dils] + [grouped(dil, LANES) for dil in dils[:len(lses)]] + [
            tok, _mod_spec(tile, d, per_token),
            pl.BlockSpec((d, d), full),
            pl.BlockSpec((LANES, d), full),
            pl.BlockSpec((1, d), full),
            pl.BlockSpec((1, d), full),
            pl.BlockSpec((2, d, EXT), lambda *_: (0, 0, 0)),
            pl.BlockSpec((1, EXT), full),
        ],
        out_specs=(
            tok,
            pl.BlockSpec((tile, ROW_W), lambda bi, i: (bi * nt + i, 0)),
            pl.BlockSpec((tile, EXT), lambda bi, i: (bi * nt + i, 0)),
        ),
        scratch_shapes=[pltpu.VMEM((len(dils) * d // LANES, tile, LANES), F32), pltpu.VMEM((len(dils), tile, LANES), F32)],
        compiler_params=_cparams("arbitrary", "arbitrary"),
        name="out_proj",
    )(*os_, *lses, x, m, w_o, head_expand, ln_g, ln_b, wr, br)


def _router_matrix(w_group, b_group, w_er, b_er):
    d = w_group.shape[0]
    n_e = N_EXPERT_GROUPS * EXPERTS_PER_GROUP
    w_e = jnp.transpose(w_er, (1, 0, 2)).reshape(d, n_e)
    pad = EXT - N_EXPERT_GROUPS - n_e
    wr = jnp.concatenate([w_group, w_e, jnp.zeros((d, pad), F32)], axis=1)
    br = jnp.concatenate([b_group, b_er.reshape(n_e), jnp.zeros((pad,), F32)])[None, :]
    wr_hi = wr.astype(BF16)
    wr_lo = (wr - wr_hi.astype(F32)).astype(BF16)
    return jnp.stack([wr_hi, wr_lo]), br


def _moe_experts(rows_p, rows_s, ext_p, ext_s, w1, w3, w2):
    dest, tile_group, n_used, n_tiles = _group_sort(ext_p, ext_s)
    sorted_rows = _scatter_rows(dest, rows_p, rows_s, n_tiles * MOE_TILE)
    return dest, _moe_ffn(tile_group, n_used, sorted_rows, w1, w3, w2, n_tiles)


def kernel(x_prompt, x_sample, state_pool, cache_k_w128, cache_v_w128, cache_k_w512, cache_v_w512, cache_k_w2048, cache_v_w2048, c_prompt, c_sample, mod_w, mod_b, ln1_g, ln1_b, ln2_g, ln2_b, pool_w, pool_scale, attn_w_qkv, attn_w_o, moe_w_group, moe_b_group, moe_w_expert_router, moe_b_expert_router, moe_w1, moe_w3, moe_w2):
    depth = mod_w.shape[0]
    alpha = float((2.0 * depth) ** 0.25)
    bp, s, d = x_prompt.shape
    bs, t_new, _ = x_sample.shape
    n_p, n_s = bp * s, bs * t_new
    kv_caches = ((cache_k_w128, cache_v_w128), (cache_k_w512, cache_v_w512), (cache_k_w2048, cache_v_w2048))

    c_all = jnp.concatenate([c_sample, c_prompt, jnp.zeros((SUBLANES - bp % SUBLANES, d), F32)], axis=0)
    m_all = _modulation(c_all, mod_w, mod_b)
    m_dec = [m_all[i, :bs] for i in range(depth)]
    m_tok = [jnp.repeat(m, t_new, axis=0).reshape(1, n_s, 6 * d) for m in m_dec]
    m_seq = [m_all[i, bs:bs + bp].reshape(bp, 1, 6 * d) for i in range(depth)]

    head_expand = jnp.asarray(np.arange(LANES)[:, None] == (np.arange(d)[None, :] // HEAD_DIM), BF16)
    dils = tuple(dil for _, dil in DIL_GROUPS)
    cos_p, sin_p = _rope_tables(jnp.arange(s, dtype=jnp.int32))
    by_residue = lambda t, dil: jnp.transpose(t.reshape(s // dil, dil, LANES), (1, 0, 2))
    tables_p = [(by_residue(cos_p, dil), by_residue(sin_p, dil)) for dil in dils]
    cos_s, sin_s = _rope_tables(PAST_LEN + (jnp.arange(n_s, dtype=jnp.int32) % t_new))
    tables_s = [(cos_s[None], sin_s[None])] * N_DIL

    xp, xs = x_prompt, x_sample.reshape(1, n_s, d)
    hp_groups = hs = None
    pool_p, pool_s = [], []
    kv_p = [[] for _ in range(2 * N_DIL)]
    kv_s = [[] for _ in range(2 * N_DIL)]
    for i in range(depth):
        li = i // 2
        wr, br = _router_matrix(moe_w_group[i], moe_b_group[i], moe_w_expert_router[i], moe_b_expert_router[i])
        ln1g, ln1b = ln1_g[i][None, :], ln1_b[i][None, :]
        if i % 2 == 0:
            pw = pool_w[li].astype(BF16)
            ps = pool_scale[li][None, :]
            xp, rows_p, ext_p, tail = _pool_prompt(xp, m_seq[i], pw, ps, ln1g, ln1b, wr, br, alpha)
            x1s, rows_s, ext_s, znew = _pool_sample(
                state_pool[li].reshape(bs, POOL_BUF * d), xs.reshape(bs, t_new * d),
                m_dec[i], pw, ps, ln1g, ln1b, wr, br, alpha)
            xs = x1s.reshape(1, n_s, d)
            rows_s = rows_s.reshape(n_s, ROW_W)
            ext_s = ext_s.reshape(n_s, EXT)
            pool_p.append(tail)
            pool_s.append(znew.reshape(bs, t_new, d))
        else:
            wqkv = attn_w_qkv[li].astype(BF16)
            wo = attn_w_o[li].astype(BF16)
            kv_chunks = [3 * g + j for g in range(N_DIL) for j in (1, 2)]
            keep_p = {3 * g + j: min(win, s) // dil for g, (win, dil) in enumerate(DIL_GROUPS) for j in (1, 2)}
            *qkv_groups, = _qkv_proj(hp_groups, wqkv, tables_p, keep_p, grouped=True)
            tails_p = qkv_groups[N_DIL:]
            rows_all = _qkv_proj([hs[:, None]] * N_DIL, wqkv, tables_s, {c: n_s for c in range(3 * N_DIL)},
                                 grouped=False)
            for j, c in enumerate(kv_chunks):
                tail = jnp.transpose(tails_p[j], (0, 2, 1, 3))
                kv_p[j].append(tail.reshape(bp, -1, N_HEADS, HEAD_DIM))
                kv_s[j].append(rows_all[c].reshape(bs, t_new, N_HEADS, HEAD_DIM))
            outs = [_attn_prompt(qkv_groups[g], g) for g in range(N_DIL)]
            xp, rows_p, ext_p = _out_proj([o for o, _ in outs], [l for _, l in outs], xp, m_seq[i], wo, head_expand,
                                          ln1g, ln1b, wr, br, alpha, per_token=False)
            kinds = [jnp.concatenate([rows_all[3 * g + j].reshape(bs, t_new, d) for g in range(N_DIL)], axis=1)
                     for j in range(3)]
            qkn = jnp.pad(jnp.stack(kinds, axis=1), ((0, 0), (0, 0), (0, 2 * SUBLANES - N_DIL * t_new), (0, 0)))
            caches_t = [tuple(jnp.transpose(c[li], (0, 2, 3, 1)).reshape(bs, d, c.shape[2]) for c in kv)
                        for kv in kv_caches]
            o_s = _attn_sample(qkn, caches_t, t_new)
            xs, rows_s, ext_s = _out_proj([o_s.reshape(1, 1, n_s, d)], [], xs, m_tok[i], wo, head_expand,
                                          ln1g, ln1b, wr, br, alpha, per_token=True)
        dest, f_sorted = _moe_experts(rows_p, rows_s, ext_p, ext_s,
                                      moe_w1[i].astype(BF16), moe_w3[i].astype(BF16), moe_w2[i].astype(BF16))
        nxt = min(i + 1, depth - 1)
        ln2g, ln2b = ln2_g[i][None, :], ln2_b[i][None, :]
        regroup = tuple(dil for dil in dils if dil > 1) if (i + 1 < depth and (i + 1) % 2 == 1) else ()
        xp, hp, *hp_dilated = _ln2(dest, xp, f_sorted, 0, m_seq[i], m_seq[nxt], ln2g, ln2b, alpha, per_token=False,
                                   dils=regroup)
        if regroup:
            hp_dilated = iter(hp_dilated)
            hp_groups = [hp[:, None] if dil == 1 else next(hp_dilated) for dil in dils]
        xs, hs = _ln2(dest, xs, f_sorted, n_p, m_tok[i], m_tok[nxt], ln2g, ln2b, alpha, per_token=True)

    stack = lambda lst: jnp.stack(lst, axis=0)
    return (xp, xs.reshape(bs, t_new, d), stack(pool_p), stack(pool_s),
            *[stack(kv_p[j]) for j in range(2 * N_DIL)], *[stack(kv_s[j]) for j in range(2 * N_DIL)])
```

```python
import functools

import numpy as np
import jax
import jax.numpy as jnp
from jax import lax
from jax.experimental import pallas as pl
from jax.experimental.pallas import tpu as pltpu

F32 = jnp.float32
BF16 = jnp.bfloat16
HIGHEST = lax.Precision.HIGHEST

D_MODEL = 1024
POOL_WINDOWS = (2, 4, 8, 16)
N_POOL_GROUPS = len(POOL_WINDOWS)
POOL_GROUP = D_MODEL // N_POOL_GROUPS
POOL_BUF = max(POOL_WINDOWS) - 1
POOL_HALO = 16
HEAD_DIM = 64
N_HEADS = D_MODEL // HEAD_DIM
DIL_GROUPS = ((128, 1), (512, 4), (2048, 16))
N_DIL = len(DIL_GROUPS)
ATT_BAND = 128
ROPE_THETA = 10000.0
N_EXPERT_GROUPS = 4
EXPERTS_PER_GROUP = 8
PAST_LEN = 2048
LN_EPS = 1e-5

LANES = 128
SUBLANES = 8
VMEM_LIMIT_BYTES = 56 * 1024 * 1024

EXT = LANES
EXT_EXPERT0 = N_EXPERT_GROUPS
EXT_ONEHOT0 = 120
ROW_W = D_MODEL + EXT

MOE_TILE = 512
SORT_CHUNK = 512
ROW_TILE = 512
N_DMA_PRIORITIES = 2

_NT = (((1,), (1,)), ((), ()))


def _cparams(*sem):
    return pltpu.CompilerParams(dimension_semantics=sem, vmem_limit_bytes=VMEM_LIMIT_BYTES)


def _iota(shape, dim):
    return lax.broadcasted_iota(jnp.int32, shape, dim)


def _mod_kernel(c_ref, w_ref, b_ref, o_ref):
    o_ref[...] = jnp.dot(c_ref[...], w_ref[...], precision=HIGHEST, preferred_element_type=F32) + b_ref[...]


def _modulation(c_all, mod_w, mod_b):
    n_layers, d, n_out = mod_w.shape
    rows = c_all.shape[0]
    tn = n_out // 4
    return pl.pallas_call(
        _mod_kernel,
        out_shape=jax.ShapeDtypeStruct((n_layers, rows, n_out), F32),
        grid=(n_layers, n_out // tn),
        in_specs=[
            pl.BlockSpec((rows, d), lambda l, j: (0, 0)),
            pl.BlockSpec((None, d, tn), lambda l, j: (l, 0, j)),
            pl.BlockSpec((None, 1, tn), lambda l, j: (l, 0, j)),
        ],
        out_specs=pl.BlockSpec((None, rows, tn), lambda l, j: (l, 0, j)),
        compiler_params=_cparams("arbitrary", "arbitrary"),
        name="modulation",
    )(c_all, mod_w, mod_b.reshape(n_layers, 1, n_out))


def _mod_chunk(m, j):
    return m[:, j * D_MODEL:(j + 1) * D_MODEL]


def _layer_norm(u, g, b):
    mu = jnp.mean(u, axis=-1, keepdims=True)
    uc = u - mu
    var = jnp.mean(uc * uc, axis=-1, keepdims=True)
    return uc * lax.rsqrt(var + LN_EPS) * g + b


def _route(h2, wr_ref, br_ref):
    h_hi = h2.astype(BF16)
    h_lo = (h2 - h_hi.astype(F32)).astype(BF16)
    dot = lambda a, b: jnp.dot(a, b, preferred_element_type=F32)
    logits = dot(h_hi, wr_ref[0]) + dot(h_lo, wr_ref[0]) + dot(h_hi, wr_ref[1]) + br_ref[...]
    lane_f = _iota(logits.shape, 1).astype(F32)
    neg = F32(-jnp.inf)
    big = F32(EXT)
    gl = jnp.where(lane_f < N_EXPERT_GROUPS, logits, neg)
    gmax = jnp.max(gl, axis=-1, keepdims=True)
    gidx = jnp.min(jnp.where(gl == gmax, lane_f, big), axis=-1, keepdims=True)
    g_p = 1.0 / jnp.sum(jnp.exp(gl - gmax), axis=-1, keepdims=True)
    e_lo = EXT_EXPERT0 + EXPERTS_PER_GROUP * gidx
    el = jnp.where((lane_f >= e_lo) & (lane_f < e_lo + EXPERTS_PER_GROUP), logits, neg)
    v1 = jnp.max(el, axis=-1, keepdims=True)
    i1 = jnp.min(jnp.where(el == v1, lane_f, big), axis=-1, keepdims=True)
    el2 = jnp.where(lane_f == i1, neg, el)
    v2 = jnp.max(el2, axis=-1, keepdims=True)
    i2 = jnp.min(jnp.where(el2 == v2, lane_f, big), axis=-1, keepdims=True)
    r = jnp.exp(v2 - v1)
    w1 = g_p / (1.0 + r)
    w2 = w1 * r
    ext = jnp.where(lane_f == i1, w1, 0.0) + jnp.where(lane_f == i2, w2, 0.0)
    return ext + jnp.where(lane_f == gidx + EXT_ONEHOT0, 1.0, 0.0)


def _post_mixer(x, y, m, lng_ref, lnb_ref, wr_ref, br_ref, alpha):
    x1 = _layer_norm(alpha * x + _mod_chunk(m, 2) * y, lng_ref[...], lnb_ref[...])
    h2 = x1 * (1.0 + _mod_chunk(m, 4)) + _mod_chunk(m, 3)
    return x1, h2, _route(h2, wr_ref, br_ref)


def _pool_prompt_kernel(x_ref, xprev_ref, m_ref, pw_ref, ps_ref, lng_ref, lnb_ref, wr_ref, br_ref,
                        x1_ref, hrow_ref, ext_ref, tail_ref, zbuf, *, tile, alpha):
    i = pl.program_id(1)
    m = m_ref[...]
    shift, scale = _mod_chunk(m, 0), _mod_chunk(m, 1)
    x = x_ref[...]
    z = x * (1.0 + scale) + shift
    zprev = xprev_ref[...] * (1.0 + scale) + shift
    zbuf[0:POOL_HALO, :] = jnp.where(i > 0, zprev, 0.0)
    zbuf[POOL_HALO:, :] = z
    pos = i * tile + _iota((tile, 1), 0)
    ys = []
    for g, w in enumerate(POOL_WINDOWS):
        cols = slice(g * POOL_GROUP, (g + 1) * POOL_GROUP)
        win = zbuf[pl.ds(POOL_HALO, tile), cols]
        for j in range(1, w):
            win = win + zbuf[pl.ds(POOL_HALO - j, tile), cols]
        cnt = jnp.minimum(pos + 1, w).astype(F32)
        d = win / cnt - z[:, cols]
        ys.append(jnp.dot(d.astype(BF16), pw_ref[g], preferred_element_type=F32))
    y = jnp.concatenate(ys, axis=-1) * ps_ref[...]
    x1, h2, ext = _post_mixer(x, y, m, lng_ref, lnb_ref, wr_ref, br_ref, alpha)
    x1_ref[...] = x1
    hrow_ref[:, 0:D_MODEL] = h2
    hrow_ref[:, D_MODEL:] = ext
    ext_ref[...] = ext

    @pl.when(i == pl.num_programs(1) - 1)
    def _():
        tail_ref[...] = zbuf[pl.ds(POOL_HALO + tile - POOL_BUF, POOL_BUF), :]


def _pool_prompt(x, m, pool_w, pool_scale, ln_g, ln_b, wr, br, alpha, tile=256):
    b, s, d = x.shape
    nt = s // tile
    halo_blocks = tile // POOL_HALO
    full = lambda bi, i: (0, 0)
    kern = functools.partial(_pool_prompt_kernel, tile=tile, alpha=alpha)
    return pl.pallas_call(
        kern,
        out_shape=(
            jax.ShapeDtypeStruct((b, s, d), F32),
            jax.ShapeDtypeStruct((b * s, ROW_W), F32),
            jax.ShapeDtypeStruct((b * s, EXT), F32),
            jax.ShapeDtypeStruct((b, POOL_BUF, d), F32),
        ),
        grid=(b, nt),
        in_specs=[
            pl.BlockSpec((None, tile, d), lambda bi, i: (bi, i, 0)),
            pl.BlockSpec((None, POOL_HALO, d), lambda bi, i: (bi, jnp.maximum(i * halo_blocks - 1, 0), 0)),
            pl.BlockSpec((None, 1, 6 * d), lambda bi, i: (bi, 0, 0)),
            pl.BlockSpec((N_POOL_GROUPS, POOL_GROUP, POOL_GROUP), lambda bi, i: (0, 0, 0)),
            pl.BlockSpec((1, d), full),
            pl.BlockSpec((1, d), full),
            pl.BlockSpec((1, d), full),
            pl.BlockSpec((2, d, EXT), lambda *_: (0, 0, 0)),
            pl.BlockSpec((1, EXT), full),
        ],
        out_specs=(
            pl.BlockSpec((None, tile, d), lambda bi, i: (bi, i, 0)),
            pl.BlockSpec((tile, ROW_W), lambda bi, i: (bi * nt + i, 0)),
            pl.BlockSpec((tile, EXT), lambda bi, i: (bi * nt + i, 0)),
            pl.BlockSpec((None, POOL_BUF, d), lambda bi, i: (bi, 0, 0)),
        ),
        scratch_shapes=[pltpu.VMEM((tile + POOL_HALO, d), F32)],
        compiler_params=_cparams("arbitrary", "arbitrary"),
        name="pool_prompt",
    )(x, x, m, pool_w, pool_scale, ln_g, ln_b, wr, br)


def _pool_sample_kernel(st_ref, x_ref, m_ref, pw_ref, ps_ref, lng_ref, lnb_ref, wr_ref, br_ref,
                        x1_ref, hrow_ref, ext_ref, znew_ref, *, n_new, alpha):
    d = D_MODEL
    rows = [st_ref[:, r * d:(r + 1) * d] for r in range(POOL_BUF)]
    xs = []
    m = m_ref[...]
    for t in range(n_new):
        x = x_ref[:, t * d:(t + 1) * d]
        z = x * (1.0 + _mod_chunk(m, 1)) + _mod_chunk(m, 0)
        znew_ref[:, t * d:(t + 1) * d] = z
        rows.append(z)
        xs.append(x)
    for t in range(n_new):
        last = POOL_BUF + t
        ys = []
        for g, w in enumerate(POOL_WINDOWS):
            cols = slice(g * POOL_GROUP, (g + 1) * POOL_GROUP)
            win = rows[last][:, cols]
            for j in range(1, w):
                win = win + rows[last - j][:, cols]
            dgrp = win / F32(w) - rows[last][:, cols]
            ys.append(jnp.dot(dgrp.astype(BF16), pw_ref[g], preferred_element_type=F32))
        y = jnp.concatenate(ys, axis=-1) * ps_ref[...]
        x1, h2, ext = _post_mixer(xs[t], y, m, lng_ref, lnb_ref, wr_ref, br_ref, alpha)
        x1_ref[:, t * d:(t + 1) * d] = x1
        hrow_ref[:, t * ROW_W:t * ROW_W + d] = h2
        hrow_ref[:, t * ROW_W + d:(t + 1) * ROW_W] = ext
        ext_ref[:, t * EXT:(t + 1) * EXT] = ext


def _pool_sample(state, x, m, pool_w, pool_scale, ln_g, ln_b, wr, br, alpha, bb=32):
    b = x.shape[0]
    d = D_MODEL
    n_new = x.shape[1] // d
    full = lambda i: (0, 0)
    kern = functools.partial(_pool_sample_kernel, n_new=n_new, alpha=alpha)
    return pl.pallas_call(
        kern,
        out_shape=(
            jax.ShapeDtypeStruct((b, n_new * d), F32),
            jax.ShapeDtypeStruct((b, n_new * ROW_W), F32),
            jax.ShapeDtypeStruct((b, n_new * EXT), F32),
            jax.ShapeDtypeStruct((b, n_new * d), F32),
        ),
        grid=(b // bb,),
        in_specs=[
            pl.BlockSpec((bb, POOL_BUF * d), lambda i: (i, 0)),
            pl.BlockSpec((bb, n_new * d), lambda i: (i, 0)),
            pl.BlockSpec((bb, 6 * d), lambda i: (i, 0)),
            pl.BlockSpec((N_POOL_GROUPS, POOL_GROUP, POOL_GROUP), lambda i: (0, 0, 0)),
            pl.BlockSpec((1, d), full),
            pl.BlockSpec((1, d), full),
            pl.BlockSpec((1, d), full),
            pl.BlockSpec((2, d, EXT), lambda *_: (0, 0, 0)),
            pl.BlockSpec((1, EXT), full),
        ],
        out_specs=(
            pl.BlockSpec((bb, n_new * d), lambda i: (i, 0)),
            pl.BlockSpec((bb, n_new * ROW_W), lambda i: (i, 0)),
            pl.BlockSpec((bb, n_new * EXT), lambda i: (i, 0)),
            pl.BlockSpec((bb, n_new * d), lambda i: (i, 0)),
        ),
        compiler_params=_cparams("arbitrary"),
        name="pool_sample",
    )(state, x, m, pool_w, pool_scale, ln_g, ln_b, wr, br)


def _sort_kernel(extp_ref, exts_ref, dest_ref, meta_ref, oh_ref, *, n_chunks_p, n_chunks_s):
    ch = SORT_CHUNK
    r_io = _iota((SUBLANES, EXT), 0)
    l_io = _iota((SUBLANES, EXT), 1)
    sel = jnp.where((l_io == r_io + EXT_ONEHOT0) & (r_io < N_EXPERT_GROUPS), 1.0, 0.0).astype(BF16)

    def count_from(ext_ref, chunk0):
        def body(c, cnt):
            ext = ext_ref[pl.ds(pl.multiple_of(c * ch, ch), ch), :]
            oh = lax.dot_general(sel, ext.astype(BF16), _NT, preferred_element_type=F32)
            oh_ref[chunk0 + c] = oh
            return cnt + jnp.sum(oh, axis=-1, keepdims=True)
        return body

    counts = lax.fori_loop(0, n_chunks_p, count_from(extp_ref, 0), jnp.zeros((SUBLANES, 1), F32))
    counts = lax.fori_loop(0, n_chunks_s, count_from(exts_ref, n_chunks_p), counts)
    padded = jnp.floor((counts + (MOE_TILE - 1)) * (1.0 / MOE_TILE)) * MOE_TILE
    row = _iota((SUBLANES, 1), 0)
    starts = jnp.zeros((SUBLANES, 1), F32)
    for g in range(1, N_EXPERT_GROUPS):
        starts = starts + jnp.where(row >= g, padded[g - 1:g, :], 0.0)
    tri = jnp.where(_iota((ch, ch), 0) < _iota((ch, ch), 1), 1.0, 0.0).astype(BF16)

    def dest_body(c, base):
        oh = oh_ref[c]
        pre = jnp.dot(oh.astype(BF16), tri, preferred_element_type=F32)
        dest = jnp.sum(oh * (base + pre), axis=0, keepdims=True)
        dest_ref[pl.ds(c, 1), :] = dest.astype(jnp.int32)
        return base + jnp.sum(oh, axis=-1, keepdims=True)

    lax.fori_loop(0, n_chunks_p + n_chunks_s, dest_body, starts)
    ends = starts + padded
    mrow = _iota((SUBLANES, EXT), 0)
    tile_lo = (_iota((SUBLANES, EXT), 1) * MOE_TILE).astype(F32)
    tgroup = jnp.sum(jnp.where((mrow < N_EXPERT_GROUPS - 1) & (tile_lo >= ends), 1.0, 0.0), axis=0, keepdims=True)
    n_used = jnp.sum(jnp.where(row < N_EXPERT_GROUPS, padded, 0.0), axis=0, keepdims=True) * (1.0 / MOE_TILE)
    meta = jnp.where(mrow == 0, tgroup, jnp.where(mrow == 1, n_used, 0.0))
    meta_ref[...] = meta.astype(jnp.int32)


def _group_sort(ext_p, ext_s):
    n_p, n_s = ext_p.shape[0], ext_s.shape[0]
    ncp, ncs = n_p // SORT_CHUNK, n_s // SORT_CHUNK
    n_tiles = (n_p + n_s) // MOE_TILE + N_EXPERT_GROUPS
    assert n_tiles <= EXT
    dest, meta = pl.pallas_call(
        functools.partial(_sort_kernel, n_chunks_p=ncp, n_chunks_s=ncs),
        out_shape=(
            jax.ShapeDtypeStruct((ncp + ncs, SORT_CHUNK), jnp.int32),
            jax.ShapeDtypeStruct((SUBLANES, EXT), jnp.int32),
        ),
        grid=(1,),
        in_specs=[pl.BlockSpec((n_p, EXT), lambda i: (0, 0)), pl.BlockSpec((n_s, EXT), lambda i: (0, 0))],
        out_specs=(
            pl.BlockSpec((ncp + ncs, SORT_CHUNK), lambda i: (0, 0)),
            pl.BlockSpec((SUBLANES, EXT), lambda i: (0, 0)),
        ),
        scratch_shapes=[pltpu.VMEM((ncp + ncs, SUBLANES, SORT_CHUNK), F32)],
        compiler_params=_cparams("arbitrary"),
        name="group_sort",
    )(ext_p, ext_s)
    return dest.reshape(-1), meta[0, :n_tiles], meta[1, :1], n_tiles


def _scatter_kernel(dest_ref, rp_ref, rs_ref, init_ref, out_ref, sem, *, tiles_p):
    del init_ref
    i = pl.program_id(0)
    base = i * ROW_TILE

    def run(src_ref):
        def issue(k2, a):
            for prio in range(N_DMA_PRIORITIES):
                k = N_DMA_PRIORITIES * k2 + prio
                pltpu.make_async_copy(src_ref.at[pl.ds(k, 1)], out_ref.at[pl.ds(dest_ref[base + k], 1)],
                                      sem).start(priority=prio)
            return a

        def wait(k, a):
            pltpu.make_async_copy(src_ref.at[pl.ds(0, 1)], out_ref.at[pl.ds(0, 1)], sem).wait()
            return a

        lax.fori_loop(0, ROW_TILE // N_DMA_PRIORITIES, issue, 0, unroll=4)
        lax.fori_loop(0, ROW_TILE, wait, 0, unroll=8)

    @pl.when(i < tiles_p)
    def _():
        run(rp_ref)

    @pl.when(i >= tiles_p)
    def _():
        run(rs_ref)


def _scatter_rows(dest, rows_p, rows_s, n_pad):
    n_p, n_s = rows_p.shape[0], rows_s.shape[0]
    w = rows_p.shape[1]
    tiles_p, tiles_s = n_p // ROW_TILE, n_s // ROW_TILE
    init = jnp.zeros((n_pad, w), rows_p.dtype)
    any_spec = pl.BlockSpec(memory_space=pl.ANY)
    return pl.pallas_call(
        functools.partial(_scatter_kernel, tiles_p=tiles_p),
        out_shape=jax.ShapeDtypeStruct((n_pad, w), rows_p.dtype),
        grid_spec=pltpu.PrefetchScalarGridSpec(
            num_scalar_prefetch=1,
            grid=(tiles_p + tiles_s,),
            in_specs=[
                pl.BlockSpec((ROW_TILE, w), lambda i, dst: (jnp.minimum(i, tiles_p - 1), 0)),
                pl.BlockSpec((ROW_TILE, w), lambda i, dst: (jnp.maximum(i - tiles_p, 0), 0)),
                any_spec,
            ],
            out_specs=any_spec,
            scratch_shapes=[pltpu.SemaphoreType.DMA(())],
        ),
        input_output_aliases={3: 0},
        compiler_params=_cparams("arbitrary"),
        name="scatter_rows",
    )(dest, rows_p, rows_s, init)


def _ffn_kernel(tg_ref, nu_ref, rows_ref, w1_ref, w3_ref, w2_ref, out_ref):
    i = pl.program_id(0)

    @pl.when(i < nu_ref[0])
    def _():
        x = rows_ref[:, 0:D_MODEL].astype(BF16)
        ext = rows_ref[:, D_MODEL:]
        lane = _iota(ext.shape, 1)
        base = EXT_EXPERT0 + EXPERTS_PER_GROUP * tg_ref[i]
        acc = jnp.zeros(out_ref.shape, F32)
        for e in range(EXPERTS_PER_GROUP):
            a = jnp.dot(x, w1_ref[e], preferred_element_type=F32)
            b = jnp.dot(x, w3_ref[e], preferred_element_type=F32)
            comb = jnp.sum(jnp.where(lane == base + e, ext, 0.0), axis=-1, keepdims=True)
            hid = (a * jax.nn.sigmoid(a)) * b * comb
            acc = acc + jnp.dot(hid.astype(BF16), w2_ref[e], preferred_element_type=F32)
        out_ref[...] = acc

    @pl.when(i >= nu_ref[0])
    def _():
        out_ref[...] = jnp.zeros(out_ref.shape, F32)


def _moe_ffn(tile_group, n_used, rows_sorted, w1, w3, w2, n_tiles):
    e, d, f = w1.shape[1:]
    wspec = lambda s1, s2: pl.BlockSpec((None, e, s1, s2), lambda i, tg, nu: (tg[i], 0, 0, 0))
    return pl.pallas_call(
        _ffn_kernel,
        out_shape=jax.ShapeDtypeStruct((n_tiles * MOE_TILE, d), F32),
        grid_spec=pltpu.PrefetchScalarGridSpec(
            num_scalar_prefetch=2,
            grid=(n_tiles,),
            in_specs=[
                pl.BlockSpec((MOE_TILE, ROW_W), lambda i, tg, nu: (i, 0)),
                wspec(d, f), wspec(d, f), wspec(f, d),
            ],
            out_specs=pl.BlockSpec((MOE_TILE, d), lambda i, tg, nu: (i, 0)),
        ),
        compiler_params=_cparams("arbitrary"),
        name="moe_ffn",
    )(tile_group, n_used, rows_sorted, w1, w3, w2)


def _ln2_kernel(dest_ref, x_ref, fs_ref, m_ref, mn_ref, lng_ref, lnb_ref, x2_ref, hn_ref, *rest,
                alpha, tile, row0, steps_per_seq, n_steps, dils):
    grouped_refs = rest[:len(dils)]
    fbuf, sem = rest[len(dils):len(dils) + 2]
    step = pl.program_id(0) * steps_per_seq + pl.program_id(1)
    slot = lax.rem(step, 2)

    def start_tile(st, sl):
        base = row0 + st * tile

        def issue(k2, a):
            for prio in range(N_DMA_PRIORITIES):
                k = N_DMA_PRIORITIES * k2 + prio
                pltpu.make_async_copy(fs_ref.at[pl.ds(dest_ref[base + k], 1)], fbuf.at[sl, pl.ds(k, 1)],
                                      sem.at[sl]).start(priority=prio)
            return a

        lax.fori_loop(0, tile // N_DMA_PRIORITIES, issue, 0, unroll=4)

    @pl.when(step == 0)
    def _():
        start_tile(0, 0)

    @pl.when(step + 1 < n_steps)
    def _():
        start_tile(step + 1, 1 - slot)

    def wait(k, a):
        pltpu.make_async_copy(fs_ref.at[pl.ds(0, 1)], fbuf.at[slot, pl.ds(0, 1)], sem.at[slot]).wait()
        return a

    lax.fori_loop(0, tile, wait, 0, unroll=8)
    m = m_ref[...]
    x2 = _layer_norm(alpha * x_ref[...] + _mod_chunk(m, 5) * fbuf[slot], lng_ref[...], lnb_ref[...])
    x2_ref[...] = x2
    mn = mn_ref[...]
    hn = x2 * (1.0 + _mod_chunk(mn, 1)) + _mod_chunk(mn, 0)
    hn_ref[...] = hn.astype(hn_ref.dtype)
    if dils:
        hbuf = rest[-1]
        n_blk = D_MODEL // LANES
        for j in range(n_blk):
            hbuf[j] = hn[:, j * LANES:(j + 1) * LANES]
        for gref, dil in zip(grouped_refs, dils):
            for r in range(dil):
                rows = jnp.concatenate([hbuf[j, pl.ds(r, tile // dil, stride=dil), :] for j in range(n_blk)], axis=-1)
                gref[r] = rows.astype(gref.dtype)


def _mod_spec(tile, d, per_token):
    if per_token:
        return pl.BlockSpec((None, tile, 6 * d), lambda bi, i, *_: (bi, i, 0))
    return pl.BlockSpec((None, 1, 6 * d), lambda bi, i, *_: (bi, 0, 0))


def _ln2(dest, x, f_sorted, row0, m, m_next, ln_g, ln_b, alpha, per_token, dils=()):
    b, s, d = x.shape
    tile = min(ROW_TILE, s)
    nt = s // tile
    mspec = _mod_spec(tile, d, per_token)
    tok = pl.BlockSpec((None, tile, d), lambda bi, i, dst: (bi, i, 0))
    vec = pl.BlockSpec((1, d), lambda bi, i, dst: (0, 0))
    kern = functools.partial(_ln2_kernel, alpha=alpha, tile=tile, row0=row0, steps_per_seq=nt, n_steps=b * nt,
                             dils=tuple(dils))
    grouped_shapes = tuple(jax.ShapeDtypeStruct((b, dil, s // dil, d), BF16) for dil in dils)
    grouped_specs = tuple(pl.BlockSpec((None, dil, tile // dil, d), lambda bi, i, dst: (bi, 0, i, 0)) for dil in dils)
    scratch = [pltpu.VMEM((2, tile, d), F32), pltpu.SemaphoreType.DMA((2,))]
    if dils:
        scratch.append(pltpu.VMEM((d // LANES, tile, LANES), F32))
    return pl.pallas_call(
        kern,
        out_shape=(jax.ShapeDtypeStruct((b, s, d), F32), jax.ShapeDtypeStruct((b, s, d), BF16)) + grouped_shapes,
        grid_spec=pltpu.PrefetchScalarGridSpec(
            num_scalar_prefetch=1,
            grid=(b, nt),
            in_specs=[tok, pl.BlockSpec(memory_space=pl.ANY), mspec, mspec, vec, vec],
            out_specs=(tok, tok) + grouped_specs,
            scratch_shapes=scratch,
        ),
        compiler_params=_cparams("arbitrary", "arbitrary"),
        name="ln2",
    )(dest, x, f_sorted, m, m_next, ln_g, ln_b)


def _rope_tables(pos):
    half = HEAD_DIM // 2
    inv = ROPE_THETA ** (-jnp.arange(half, dtype=F32) / half)
    ang = pos.astype(F32)[:, None] * inv[None, :]
    cos, sin = jnp.cos(ang), jnp.sin(ang)
    reps = LANES // HEAD_DIM
    return (jnp.tile(jnp.concatenate([cos, cos], axis=-1), (1, reps)),
            jnp.tile(jnp.concatenate([-sin, sin], axis=-1), (1, reps)))


def _qkv_kernel(*refs, tile, grouped, tail_chunks, tail_first, tail_rows, tiles_per_res):
    h_refs = refs[0:N_DIL]
    w_ref = refs[N_DIL]
    cos_refs = refs[N_DIL + 1:2 * N_DIL + 1]
    sin_refs = refs[2 * N_DIL + 1:3 * N_DIL + 1]
    outs = refs[3 * N_DIL + 1:]
    main_refs = outs[:N_DIL] if grouped else ()
    tail_refs = outs[len(main_refs):]
    c = pl.program_id(0)
    i = pl.program_id(2)
    first_half = (_iota((tile, LANES), 1) & (HEAD_DIM - 1)) < (HEAD_DIM // 2)

    def rope(acc, cos, sin):
        blocks = []
        for j in range(D_MODEL // LANES):
            blk = acc[:, j * LANES:(j + 1) * LANES]
            partner = jnp.where(first_half, pltpu.roll(blk, LANES - HEAD_DIM // 2, 1), pltpu.roll(blk, HEAD_DIM // 2, 1))
            blocks.append(blk * cos + partner * sin)
        return jnp.concatenate(blocks, axis=-1)

    for grp in range(N_DIL):
        @pl.when(c == grp)
        def _(grp=grp):
            ii = i & (tiles_per_res[grp] - 1)
            h = h_refs[grp][...]
            cos, sin = cos_refs[grp][...], sin_refs[grp][...]
            for j in range(3):
                acc = jnp.dot(h, w_ref[:, j * D_MODEL:(j + 1) * D_MODEL], preferred_element_type=F32)
                res = rope(acc, cos, sin) if j < 2 else acc
                if grouped:
                    main_refs[grp][j] = res.astype(BF16)
                for t, tref in enumerate(tail_refs):
                    if tail_chunks[t] == 3 * grp + j:
                        @pl.when(ii >= tail_first[t])
                        def _(tref=tref, t=t, res=res):
                            tref[...] = res[tile - tail_rows[t]:, :]


def _qkv_proj(hs, w, tables, tail_keep, grouped, max_tile=ROW_TILE):
    b, d = hs[0].shape[0], hs[0].shape[-1]
    dils = tuple(h.shape[1] for h in hs)
    subs = tuple(h.shape[2] for h in hs)
    tile = min(max_tile, min(subs))
    nt = dils[0] * subs[0] // tile
    tpr = tuple(sub // tile for sub in subs)
    assert all(t & (t - 1) == 0 and dil * t == nt for t, dil in zip(tpr, dils))
    n_chunks = w.shape[1] // d
    tail_chunks = tuple(sorted(tail_keep))
    tail_rows = tuple(min(tail_keep[c], tile) for c in tail_chunks)
    tail_blocks = tuple(tail_keep[c] // r for c, r in zip(tail_chunks, tail_rows))
    tail_first = tuple(tpr[c // 3] - nb for c, nb in zip(tail_chunks, tail_blocks))

    def parked(grp, own, last):
        def imap(c, bi, i):
            before, after = c < grp, c > grp
            return tuple(jnp.where(before, 0, jnp.where(after, l, o)) for o, l in zip(own(bi, i), last))
        return imap

    in_specs, cos_specs, sin_specs = [], [], []
    out_shape, out_specs = [], []
    for grp in range(N_DIL):
        dil, t = dils[grp], tpr[grp]
        in_specs.append(pl.BlockSpec((None, None, tile, d), parked(
            grp, lambda bi, i, t=t: (bi, i // t, i % t, 0), (b - 1, dil - 1, t - 1, 0))))
        tab = pl.BlockSpec((None, tile, LANES), parked(grp, lambda bi, i, t=t: (i // t, i % t, 0), (dil - 1, t - 1, 0)))
        cos_specs.append(tab)
        sin_specs.append(tab)
        if grouped:
            out_shape.append(jax.ShapeDtypeStruct((3, b, dil, subs[grp], d), BF16))
            out_specs.append(pl.BlockSpec((3, None, None, tile, d), parked(
                grp, lambda bi, i, t=t: (0, bi, i // t, i % t, 0), (0, b - 1, dil - 1, t - 1, 0))))
    for c_t, rows, first, nb in zip(tail_chunks, tail_rows, tail_first, tail_blocks):
        dil, t = dils[c_t // 3], tpr[c_t // 3]
        out_shape.append(jax.ShapeDtypeStruct((b, dil, tail_keep[c_t], d), F32))
        out_specs.append(pl.BlockSpec((None, None, rows, d), parked(
            c_t // 3, lambda bi, i, t=t, first=first: (bi, i // t, jnp.maximum(i % t - first, 0), 0),
            (b - 1, dil - 1, nb - 1, 0))))
    kern = functools.partial(_qkv_kernel, tile=tile, grouped=grouped, tail_chunks=tail_chunks, tail_first=tail_first,
                             tail_rows=tail_rows, tiles_per_res=tpr)
    return pl.pallas_call(
        kern,
        out_shape=tuple(out_shape),
        grid=(n_chunks // 3, b, nt),
        in_specs=in_specs + [pl.BlockSpec((d, 3 * d), lambda c, bi, i: (0, c))] + cos_specs + sin_specs,
        out_specs=tuple(out_specs),
        compiler_params=_cparams("arbitrary", "arbitrary", "arbitrary"),
        name="qkv_proj",
    )(*hs, w, *[t[0] for t in tables], *[t[1] for t in tables])


def _attn_prompt_kernel(q_ref, kp_ref, kc_ref, vp_ref, vc_ref, o_ref, lse_ref):
    i = pl.program_id(2)
    tq = ATT_BAND
    lane = _iota((tq, LANES), 1)
    low = lane < HEAD_DIM
    qi = _iota((2 * tq, 2 * tq), 0) & (tq - 1)
    kj = _iota((2 * tq, 2 * tq), 1)
    mask = ((kj < tq) & (kj >= qi) & (i > 0)) | ((kj >= tq) & ((kj - tq) <= qi))
    lse_all = jnp.zeros((tq, LANES), F32)
    zero = jnp.zeros((), q_ref.dtype)
    for hp in range(N_HEADS // 2):
        sl = slice(hp * LANES, (hp + 1) * LANES)
        q = q_ref[:, sl]
        q2 = jnp.concatenate([jnp.where(low, q, zero), jnp.where(low, zero, q)], axis=0)
        kw = jnp.concatenate([kp_ref[:, sl], kc_ref[:, sl]], axis=0)
        vw = jnp.concatenate([vp_ref[:, sl], vc_ref[:, sl]], axis=0)
        s = lax.dot_general(q2, kw, _NT, preferred_element_type=F32) * (HEAD_DIM ** -0.5)
        s = jnp.where(mask, s, -jnp.inf)
        mx = jnp.max(s, axis=-1, keepdims=True)
        p = jnp.exp(s - mx)
        den = jnp.sum(p, axis=-1, keepdims=True)
        o2 = jnp.dot(p.astype(vw.dtype), vw, preferred_element_type=F32) * (1.0 / den)
        o_ref[:, sl] = jnp.where(low, o2[:tq], o2[tq:]).astype(o_ref.dtype)
        lse2 = mx + jnp.log(den)
        lse_all = jnp.where(lane == 2 * hp, lse2[:tq], jnp.where(lane == 2 * hp + 1, lse2[tq:], lse_all))
    lse_ref[...] = lse_all


def _attn_prompt(qkv_g, grp):
    _, b, dil, sub, d = qkv_g.shape
    tq = ATT_BAND
    cur = lambda which: pl.BlockSpec((None, None, None, tq, d), lambda bi, r, i: (which, bi, r, i, 0))
    prev = lambda which: pl.BlockSpec((None, None, None, tq, d), lambda bi, r, i: (which, bi, r, jnp.maximum(i - 1, 0), 0))
    return pl.pallas_call(
        _attn_prompt_kernel,
        out_shape=(jax.ShapeDtypeStruct((b, dil, sub, d), BF16), jax.ShapeDtypeStruct((b, dil, sub, LANES), F32)),
        grid=(b, dil, sub // tq),
        in_specs=[cur(0), prev(1), cur(1), prev(2), cur(2)],
        out_specs=(
            pl.BlockSpec((None, None, tq, d), lambda bi, r, i: (bi, r, i, 0)),
            pl.BlockSpec((None, None, tq, LANES), lambda bi, r, i: (bi, r, i, 0)),
        ),
        compiler_params=_cparams("arbitrary", "arbitrary", "arbitrary"),
        name=f"attn_prompt_g{grp}",
    )(qkv_g, qkv_g, qkv_g, qkv_g, qkv_g)


def _attn_sample_kernel(qkn_ref, k0_ref, v0_ref, k1_ref, v1_ref, k2_ref, v2_ref, o_ref, *, n_new):
    rows = k0_ref.shape[0]
    hpb = rows // HEAD_DIM
    nq = n_new * hpb
    shift = hpb.bit_length() - 1
    scale = HEAD_DIM ** -0.5
    neg = F32(-jnp.inf)
    own = (_iota((nq, rows), 1) >> 6) == (_iota((nq, rows), 0) & (hpb - 1))
    n_pad = 2 * SUBLANES
    pad = jnp.zeros((n_pad - n_new, rows), F32)
    scores, values = [], []
    for grp, ((_, dil), k_ref, v_ref) in enumerate(zip(DIL_GROUPS, (k0_ref, k1_ref, k2_ref), (v0_ref, v1_ref, v2_ref))):
        q = qkn_ref[0, grp * n_new:(grp + 1) * n_new, :]
        q_rows = jnp.concatenate([jnp.broadcast_to(q[t:t + 1, :], (hpb, rows)) for t in range(n_new)], axis=0)
        qbd = jnp.where(own, q_rows, 0.0).astype(BF16)
        n_buf = k_ref.shape[1]
        s = jnp.dot(qbd, k_ref[...].astype(BF16), preferred_element_type=F32) * scale
        pos = _iota((nq, n_buf), 1)
        t_row = _iota((nq, n_buf), 0) >> shift
        ok = (pos >= t_row) if dil == 1 else ((pos & (dil - 1)) == t_row)
        scores.append(jnp.where(ok, s, neg))
        values.append((v_ref[...].astype(BF16), _NT))
        k_new = jnp.concatenate([qkn_ref[1, grp * n_new:(grp + 1) * n_new, :], pad], axis=0).astype(BF16)
        v_new = jnp.concatenate([qkn_ref[2, grp * n_new:(grp + 1) * n_new, :], pad], axis=0).astype(BF16)
        s_new = lax.dot_general(qbd, k_new, _NT, preferred_element_type=F32) * scale
        t2 = _iota((nq, n_pad), 1)
        tq = _iota((nq, n_pad), 0) >> shift
        ok_new = (t2 <= tq) & (((tq - t2) & (dil - 1)) == 0)
        scores.append(jnp.where(ok_new, s_new, neg))
        values.append((v_new, (((1,), (0,)), ((), ()))))
    mx = functools.reduce(jnp.maximum, [jnp.max(s, axis=-1, keepdims=True) for s in scores])
    den = jnp.zeros((nq, 1), F32)
    acc = jnp.zeros((nq, rows), F32)
    for s, (v, dims) in zip(scores, values):
        p = jnp.exp(s - mx)
        den = den + jnp.sum(p, axis=-1, keepdims=True)
        acc = acc + lax.dot_general(p.astype(BF16), v, dims, preferred_element_type=F32)
    acc = jnp.where(own, acc * (1.0 / den), 0.0)
    out = jnp.zeros((SUBLANES, rows), F32)
    out_r = _iota((SUBLANES, rows), 0)
    for t in range(n_new):
        out = jnp.where(out_r == t, jnp.sum(acc[t * hpb:(t + 1) * hpb, :], axis=0, keepdims=True), out)
    o_ref[...] = out[0:n_new, :]


def _attn_sample(qkn, caches_t, n_new, rows=512):
    b, d = qkn.shape[0], qkn.shape[-1]
    assert n_new <= SUBLANES and (rows // HEAD_DIM) & (rows // HEAD_DIM - 1) == 0
    cache_args, cache_specs = [], []
    for (win, dil), (ck, cv) in zip(DIL_GROUPS, caches_t):
        assert ck.shape == (b, d, win) and win // dil == ATT_BAND and (dil == 1 or dil >= n_new)
        for c in (ck, cv):
            cache_args.append(c)
            cache_specs.append(pl.BlockSpec((None, rows, win), lambda bi, hh: (bi, hh, 0)))
    return pl.pallas_call(
        functools.partial(_attn_sample_kernel, n_new=n_new),
        out_shape=jax.ShapeDtypeStruct((b, n_new, d), F32),
        grid=(b, d // rows),
        in_specs=[pl.BlockSpec((None, 3, qkn.shape[2], rows), lambda bi, hh: (bi, 0, 0, hh))] + cache_specs,
        out_specs=pl.BlockSpec((None, n_new, rows), lambda bi, hh: (bi, 0, hh)),
        compiler_params=_cparams("arbitrary", "arbitrary"),
        name="attn_sample",
    )(qkn, *cache_args)


def _oproj_kernel(*refs, dils, n_lse, alpha, tile):
    n_groups = len(dils)
    o_refs = refs[:n_groups]
    lse_refs = refs[n_groups:n_groups + n_lse]
    k = n_groups + n_lse
    x_ref, m_ref, wo_ref, hx_ref, lng_ref, lnb_ref, wr_ref, br_ref = refs[k:k + 8]
    x1_ref, hrow_ref, ext_ref = refs[k + 8:k + 11]
    obuf, lbuf = refs[k + 11:]

    def natural(ref, buf, g, dil):
        if dil == 1:
            return ref[0].astype(F32)
        n_blk = ref.shape[-1] // LANES
        for r in range(dil):
            rows = ref[r].astype(F32)
            for j in range(n_blk):
                buf[g * n_blk + j, pl.ds(r, tile // dil, stride=dil), :] = rows[:, j * LANES:(j + 1) * LANES]
        return jnp.concatenate([buf[g * n_blk + j] for j in range(n_blk)], axis=-1)

    os_ = [natural(o_refs[g], obuf, g, dils[g]) for g in range(n_groups)]
    if n_lse == 0:
        o = os_[0].astype(BF16)
    else:
        lses = [natural(lse_refs[g], lbuf, g, dils[g]) for g in range(n_groups)]
        mx = functools.reduce(jnp.maximum, lses)
        es = [jnp.exp(l - mx) for l in lses]
        inv = 1.0 / functools.reduce(lambda a, b: a + b, es)
        o = jnp.zeros(x_ref.shape, F32)
        hx = hx_ref[...]
        for e, og in zip(es, os_):
            wgt = e * inv
            hi = wgt.astype(BF16)
            lo = (wgt - hi.astype(F32)).astype(BF16)
            wexp = jnp.dot(hi, hx, preferred_element_type=F32) + jnp.dot(lo, hx, preferred_element_type=F32)
            o = o + wexp * og
        o = o.astype(BF16)
    y = jnp.dot(o, wo_ref[...], preferred_element_type=F32)
    x1, h2, ext = _post_mixer(x_ref[...], y, m_ref[...], lng_ref, lnb_ref, wr_ref, br_ref, alpha)
    x1_ref[...] = x1
    hrow_ref[:, 0:D_MODEL] = h2
    hrow_ref[:, D_MODEL:] = ext
    ext_ref[...] = ext


def _out_proj(os_, lses, x, m, w_o, head_expand, ln_g, ln_b, wr, br, alpha, per_token):
    b, s, d = x.shape
    tile = min(ROW_TILE, s)
    nt = s // tile
    dils = tuple(o.shape[1] for o in os_)
    tok = pl.BlockSpec((None, tile, d), lambda bi, i: (bi, i, 0))
    grouped = lambda dil, w: pl.BlockSpec((None, dil, tile // dil, w), lambda bi, i: (bi, 0, i, 0))
    full = lambda bi, i: (0, 0)
    return pl.pallas_call(
        functools.partial(_oproj_kernel, dils=dils, n_lse=len(lses), alpha=alpha, tile=tile),
        out_shape=(
            jax.ShapeDtypeStruct((b, s, d), F32),
            jax.ShapeDtypeStruct((b * s, ROW_W), F32),
            jax.ShapeDtypeStruct((b * s, EXT), F32),
        ),
        grid=(b, nt),
        in_specs=[grouped(dil, d) for dil in dils] + [grouped(dil, LANES) for dil in dils[:len(lses)]] + [
            tok, _mod_spec(tile, d, per_token),
            pl.BlockSpec((d, d), full),
            pl.BlockSpec((LANES, d), full),
            pl.BlockSpec((1, d), full),
            pl.BlockSpec((1, d), full),
            pl.BlockSpec((2, d, EXT), lambda *_: (0, 0, 0)),
            pl.BlockSpec((1, EXT), full),
        ],
        out_specs=(
            tok,
            pl.BlockSpec((tile, ROW_W), lambda bi, i: (bi * nt + i, 0)),
            pl.BlockSpec((tile, EXT), lambda bi, i: (bi * nt + i, 0)),
        ),
        scratch_shapes=[pltpu.VMEM((len(dils) * d // LANES, tile, LANES), F32), pltpu.VMEM((len(dils), tile, LANES), F32)],
        compiler_params=_cparams("arbitrary", "arbitrary"),
        name="out_proj",
    )(*os_, *lses, x, m, w_o, head_expand, ln_g, ln_b, wr, br)


def _router_matrix(w_group, b_group, w_er, b_er):
    d = w_group.shape[0]
    n_e = N_EXPERT_GROUPS * EXPERTS_PER_GROUP
    w_e = jnp.transpose(w_er, (1, 0, 2)).reshape(d, n_e)
    pad = EXT - N_EXPERT_GROUPS - n_e
    wr = jnp.concatenate([w_group, w_e, jnp.zeros((d, pad), F32)], axis=1)
    br = jnp.concatenate([b_group, b_er.reshape(n_e), jnp.zeros((pad,), F32)])[None, :]
    wr_hi = wr.astype(BF16)
    wr_lo = (wr - wr_hi.astype(F32)).astype(BF16)
    return jnp.stack([wr_hi, wr_lo]), br


def _moe_experts(rows_p, rows_s, ext_p, ext_s, w1, w3, w2):
    dest, tile_group, n_used, n_tiles = _group_sort(ext_p, ext_s)
    sorted_rows = _scatter_rows(dest, rows_p, rows_s, n_tiles * MOE_TILE)
    return dest, _moe_ffn(tile_group, n_used, sorted_rows, w1, w3, w2, n_tiles)


def kernel(x_prompt, x_sample, state_pool, cache_k_w128, cache_v_w128, cache_k_w512, cache_v_w512, cache_k_w2048, cache_v_w2048, c_prompt, c_sample, mod_w, mod_b, ln1_g, ln1_b, ln2_g, ln2_b, pool_w, pool_scale, attn_w_qkv, attn_w_o, moe_w_group, moe_b_group, moe_w_expert_router, moe_b_expert_router, moe_w1, moe_w3, moe_w2):
    depth = mod_w.shape[0]
    alpha = float((2.0 * depth) ** 0.25)
    bp, s, d = x_prompt.shape
    bs, t_new, _ = x_sample.shape
    n_p, n_s = bp * s, bs * t_new
    kv_caches = ((cache_k_w128, cache_v_w128), (cache_k_w512, cache_v_w512), (cache_k_w2048, cache_v_w2048))

    c_all = jnp.concatenate([c_sample, c_prompt, jnp.zeros((SUBLANES - bp % SUBLANES, d), F32)], axis=0)
    m_all = _modulation(c_all, mod_w, mod_b)
    m_dec = [m_all[i, :bs] for i in range(depth)]
    m_tok = [jnp.repeat(m, t_new, axis=0).reshape(1, n_s, 6 * d) for m in m_dec]
    m_seq = [m_all[i, bs:bs + bp].reshape(bp, 1, 6 * d) for i in range(depth)]

    head_expand = jnp.asarray(np.arange(LANES)[:, None] == (np.arange(d)[None, :] // HEAD_DIM), BF16)
    dils = tuple(dil for _, dil in DIL_GROUPS)
    cos_p, sin_p = _rope_tables(jnp.arange(s, dtype=jnp.int32))
    by_residue = lambda t, dil: jnp.transpose(t.reshape(s // dil, dil, LANES), (1, 0, 2))
    tables_p = [(by_residue(cos_p, dil), by_residue(sin_p, dil)) for dil in dils]
    cos_s, sin_s = _rope_tables(PAST_LEN + (jnp.arange(n_s, dtype=jnp.int32) % t_new))
    tables_s = [(cos_s[None], sin_s[None])] * N_DIL

    xp, xs = x_prompt, x_sample.reshape(1, n_s, d)
    hp_groups = hs = None
    pool_p, pool_s = [], []
    kv_p = [[] for _ in range(2 * N_DIL)]
    kv_s = [[] for _ in range(2 * N_DIL)]
    for i in range(depth):
        li = i // 2
        wr, br = _router_matrix(moe_w_group[i], moe_b_group[i], moe_w_expert_router[i], moe_b_expert_router[i])
        ln1g, ln1b = ln1_g[i][None, :], ln1_b[i][None, :]
        if i % 2 == 0:
            pw = pool_w[li].astype(BF16)
            ps = pool_scale[li][None, :]
            xp, rows_p, ext_p, tail = _pool_prompt(xp, m_seq[i], pw, ps, ln1g, ln1b, wr, br, alpha)
            x1s, rows_s, ext_s, znew = _pool_sample(
                state_pool[li].reshape(bs, POOL_BUF * d), xs.reshape(bs, t_new * d),
                m_dec[i], pw, ps, ln1g, ln1b, wr, br, alpha)
            xs = x1s.reshape(1, n_s, d)
            rows_s = rows_s.reshape(n_s, ROW_W)
            ext_s = ext_s.reshape(n_s, EXT)
            pool_p.append(tail)
            pool_s.append(znew.reshape(bs, t_new, d))
        else:
            wqkv = attn_w_qkv[li].astype(BF16)
            wo = attn_w_o[li].astype(BF16)
            kv_chunks = [3 * g + j for g in range(N_DIL) for j in (1, 2)]
            keep_p = {3 * g + j: min(win, s) // dil for g, (win, dil) in enumerate(DIL_GROUPS) for j in (1, 2)}
            *qkv_groups, = _qkv_proj(hp_groups, wqkv, tables_p, keep_p, grouped=True)
            tails_p = qkv_groups[N_DIL:]
            rows_all = _qkv_proj([hs[:, None]] * N_DIL, wqkv, tables_s, {c: n_s for c in range(3 * N_DIL)},
                                 grouped=False, max_tile=ROW_TILE // 2)
            for j, c in enumerate(kv_chunks):
                tail = jnp.transpose(tails_p[j], (0, 2, 1, 3))
                kv_p[j].append(tail.reshape(bp, -1, N_HEADS, HEAD_DIM))
                kv_s[j].append(rows_all[c].reshape(bs, t_new, N_HEADS, HEAD_DIM))
            outs = [_attn_prompt(qkv_groups[g], g) for g in range(N_DIL)]
            xp, rows_p, ext_p = _out_proj([o for o, _ in outs], [l for _, l in outs], xp, m_seq[i], wo, head_expand,
                                          ln1g, ln1b, wr, br, alpha, per_token=False)
            kinds = [jnp.concatenate([rows_all[3 * g + j].reshape(bs, t_new, d) for g in range(N_DIL)], axis=1)
                     for j in range(3)]
            qkn = jnp.pad(jnp.stack(kinds, axis=1), ((0, 0), (0, 0), (0, 2 * SUBLANES - N_DIL * t_new), (0, 0)))
            caches_t = [tuple(jnp.transpose(c[li], (0, 2, 3, 1)).reshape(bs, d, c.shape[2]) for c in kv)
                        for kv in kv_caches]
            o_s = _attn_sample(qkn, caches_t, t_new)
            xs, rows_s, ext_s = _out_proj([o_s.reshape(1, 1, n_s, d)], [], xs, m_tok[i], wo, head_expand,
                                          ln1g, ln1b, wr, br, alpha, per_token=True)
        dest, f_sorted = _moe_experts(rows_p, rows_s, ext_p, ext_s,
                                      moe_w1[i].astype(BF16), moe_w3[i].astype(BF16), moe_w2[i].astype(BF16))
        nxt = min(i + 1, depth - 1)
        ln2g, ln2b = ln2_g[i][None, :], ln2_b[i][None, :]
        regroup = tuple(dil for dil in dils if dil > 1) if (i + 1 < depth and (i + 1) % 2 == 1) else ()
        xp, hp, *hp_dilated = _ln2(dest, xp, f_sorted, 0, m_seq[i], m_seq[nxt], ln2g, ln2b, alpha, per_token=False,
                                   dils=regroup)
        if regroup:
            hp_dilated = iter(hp_dilated)
            hp_groups = [hp[:, None] if dil == 1 else next(hp_dilated) for dil in dils]
        xs, hs = _ln2(dest, xs, f_sorted, n_p, m_tok[i], m_tok[nxt], ln2g, ln2b, alpha, per_token=True)

    stack = lambda lst: jnp.stack(lst, axis=0)
    return (xp, xs.reshape(bs, t_new, d), stack(pool_p), stack(pool_s),
            *[stack(kv_p[j]) for j in range(2 * N_DIL)], *[stack(kv_s[j]) for j in range(2 * N_DIL)])
```

```python
import functools

import numpy as np
import jax
import jax.numpy as jnp
from jax import lax
from jax.experimental import pallas as pl
from jax.experimental.pallas import tpu as pltpu

F32 = jnp.float32
BF16 = jnp.bfloat16
HIGHEST = lax.Precision.HIGHEST

D_MODEL = 1024
POOL_WINDOWS = (2, 4, 8, 16)
N_POOL_GROUPS = len(POOL_WINDOWS)
POOL_GROUP = D_MODEL // N_POOL_GROUPS
POOL_BUF = max(POOL_WINDOWS) - 1
POOL_HALO = 16
HEAD_DIM = 64
N_HEADS = D_MODEL // HEAD_DIM
DIL_GROUPS = ((128, 1), (512, 4), (2048, 16))
N_DIL = len(DIL_GROUPS)
ATT_BAND = 128
ROPE_THETA = 10000.0
N_EXPERT_GROUPS = 4
EXPERTS_PER_GROUP = 8
PAST_LEN = 2048
LN_EPS = 1e-5

LANES = 128
SUBLANES = 8
VMEM_LIMIT_BYTES = 56 * 1024 * 1024

EXT = LANES
EXT_EXPERT0 = N_EXPERT_GROUPS
EXT_ONEHOT0 = 120
ROW_W = D_MODEL + EXT

MOE_TILE = 512
SORT_CHUNK = 512
ROW_TILE = 512
N_DMA_PRIORITIES = 2

_NT = (((1,), (1,)), ((), ()))


def _cparams(*sem):
    return pltpu.CompilerParams(dimension_semantics=sem, vmem_limit_bytes=VMEM_LIMIT_BYTES)


def _iota(shape, dim):
    return lax.broadcasted_iota(jnp.int32, shape, dim)


def _mod_kernel(c_ref, w_ref, b_ref, o_ref):
    o_ref[...] = jnp.dot(c_ref[...], w_ref[...], precision=HIGHEST, preferred_element_type=F32) + b_ref[...]


def _modulation(c_all, mod_w, mod_b):
    n_layers, d, n_out = mod_w.shape
    rows = c_all.shape[0]
    tn = n_out // 4
    return pl.pallas_call(
        _mod_kernel,
        out_shape=jax.ShapeDtypeStruct((n_layers, rows, n_out), F32),
        grid=(n_layers, n_out // tn),
        in_specs=[
            pl.BlockSpec((rows, d), lambda l, j: (0, 0)),
            pl.BlockSpec((None, d, tn), lambda l, j: (l, 0, j)),
            pl.BlockSpec((None, 1, tn), lambda l, j: (l, 0, j)),
        ],
        out_specs=pl.BlockSpec((None, rows, tn), lambda l, j: (l, 0, j)),
        compiler_params=_cparams("arbitrary", "arbitrary"),
        name="modulation",
    )(c_all, mod_w, mod_b.reshape(n_layers, 1, n_out))


def _mod_chunk(m, j):
    return m[:, j * D_MODEL:(j + 1) * D_MODEL]


def _layer_norm(u, g, b):
    mu = jnp.mean(u, axis=-1, keepdims=True)
    uc = u - mu
    var = jnp.mean(uc * uc, axis=-1, keepdims=True)
    return uc * lax.rsqrt(var + LN_EPS) * g + b


def _route(h2, wr_ref, br_ref):
    h_hi = h2.astype(BF16)
    h_lo = (h2 - h_hi.astype(F32)).astype(BF16)
    dot = lambda a, b: jnp.dot(a, b, preferred_element_type=F32)
    logits = dot(h_hi, wr_ref[0]) + dot(h_lo, wr_ref[0]) + dot(h_hi, wr_ref[1]) + br_ref[...]
    lane_f = _iota(logits.shape, 1).astype(F32)
    neg = F32(-jnp.inf)
    big = F32(EXT)
    gl = jnp.where(lane_f < N_EXPERT_GROUPS, logits, neg)
    gmax = jnp.max(gl, axis=-1, keepdims=True)
    gidx = jnp.min(jnp.where(gl == gmax, lane_f, big), axis=-1, keepdims=True)
    g_p = 1.0 / jnp.sum(jnp.exp(gl - gmax), axis=-1, keepdims=True)
    e_lo = EXT_EXPERT0 + EXPERTS_PER_GROUP * gidx
    el = jnp.where((lane_f >= e_lo) & (lane_f < e_lo + EXPERTS_PER_GROUP), logits, neg)
    v1 = jnp.max(el, axis=-1, keepdims=True)
    i1 = jnp.min(jnp.where(el == v1, lane_f, big), axis=-1, keepdims=True)
    el2 = jnp.where(lane_f == i1, neg, el)
    v2 = jnp.max(el2, axis=-1, keepdims=True)
    i2 = jnp.min(jnp.where(el2 == v2, lane_f, big), axis=-1, keepdims=True)
    r = jnp.exp(v2 - v1)
    w1 = g_p / (1.0 + r)
    w2 = w1 * r
    ext = jnp.where(lane_f == i1, w1, 0.0) + jnp.where(lane_f == i2, w2, 0.0)
    return ext + jnp.where(lane_f == gidx + EXT_ONEHOT0, 1.0, 0.0)


def _post_mixer(x, y, m, lng_ref, lnb_ref, wr_ref, br_ref, alpha):
    x1 = _layer_norm(alpha * x + _mod_chunk(m, 2) * y, lng_ref[...], lnb_ref[...])
    h2 = x1 * (1.0 + _mod_chunk(m, 4)) + _mod_chunk(m, 3)
    return x1, h2, _route(h2, wr_ref, br_ref)


def _pool_prompt_kernel(x_ref, xprev_ref, m_ref, pw_ref, ps_ref, lng_ref, lnb_ref, wr_ref, br_ref,
                        x1_ref, hrow_ref, ext_ref, tail_ref, zbuf, *, tile, alpha):
    i = pl.program_id(1)
    m = m_ref[...]
    shift, scale = _mod_chunk(m, 0), _mod_chunk(m, 1)
    x = x_ref[...]
    z = x * (1.0 + scale) + shift
    zprev = xprev_ref[...] * (1.0 + scale) + shift
    zbuf[0:POOL_HALO, :] = jnp.where(i > 0, zprev, 0.0)
    zbuf[POOL_HALO:, :] = z
    pos = i * tile + _iota((tile, 1), 0)
    ys = []
    for g, w in enumerate(POOL_WINDOWS):
        cols = slice(g * POOL_GROUP, (g + 1) * POOL_GROUP)
        win = zbuf[pl.ds(POOL_HALO, tile), cols]
        for j in range(1, w):
            win = win + zbuf[pl.ds(POOL_HALO - j, tile), cols]
        cnt = jnp.minimum(pos + 1, w).astype(F32)
        d = win / cnt - z[:, cols]
        ys.append(jnp.dot(d.astype(BF16), pw_ref[g], preferred_element_type=F32))
    y = jnp.concatenate(ys, axis=-1) * ps_ref[...]
    x1, h2, ext = _post_mixer(x, y, m, lng_ref, lnb_ref, wr_ref, br_ref, alpha)
    x1_ref[...] = x1
    hrow_ref[:, 0:D_MODEL] = h2
    hrow_ref[:, D_MODEL:] = ext
    ext_ref[...] = ext

    @pl.when(i == pl.num_programs(1) - 1)
    def _():
        tail_ref[...] = zbuf[pl.ds(POOL_HALO + tile - POOL_BUF, POOL_BUF), :]


def _pool_prompt(x, m, pool_w, pool_scale, ln_g, ln_b, wr, br, alpha, tile=256):
    b, s, d = x.shape
    nt = s // tile
    halo_blocks = tile // POOL_HALO
    full = lambda bi, i: (0, 0)
    kern = functools.partial(_pool_prompt_kernel, tile=tile, alpha=alpha)
    return pl.pallas_call(
        kern,
        out_shape=(
            jax.ShapeDtypeStruct((b, s, d), F32),
            jax.ShapeDtypeStruct((b * s, ROW_W), F32),
            jax.ShapeDtypeStruct((b * s, EXT), F32),
            jax.ShapeDtypeStruct((b, POOL_BUF, d), F32),
        ),
        grid=(b, nt),
        in_specs=[
            pl.BlockSpec((None, tile, d), lambda bi, i: (bi, i, 0)),
            pl.BlockSpec((None, POOL_HALO, d), lambda bi, i: (bi, jnp.maximum(i * halo_blocks - 1, 0), 0)),
            pl.BlockSpec((None, 1, 6 * d), lambda bi, i: (bi, 0, 0)),
            pl.BlockSpec((N_POOL_GROUPS, POOL_GROUP, POOL_GROUP), lambda bi, i: (0, 0, 0)),
            pl.BlockSpec((1, d), full),
            pl.BlockSpec((1, d), full),
            pl.BlockSpec((1, d), full),
            pl.BlockSpec((2, d, EXT), lambda *_: (0, 0, 0)),
            pl.BlockSpec((1, EXT), full),
        ],
        out_specs=(
            pl.BlockSpec((None, tile, d), lambda bi, i: (bi, i, 0)),
            pl.BlockSpec((tile, ROW_W), lambda bi, i: (bi * nt + i, 0)),
            pl.BlockSpec((tile, EXT), lambda bi, i: (bi * nt + i, 0)),
            pl.BlockSpec((None, POOL_BUF, d), lambda bi, i: (bi, 0, 0)),
        ),
        scratch_shapes=[pltpu.VMEM((tile + POOL_HALO, d), F32)],
        compiler_params=_cparams("arbitrary", "arbitrary"),
        name="pool_prompt",
    )(x, x, m, pool_w, pool_scale, ln_g, ln_b, wr, br)


def _pool_sample_kernel(st_ref, x_ref, m_ref, pw_ref, ps_ref, lng_ref, lnb_ref, wr_ref, br_ref,
                        x1_ref, hrow_ref, ext_ref, znew_ref, *, n_new, alpha):
    d = D_MODEL
    rows = [st_ref[:, r * d:(r + 1) * d] for r in range(POOL_BUF)]
    xs = []
    m = m_ref[...]
    for t in range(n_new):
        x = x_ref[:, t * d:(t + 1) * d]
        z = x * (1.0 + _mod_chunk(m, 1)) + _mod_chunk(m, 0)
        znew_ref[:, t * d:(t + 1) * d] = z
        rows.append(z)
        xs.append(x)
    for t in range(n_new):
        last = POOL_BUF + t
        ys = []
        for g, w in enumerate(POOL_WINDOWS):
            cols = slice(g * POOL_GROUP, (g + 1) * POOL_GROUP)
            win = rows[last][:, cols]
            for j in range(1, w):
                win = win + rows[last - j][:, cols]
            dgrp = win / F32(w) - rows[last][:, cols]
            ys.append(jnp.dot(dgrp.astype(BF16), pw_ref[g], preferred_element_type=F32))
        y = jnp.concatenate(ys, axis=-1) * ps_ref[...]
        x1, h2, ext = _post_mixer(xs[t], y, m, lng_ref, lnb_ref, wr_ref, br_ref, alpha)
        x1_ref[:, t * d:(t + 1) * d] = x1
        hrow_ref[:, t * ROW_W:t * ROW_W + d] = h2
        hrow_ref[:, t * ROW_W + d:(t + 1) * ROW_W] = ext
        ext_ref[:, t * EXT:(t + 1) * EXT] = ext


def _pool_sample(state, x, m, pool_w, pool_scale, ln_g, ln_b, wr, br, alpha, bb=32):
    b = x.shape[0]
    d = D_MODEL
    n_new = x.shape[1] // d
    full = lambda i: (0, 0)
    kern = functools.partial(_pool_sample_kernel, n_new=n_new, alpha=alpha)
    return pl.pallas_call(
        kern,
        out_shape=(
            jax.ShapeDtypeStruct((b, n_new * d), F32),
            jax.ShapeDtypeStruct((b, n_new * ROW_W), F32),
            jax.ShapeDtypeStruct((b, n_new * EXT), F32),
            jax.ShapeDtypeStruct((b, n_new * d), F32),
        ),
        grid=(b // bb,),
        in_specs=[
            pl.BlockSpec((bb, POOL_BUF * d), lambda i: (i, 0)),
            pl.BlockSpec((bb, n_new * d), lambda i: (i, 0)),
            pl.BlockSpec((bb, 6 * d), lambda i: (i, 0)),
            pl.BlockSpec((N_POOL_GROUPS, POOL_GROUP, POOL_GROUP), lambda i: (0, 0, 0)),
            pl.BlockSpec((1, d), full),
            pl.BlockSpec((1, d), full),
            pl.BlockSpec((1, d), full),
            pl.BlockSpec((2, d, EXT), lambda *_: (0, 0, 0)),
            pl.BlockSpec((1, EXT), full),
        ],
        out_specs=(
            pl.BlockSpec((bb, n_new * d), lambda i: (i, 0)),
            pl.BlockSpec((bb, n_new * ROW_W), lambda i: (i, 0)),
            pl.BlockSpec((bb, n_new * EXT), lambda i: (i, 0)),
            pl.BlockSpec((bb, n_new * d), lambda i: (i, 0)),
        ),
        compiler_params=_cparams("arbitrary"),
        name="pool_sample",
    )(state, x, m, pool_w, pool_scale, ln_g, ln_b, wr, br)


def _sort_kernel(extp_ref, exts_ref, dest_ref, meta_ref, oh_ref, *, n_chunks_p, n_chunks_s):
    ch = SORT_CHUNK
    r_io = _iota((SUBLANES, EXT), 0)
    l_io = _iota((SUBLANES, EXT), 1)
    sel = jnp.where((l_io == r_io + EXT_ONEHOT0) & (r_io < N_EXPERT_GROUPS), 1.0, 0.0).astype(BF16)

    def count_from(ext_ref, chunk0):
        def body(c, cnt):
            ext = ext_ref[pl.ds(pl.multiple_of(c * ch, ch), ch), :]
            oh = lax.dot_general(sel, ext.astype(BF16), _NT, preferred_element_type=F32)
            oh_ref[chunk0 + c] = oh
            return cnt + jnp.sum(oh, axis=-1, keepdims=True)
        return body

    counts = lax.fori_loop(0, n_chunks_p, count_from(extp_ref, 0), jnp.zeros((SUBLANES, 1), F32))
    counts = lax.fori_loop(0, n_chunks_s, count_from(exts_ref, n_chunks_p), counts)
    padded = jnp.floor((counts + (MOE_TILE - 1)) * (1.0 / MOE_TILE)) * MOE_TILE
    row = _iota((SUBLANES, 1), 0)
    starts = jnp.zeros((SUBLANES, 1), F32)
    for g in range(1, N_EXPERT_GROUPS):
        starts = starts + jnp.where(row >= g, padded[g - 1:g, :], 0.0)
    tri = jnp.where(_iota((ch, ch), 0) < _iota((ch, ch), 1), 1.0, 0.0).astype(BF16)

    def dest_body(c, base):
        oh = oh_ref[c]
        pre = jnp.dot(oh.astype(BF16), tri, preferred_element_type=F32)
        dest = jnp.sum(oh * (base + pre), axis=0, keepdims=True)
        dest_ref[pl.ds(c, 1), :] = dest.astype(jnp.int32)
        return base + jnp.sum(oh, axis=-1, keepdims=True)

    lax.fori_loop(0, n_chunks_p + n_chunks_s, dest_body, starts)
    ends = starts + padded
    mrow = _iota((SUBLANES, EXT), 0)
    tile_lo = (_iota((SUBLANES, EXT), 1) * MOE_TILE).astype(F32)
    tgroup = jnp.sum(jnp.where((mrow < N_EXPERT_GROUPS - 1) & (tile_lo >= ends), 1.0, 0.0), axis=0, keepdims=True)
    n_used = jnp.sum(jnp.where(row < N_EXPERT_GROUPS, padded, 0.0), axis=0, keepdims=True) * (1.0 / MOE_TILE)
    meta = jnp.where(mrow == 0, tgroup, jnp.where(mrow == 1, n_used, 0.0))
    meta_ref[...] = meta.astype(jnp.int32)


def _group_sort(ext_p, ext_s):
    n_p, n_s = ext_p.shape[0], ext_s.shape[0]
    ncp, ncs = n_p // SORT_CHUNK, n_s // SORT_CHUNK
    n_tiles = (n_p + n_s) // MOE_TILE + N_EXPERT_GROUPS
    assert n_tiles <= EXT
    dest, meta = pl.pallas_call(
        functools.partial(_sort_kernel, n_chunks_p=ncp, n_chunks_s=ncs),
        out_shape=(
            jax.ShapeDtypeStruct((ncp + ncs, SORT_CHUNK), jnp.int32),
            jax.ShapeDtypeStruct((SUBLANES, EXT), jnp.int32),
        ),
        grid=(1,),
        in_specs=[pl.BlockSpec((n_p, EXT), lambda i: (0, 0)), pl.BlockSpec((n_s, EXT), lambda i: (0, 0))],
        out_specs=(
            pl.BlockSpec((ncp + ncs, SORT_CHUNK), lambda i: (0, 0)),
            pl.BlockSpec((SUBLANES, EXT), lambda i: (0, 0)),
        ),
        scratch_shapes=[pltpu.VMEM((ncp + ncs, SUBLANES, SORT_CHUNK), F32)],
        compiler_params=_cparams("arbitrary"),
        name="group_sort",
    )(ext_p, ext_s)
    return dest.reshape(-1), meta[0, :n_tiles], meta[1, :1], n_tiles


def _scatter_kernel(dest_ref, rp_ref, rs_ref, init_ref, out_ref, sem, *, tiles_p):
    del init_ref
    i = pl.program_id(0)
    base = i * ROW_TILE

    def run(src_ref):
        def issue(k2, a):
            for prio in range(N_DMA_PRIORITIES):
                k = N_DMA_PRIORITIES * k2 + prio
                pltpu.make_async_copy(src_ref.at[pl.ds(k, 1)], out_ref.at[pl.ds(dest_ref[base + k], 1)],
                                      sem).start(priority=prio)
            return a

        def wait(k, a):
            pltpu.make_async_copy(src_ref.at[pl.ds(0, 1)], out_ref.at[pl.ds(0, 1)], sem).wait()
            return a

        lax.fori_loop(0, ROW_TILE // N_DMA_PRIORITIES, issue, 0, unroll=4)
        lax.fori_loop(0, ROW_TILE, wait, 0, unroll=8)

    @pl.when(i < tiles_p)
    def _():
        run(rp_ref)

    @pl.when(i >= tiles_p)
    def _():
        run(rs_ref)


def _scatter_rows(dest, rows_p, rows_s, n_pad):
    n_p, n_s = rows_p.shape[0], rows_s.shape[0]
    w = rows_p.shape[1]
    tiles_p, tiles_s = n_p // ROW_TILE, n_s // ROW_TILE
    init = jnp.zeros((n_pad, w), rows_p.dtype)
    any_spec = pl.BlockSpec(memory_space=pl.ANY)
    return pl.pallas_call(
        functools.partial(_scatter_kernel, tiles_p=tiles_p),
        out_shape=jax.ShapeDtypeStruct((n_pad, w), rows_p.dtype),
        grid_spec=pltpu.PrefetchScalarGridSpec(
            num_scalar_prefetch=1,
            grid=(tiles_p + tiles_s,),
            in_specs=[
                pl.BlockSpec((ROW_TILE, w), lambda i, dst: (jnp.minimum(i, tiles_p - 1), 0)),
                pl.BlockSpec((ROW_TILE, w), lambda i, dst: (jnp.maximum(i - tiles_p, 0), 0)),
                any_spec,
            ],
            out_specs=any_spec,
            scratch_shapes=[pltpu.SemaphoreType.DMA(())],
        ),
        input_output_aliases={3: 0},
        compiler_params=_cparams("arbitrary"),
        name="scatter_rows",
    )(dest, rows_p, rows_s, init)


def _ffn_kernel(tg_ref, nu_ref, rows_ref, w1_ref, w3_ref, w2_ref, out_ref):
    i = pl.program_id(0)

    @pl.when(i < nu_ref[0])
    def _():
        x = rows_ref[:, 0:D_MODEL].astype(BF16)
        ext = rows_ref[:, D_MODEL:]
        lane = _iota(ext.shape, 1)
        base = EXT_EXPERT0 + EXPERTS_PER_GROUP * tg_ref[i]
        acc = jnp.zeros(out_ref.shape, F32)
        for e in range(EXPERTS_PER_GROUP):
            a = jnp.dot(x, w1_ref[e], preferred_element_type=F32)
            b = jnp.dot(x, w3_ref[e], preferred_element_type=F32)
            comb = jnp.sum(jnp.where(lane == base + e, ext, 0.0), axis=-1, keepdims=True)
            hid = (a * jax.nn.sigmoid(a)) * b * comb
            acc = acc + jnp.dot(hid.astype(BF16), w2_ref[e], preferred_element_type=F32)
        out_ref[...] = acc

    @pl.when(i >= nu_ref[0])
    def _():
        out_ref[...] = jnp.zeros(out_ref.shape, F32)


def _moe_ffn(tile_group, n_used, rows_sorted, w1, w3, w2, n_tiles, layer):
    e, d, f = w1.shape[2:]
    wspec = lambda s1, s2: pl.BlockSpec((None, None, e, s1, s2), lambda i, tg, nu: (layer, tg[i], 0, 0, 0))
    return pl.pallas_call(
        _ffn_kernel,
        out_shape=jax.ShapeDtypeStruct((n_tiles * MOE_TILE, d), F32),
        grid_spec=pltpu.PrefetchScalarGridSpec(
            num_scalar_prefetch=2,
            grid=(n_tiles,),
            in_specs=[
                pl.BlockSpec((MOE_TILE, ROW_W), lambda i, tg, nu: (i, 0)),
                wspec(d, f), wspec(d, f), wspec(f, d),
            ],
            out_specs=pl.BlockSpec((MOE_TILE, d), lambda i, tg, nu: (i, 0)),
        ),
        compiler_params=_cparams("arbitrary"),
        name="moe_ffn",
    )(tile_group, n_used, rows_sorted, w1, w3, w2)


def _ln2_kernel(dest_ref, x_ref, fs_ref, m_ref, mn_ref, lng_ref, lnb_ref, x2_ref, hn_ref, *rest,
                alpha, tile, row0, steps_per_seq, n_steps, dils):
    grouped_refs = rest[:len(dils)]
    fbuf, sem = rest[len(dils):len(dils) + 2]
    step = pl.program_id(0) * steps_per_seq + pl.program_id(1)
    slot = lax.rem(step, 2)

    def start_tile(st, sl):
        base = row0 + st * tile

        def issue(k2, a):
            for prio in range(N_DMA_PRIORITIES):
                k = N_DMA_PRIORITIES * k2 + prio
                pltpu.make_async_copy(fs_ref.at[pl.ds(dest_ref[base + k], 1)], fbuf.at[sl, pl.ds(k, 1)],
                                      sem.at[sl]).start(priority=prio)
            return a

        lax.fori_loop(0, tile // N_DMA_PRIORITIES, issue, 0, unroll=4)

    @pl.when(step == 0)
    def _():
        start_tile(0, 0)

    @pl.when(step + 1 < n_steps)
    def _():
        start_tile(step + 1, 1 - slot)

    def wait(k, a):
        pltpu.make_async_copy(fs_ref.at[pl.ds(0, 1)], fbuf.at[slot, pl.ds(0, 1)], sem.at[slot]).wait()
        return a

    lax.fori_loop(0, tile, wait, 0, unroll=8)
    m = m_ref[...]
    x2 = _layer_norm(alpha * x_ref[...] + _mod_chunk(m, 5) * fbuf[slot], lng_ref[...], lnb_ref[...])
    x2_ref[...] = x2
    mn = mn_ref[...]
    hn = x2 * (1.0 + _mod_chunk(mn, 1)) + _mod_chunk(mn, 0)
    hn_ref[...] = hn.astype(hn_ref.dtype)
    if dils:
        hbuf = rest[-1]
        n_blk = D_MODEL // LANES
        for j in range(n_blk):
            hbuf[j] = hn[:, j * LANES:(j + 1) * LANES]
        for gref, dil in zip(grouped_refs, dils):
            for r in range(dil):
                rows = jnp.concatenate([hbuf[j, pl.ds(r, tile // dil, stride=dil), :] for j in range(n_blk)], axis=-1)
                gref[r] = rows.astype(gref.dtype)


def _mod_spec(tile, d, per_token):
    if per_token:
        return pl.BlockSpec((None, tile, 6 * d), lambda bi, i, *_: (bi, i, 0))
    return pl.BlockSpec((None, 1, 6 * d), lambda bi, i, *_: (bi, 0, 0))


def _ln2(dest, x, f_sorted, row0, m, m_next, ln_g, ln_b, alpha, per_token, dils=()):
    b, s, d = x.shape
    tile = min(ROW_TILE, s)
    nt = s // tile
    mspec = _mod_spec(tile, d, per_token)
    tok = pl.BlockSpec((None, tile, d), lambda bi, i, dst: (bi, i, 0))
    vec = pl.BlockSpec((1, d), lambda bi, i, dst: (0, 0))
    kern = functools.partial(_ln2_kernel, alpha=alpha, tile=tile, row0=row0, steps_per_seq=nt, n_steps=b * nt,
                             dils=tuple(dils))
    grouped_shapes = tuple(jax.ShapeDtypeStruct((b, dil, s // dil, d), BF16) for dil in dils)
    grouped_specs = tuple(pl.BlockSpec((None, dil, tile // dil, d), lambda bi, i, dst: (bi, 0, i, 0)) for dil in dils)
    scratch = [pltpu.VMEM((2, tile, d), F32), pltpu.SemaphoreType.DMA((2,))]
    if dils:
        scratch.append(pltpu.VMEM((d // LANES, tile, LANES), F32))
    return pl.pallas_call(
        kern,
        out_shape=(jax.ShapeDtypeStruct((b, s, d), F32), jax.ShapeDtypeStruct((b, s, d), BF16)) + grouped_shapes,
        grid_spec=pltpu.PrefetchScalarGridSpec(
            num_scalar_prefetch=1,
            grid=(b, nt),
            in_specs=[tok, pl.BlockSpec(memory_space=pl.ANY), mspec, mspec, vec, vec],
            out_specs=(tok, tok) + grouped_specs,
            scratch_shapes=scratch,
        ),
        compiler_params=_cparams("arbitrary", "arbitrary"),
        name="ln2",
    )(dest, x, f_sorted, m, m_next, ln_g, ln_b)


def _rope_tables(pos):
    half = HEAD_DIM // 2
    inv = ROPE_THETA ** (-jnp.arange(half, dtype=F32) / half)
    lane = np.arange(LANES)
    inv_lanes = inv[lane % half]
    sign = jnp.asarray(np.where(lane % HEAD_DIM < half, -1.0, 1.0), F32)
    ang = pos.astype(F32)[..., None] * inv_lanes
    return jnp.cos(ang), jnp.sin(ang) * sign


def _qkv_kernel(*refs, tile, grouped, tail_chunks, tail_first, tail_rows, tiles_per_res):
    h_refs = refs[0:N_DIL]
    w_ref = refs[N_DIL]
    cos_refs = refs[N_DIL + 1:2 * N_DIL + 1]
    sin_refs = refs[2 * N_DIL + 1:3 * N_DIL + 1]
    outs = refs[3 * N_DIL + 1:]
    main_refs = outs[:N_DIL] if grouped else ()
    tail_refs = outs[len(main_refs):]
    c = pl.program_id(0)
    i = pl.program_id(2)
    first_half = (_iota((tile, LANES), 1) & (HEAD_DIM - 1)) < (HEAD_DIM // 2)

    def rope(acc, cos, sin):
        blocks = []
        for j in range(D_MODEL // LANES):
            blk = acc[:, j * LANES:(j + 1) * LANES]
            partner = jnp.where(first_half, pltpu.roll(blk, LANES - HEAD_DIM // 2, 1), pltpu.roll(blk, HEAD_DIM // 2, 1))
            blocks.append(blk * cos + partner * sin)
        return jnp.concatenate(blocks, axis=-1)

    for grp in range(N_DIL):
        @pl.when(c == grp)
        def _(grp=grp):
            ii = i & (tiles_per_res[grp] - 1)
            h = h_refs[grp][...]
            cos, sin = cos_refs[grp][...], sin_refs[grp][...]
            for j in range(3):
                acc = jnp.dot(h, w_ref[:, j * D_MODEL:(j + 1) * D_MODEL], preferred_element_type=F32)
                res = rope(acc, cos, sin) if j < 2 else acc
                if grouped:
                    main_refs[grp][j] = ((res * (HEAD_DIM ** -0.5)) if j == 0 else res).astype(BF16)
                for t, tref in enumerate(tail_refs):
                    if tail_chunks[t] == 3 * grp + j:
                        @pl.when(ii >= tail_first[t])
                        def _(tref=tref, t=t, res=res):
                            tref[...] = res[tile - tail_rows[t]:, :]


def _qkv_proj(hs, w, tables, tail_keep, grouped, max_tile=ROW_TILE):
    b, d = hs[0].shape[0], hs[0].shape[-1]
    dils = tuple(h.shape[1] for h in hs)
    subs = tuple(h.shape[2] for h in hs)
    tile = min(max_tile, min(subs))
    nt = dils[0] * subs[0] // tile
    tpr = tuple(sub // tile for sub in subs)
    assert all(t & (t - 1) == 0 and dil * t == nt for t, dil in zip(tpr, dils))
    n_chunks = w.shape[1] // d
    tail_chunks = tuple(sorted(tail_keep))
    tail_rows = tuple(min(tail_keep[c], tile) for c in tail_chunks)
    tail_blocks = tuple(tail_keep[c] // r for c, r in zip(tail_chunks, tail_rows))
    tail_first = tuple(tpr[c // 3] - nb for c, nb in zip(tail_chunks, tail_blocks))

    def parked(grp, own, last):
        def imap(c, bi, i):
            before, after = c < grp, c > grp
            return tuple(jnp.where(before, 0, jnp.where(after, l, o)) for o, l in zip(own(bi, i), last))
        return imap

    in_specs, cos_specs, sin_specs = [], [], []
    out_shape, out_specs = [], []
    for grp in range(N_DIL):
        dil, t = dils[grp], tpr[grp]
        in_specs.append(pl.BlockSpec((None, None, tile, d), parked(
            grp, lambda bi, i, t=t: (bi, i // t, i % t, 0), (b - 1, dil - 1, t - 1, 0))))
        tab = pl.BlockSpec((None, tile, LANES), parked(grp, lambda bi, i, t=t: (i // t, i % t, 0), (dil - 1, t - 1, 0)))
        cos_specs.append(tab)
        sin_specs.append(tab)
        if grouped:
            out_shape.append(jax.ShapeDtypeStruct((3, b, dil, subs[grp], d), BF16))
            out_specs.append(pl.BlockSpec((3, None, None, tile, d), parked(
                grp, lambda bi, i, t=t: (0, bi, i // t, i % t, 0), (0, b - 1, dil - 1, t - 1, 0))))
    for c_t, rows, first, nb in zip(tail_chunks, tail_rows, tail_first, tail_blocks):
        dil, t = dils[c_t // 3], tpr[c_t // 3]
        out_shape.append(jax.ShapeDtypeStruct((b, dil, tail_keep[c_t], d), F32))
        out_specs.append(pl.BlockSpec((None, None, rows, d), parked(
            c_t // 3, lambda bi, i, t=t, first=first: (bi, i // t, jnp.maximum(i % t - first, 0), 0),
            (b - 1, dil - 1, nb - 1, 0))))
    kern = functools.partial(_qkv_kernel, tile=tile, grouped=grouped, tail_chunks=tail_chunks, tail_first=tail_first,
                             tail_rows=tail_rows, tiles_per_res=tpr)
    return pl.pallas_call(
        kern,
        out_shape=tuple(out_shape),
        grid=(n_chunks // 3, b, nt),
        in_specs=in_specs + [pl.BlockSpec((d, 3 * d), lambda c, bi, i: (0, c))] + cos_specs + sin_specs,
        out_specs=tuple(out_specs),
        compiler_params=_cparams("arbitrary", "arbitrary", "arbitrary"),
        name="qkv_proj",
    )(*hs, w, *[t[0] for t in tables], *[t[1] for t in tables])


def _attn_prompt_kernel(q_ref, kp_ref, kc_ref, vp_ref, vc_ref, o_ref, lse_ref, *, n_sub):
    i = pl.program_id(2)
    tq = ATT_BAND
    lane = _iota((tq, LANES), 1)
    low = lane < HEAD_DIM
    qi = _iota((2 * tq, 2 * tq), 0) & (tq - 1)
    kj = _iota((2 * tq, 2 * tq), 1)
    in_prev = (kj < tq) & (kj >= qi)
    in_cur = (kj >= tq) & ((kj - tq) <= qi)
    zero = jnp.zeros((), q_ref.dtype)
    for sub in range(n_sub):
        rows = slice(sub * tq, (sub + 1) * tq)
        before = slice((sub - 1) * tq, sub * tq)
        mask = ((in_prev & (i > 0)) if sub == 0 else in_prev) | in_cur
        lse_all = jnp.zeros((tq, LANES), F32)
        for hp in range(N_HEADS // 2):
            sl = slice(hp * LANES, (hp + 1) * LANES)
            q = q_ref[rows, sl]
            q2 = jnp.concatenate([jnp.where(low, q, zero), jnp.where(low, zero, q)], axis=0)
            k_prev = kp_ref[:, sl] if sub == 0 else kc_ref[before, sl]
            v_prev = vp_ref[:, sl] if sub == 0 else vc_ref[before, sl]
            kw = jnp.concatenate([k_prev, kc_ref[rows, sl]], axis=0)
            vw = jnp.concatenate([v_prev, vc_ref[rows, sl]], axis=0)
            s = lax.dot_general(q2, kw, _NT, preferred_element_type=F32)
            s = jnp.where(mask, s, -jnp.inf)
            mx = jnp.max(s, axis=-1, keepdims=True)
            p = jnp.exp(s - mx)
            den = jnp.sum(p, axis=-1, keepdims=True)
            o2 = jnp.dot(p.astype(vw.dtype), vw, preferred_element_type=F32) * (1.0 / den)
            o_ref[rows, sl] = jnp.where(low, o2[:tq], o2[tq:]).astype(o_ref.dtype)
            lse2 = mx + jnp.log(den)
            lse_all = jnp.where(lane == 2 * hp, lse2[:tq], jnp.where(lane == 2 * hp + 1, lse2[tq:], lse_all))
        lse_ref[rows, :] = lse_all


def _attn_prompt(qkv_g, grp, n_sub=2):
    _, b, dil, sub, d = qkv_g.shape
    tq = ATT_BAND
    rows = n_sub * tq
    cur = lambda which: pl.BlockSpec((None, None, None, rows, d), lambda bi, r, i: (which, bi, r, i, 0))
    prev = lambda which: pl.BlockSpec((None, None, None, tq, d),
                                      lambda bi, r, i: (which, bi, r, jnp.maximum(i * n_sub - 1, 0), 0))
    return pl.pallas_call(
        functools.partial(_attn_prompt_kernel, n_sub=n_sub),
        out_shape=(jax.ShapeDtypeStruct((b, dil, sub, d), BF16), jax.ShapeDtypeStruct((b, dil, sub, LANES), F32)),
        grid=(b, dil, sub // rows),
        in_specs=[cur(0), prev(1), cur(1), prev(2), cur(2)],
        out_specs=(
            pl.BlockSpec((None, None, rows, d), lambda bi, r, i: (bi, r, i, 0)),
            pl.BlockSpec((None, None, rows, LANES), lambda bi, r, i: (bi, r, i, 0)),
        ),
        compiler_params=_cparams("arbitrary", "arbitrary", "arbitrary"),
        name=f"attn_prompt_g{grp}",
    )(qkv_g, qkv_g, qkv_g, qkv_g, qkv_g)


def _attn_sample_kernel(qkn_ref, k0_ref, v0_ref, k1_ref, v1_ref, k2_ref, v2_ref, o_ref, *, n_new):
    rows = k0_ref.shape[0]
    hpb = rows // HEAD_DIM
    nq = n_new * hpb
    shift = hpb.bit_length() - 1
    scale = HEAD_DIM ** -0.5
    neg = F32(-jnp.inf)
    own = (_iota((nq, rows), 1) >> 6) == (_iota((nq, rows), 0) & (hpb - 1))
    n_pad = 2 * SUBLANES
    pad = jnp.zeros((n_pad - n_new, rows), F32)
    scores, values = [], []
    for grp, ((_, dil), k_ref, v_ref) in enumerate(zip(DIL_GROUPS, (k0_ref, k1_ref, k2_ref), (v0_ref, v1_ref, v2_ref))):
        q = qkn_ref[0, grp * n_new:(grp + 1) * n_new, :]
        q_rows = jnp.concatenate([jnp.broadcast_to(q[t:t + 1, :], (hpb, rows)) for t in range(n_new)], axis=0)
        qbd = jnp.where(own, q_rows, 0.0).astype(BF16)
        n_buf = k_ref.shape[1]
        s = jnp.dot(qbd, k_ref[...].astype(BF16), preferred_element_type=F32) * scale
        pos = _iota((nq, n_buf), 1)
        t_row = _iota((nq, n_buf), 0) >> shift
        ok = (pos >= t_row) if dil == 1 else ((pos & (dil - 1)) == t_row)
        scores.append(jnp.where(ok, s, neg))
        values.append((v_ref[...].astype(BF16), _NT))
        k_new = jnp.concatenate([qkn_ref[1, grp * n_new:(grp + 1) * n_new, :], pad], axis=0).astype(BF16)
        v_new = jnp.concatenate([qkn_ref[2, grp * n_new:(grp + 1) * n_new, :], pad], axis=0).astype(BF16)
        s_new = lax.dot_general(qbd, k_new, _NT, preferred_element_type=F32) * scale
        t2 = _iota((nq, n_pad), 1)
        tq = _iota((nq, n_pad), 0) >> shift
        ok_new = (t2 <= tq) & (((tq - t2) & (dil - 1)) == 0)
        scores.append(jnp.where(ok_new, s_new, neg))
        values.append((v_new, (((1,), (0,)), ((), ()))))
    mx = functools.reduce(jnp.maximum, [jnp.max(s, axis=-1, keepdims=True) for s in scores])
    den = jnp.zeros((nq, 1), F32)
    acc = jnp.zeros((nq, rows), F32)
    for s, (v, dims) in zip(scores, values):
        p = jnp.exp(s - mx)
        den = den + jnp.sum(p, axis=-1, keepdims=True)
        acc = acc + lax.dot_general(p.astype(BF16), v, dims, preferred_element_type=F32)
    acc = jnp.where(own, acc * (1.0 / den), 0.0)
    out = jnp.zeros((SUBLANES, rows), F32)
    out_r = _iota((SUBLANES, rows), 0)
    for t in range(n_new):
        out = jnp.where(out_r == t, jnp.sum(acc[t * hpb:(t + 1) * hpb, :], axis=0, keepdims=True), out)
    o_ref[...] = out[0:n_new, :]


def _attn_sample(qkn, caches_t, n_new, rows=512):
    b, d = qkn.shape[0], qkn.shape[-1]
    assert n_new <= SUBLANES and (rows // HEAD_DIM) & (rows // HEAD_DIM - 1) == 0
    cache_args, cache_specs = [], []
    for (win, dil), (ck, cv) in zip(DIL_GROUPS, caches_t):
        assert ck.shape == (b, d, win) and win // dil == ATT_BAND and (dil == 1 or dil >= n_new)
        for c in (ck, cv):
            cache_args.append(c)
            cache_specs.append(pl.BlockSpec((None, rows, win), lambda bi, hh: (bi, hh, 0)))
    return pl.pallas_call(
        functools.partial(_attn_sample_kernel, n_new=n_new),
        out_shape=jax.ShapeDtypeStruct((b, n_new, d), F32),
        grid=(b, d // rows),
        in_specs=[pl.BlockSpec((None, 3, qkn.shape[2], rows), lambda bi, hh: (bi, 0, 0, hh))] + cache_specs,
        out_specs=pl.BlockSpec((None, n_new, rows), lambda bi, hh: (bi, 0, hh)),
        compiler_params=_cparams("arbitrary", "arbitrary"),
        name="attn_sample",
    )(qkn, *cache_args)


def _oproj_kernel(*refs, dils, n_lse, alpha, tile):
    n_groups = len(dils)
    o_refs = refs[:n_groups]
    lse_refs = refs[n_groups:n_groups + n_lse]
    k = n_groups + n_lse
    x_ref, m_ref, wo_ref, hx_ref, lng_ref, lnb_ref, wr_ref, br_ref = refs[k:k + 8]
    x1_ref, hrow_ref, ext_ref = refs[k + 8:k + 11]
    obuf, lbuf = refs[k + 11:]

    def natural(ref, buf, g, dil):
        if dil == 1:
            return ref[0].astype(F32)
        n_blk = ref.shape[-1] // LANES
        for r in range(dil):
            rows = ref[r].astype(F32)
            for j in range(n_blk):
                buf[g * n_blk + j, pl.ds(r, tile // dil, stride=dil), :] = rows[:, j * LANES:(j + 1) * LANES]
        return jnp.concatenate([buf[g * n_blk + j] for j in range(n_blk)], axis=-1)

    os_ = [natural(o_refs[g], obuf, g, dils[g]) for g in range(n_groups)]
    if n_lse == 0:
        o = os_[0].astype(BF16)
    else:
        lses = [natural(lse_refs[g], lbuf, g, dils[g]) for g in range(n_groups)]
        mx = functools.reduce(jnp.maximum, lses)
        es = [jnp.exp(l - mx) for l in lses]
        inv = 1.0 / functools.reduce(lambda a, b: a + b, es)
        o = jnp.zeros(x_ref.shape, F32)
        hx = hx_ref[...]
        for e, og in zip(es, os_):
            wgt = e * inv
            hi = wgt.astype(BF16)
            lo = (wgt - hi.astype(F32)).astype(BF16)
            wexp = jnp.dot(hi, hx, preferred_element_type=F32) + jnp.dot(lo, hx, preferred_element_type=F32)
            o = o + wexp * og
        o = o.astype(BF16)
    y = jnp.dot(o, wo_ref[...], preferred_element_type=F32)
    x1, h2, ext = _post_mixer(x_ref[...], y, m_ref[...], lng_ref, lnb_ref, wr_ref, br_ref, alpha)
    x1_ref[...] = x1
    hrow_ref[:, 0:D_MODEL] = h2
    hrow_ref[:, D_MODEL:] = ext
    ext_ref[...] = ext


def _out_proj(os_, lses, x, m, w_o, head_expand, ln_g, ln_b, wr, br, alpha, per_token):
    b, s, d = x.shape
    tile = min(ROW_TILE, s)
    nt = s // tile
    dils = tuple(o.shape[1] for o in os_)
    tok = pl.BlockSpec((None, tile, d), lambda bi, i: (bi, i, 0))
    grouped = lambda dil, w: pl.BlockSpec((None, dil, tile // dil, w), lambda bi, i: (bi, 0, i, 0))
    full = lambda bi, i: (0, 0)
    return pl.pallas_call(
        functools.partial(_oproj_kernel, dils=dils, n_lse=len(lses), alpha=alpha, tile=tile),
        out_shape=(
            jax.ShapeDtypeStruct((b, s, d), F32),
            jax.ShapeDtypeStruct((b * s, ROW_W), F32),
            jax.ShapeDtypeStruct((b * s, EXT), F32),
        ),
        grid=(b, nt),
        in_specs=[grouped(dil, d) for dil in dils] + [grouped(dil, LANES) for dil in dils[:len(lses)]] + [
            tok, _mod_spec(tile, d, per_token),
            pl.BlockSpec((d, d), full),
            pl.BlockSpec((LANES, d), full),
            pl.BlockSpec((1, d), full),
            pl.BlockSpec((1, d), full),
            pl.BlockSpec((2, d, EXT), lambda *_: (0, 0, 0)),
            pl.BlockSpec((1, EXT), full),
        ],
        out_specs=(
            tok,
            pl.BlockSpec((tile, ROW_W), lambda bi, i: (bi * nt + i, 0)),
            pl.BlockSpec((tile, EXT), lambda bi, i: (bi * nt + i, 0)),
        ),
        scratch_shapes=[pltpu.VMEM((len(dils) * d // LANES, tile, LANES), F32), pltpu.VMEM((len(dils), tile, LANES), F32)],
        compiler_params=_cparams("arbitrary", "arbitrary"),
        name="out_proj",
    )(*os_, *lses, x, m, w_o, head_expand, ln_g, ln_b, wr, br)


def _router_matrix(w_group, b_group, w_er, b_er):
    d = w_group.shape[0]
    n_e = N_EXPERT_GROUPS * EXPERTS_PER_GROUP
    w_e = jnp.transpose(w_er, (1, 0, 2)).reshape(d, n_e)
    pad = EXT - N_EXPERT_GROUPS - n_e
    wr = jnp.concatenate([w_group, w_e, jnp.zeros((d, pad), F32)], axis=1)
    br = jnp.concatenate([b_group, b_er.reshape(n_e), jnp.zeros((pad,), F32)])[None, :]
    wr_hi = wr.astype(BF16)
    wr_lo = (wr - wr_hi.astype(F32)).astype(BF16)
    return jnp.stack([wr_hi, wr_lo]), br


def _moe_experts(rows_p, rows_s, ext_p, ext_s, w1, w3, w2, layer):
    dest, tile_group, n_used, n_tiles = _group_sort(ext_p, ext_s)
    sorted_rows = _scatter_rows(dest, rows_p, rows_s, n_tiles * MOE_TILE)
    return dest, _moe_ffn(tile_group, n_used, sorted_rows, w1, w3, w2, n_tiles, layer)


def kernel(x_prompt, x_sample, state_pool, cache_k_w128, cache_v_w128, cache_k_w512, cache_v_w512, cache_k_w2048, cache_v_w2048, c_prompt, c_sample, mod_w, mod_b, ln1_g, ln1_b, ln2_g, ln2_b, pool_w, pool_scale, attn_w_qkv, attn_w_o, moe_w_group, moe_b_group, moe_w_expert_router, moe_b_expert_router, moe_w1, moe_w3, moe_w2):
    depth = mod_w.shape[0]
    alpha = float((2.0 * depth) ** 0.25)
    bp, s, d = x_prompt.shape
    bs, t_new, _ = x_sample.shape
    n_p, n_s = bp * s, bs * t_new
    kv_caches = ((cache_k_w128, cache_v_w128), (cache_k_w512, cache_v_w512), (cache_k_w2048, cache_v_w2048))

    c_all = jnp.concatenate([c_sample, c_prompt, jnp.zeros((SUBLANES - bp % SUBLANES, d), F32)], axis=0)
    m_all = _modulation(c_all, mod_w, mod_b)
    m_dec = [m_all[i, :bs] for i in range(depth)]
    m_tok = [jnp.repeat(m, t_new, axis=0).reshape(1, n_s, 6 * d) for m in m_dec]
    m_seq = [m_all[i, bs:bs + bp].reshape(bp, 1, 6 * d) for i in range(depth)]

    head_expand = jnp.asarray(np.arange(LANES)[:, None] == (np.arange(d)[None, :] // HEAD_DIM), BF16)
    dils = tuple(dil for _, dil in DIL_GROUPS)
    tables_p = [_rope_tables(jnp.arange(s // dil, dtype=jnp.int32)[None, :] * dil
                             + jnp.arange(dil, dtype=jnp.int32)[:, None]) for dil in dils]
    tables_s = [_rope_tables(PAST_LEN + (jnp.arange(n_s, dtype=jnp.int32) % t_new)[None, :])] * N_DIL

    w1_bf, w3_bf, w2_bf = moe_w1.astype(BF16), moe_w3.astype(BF16), moe_w2.astype(BF16)
    xp, xs = x_prompt, x_sample.reshape(1, n_s, d)
    hp_groups = hs = None
    pool_p, pool_s = [], []
    kv_p = [[] for _ in range(2 * N_DIL)]
    kv_s = [[] for _ in range(2 * N_DIL)]
    for i in range(depth):
        li = i // 2
        wr, br = _router_matrix(moe_w_group[i], moe_b_group[i], moe_w_expert_router[i], moe_b_expert_router[i])
        ln1g, ln1b = ln1_g[i][None, :], ln1_b[i][None, :]
        if i % 2 == 0:
            pw = pool_w[li].astype(BF16)
            ps = pool_scale[li][None, :]
            xp, rows_p, ext_p, tail = _pool_prompt(xp, m_seq[i], pw, ps, ln1g, ln1b, wr, br, alpha)
            x1s, rows_s, ext_s, znew = _pool_sample(
                state_pool[li].reshape(bs, POOL_BUF * d), xs.reshape(bs, t_new * d),
                m_dec[i], pw, ps, ln1g, ln1b, wr, br, alpha)
            xs = x1s.reshape(1, n_s, d)
            rows_s = rows_s.reshape(n_s, ROW_W)
            ext_s = ext_s.reshape(n_s, EXT)
            pool_p.append(tail)
            pool_s.append(znew.reshape(bs, t_new, d))
        else:
            wqkv = attn_w_qkv[li].astype(BF16)
            wo = attn_w_o[li].astype(BF16)
            kv_chunks = [3 * g + j for g in range(N_DIL) for j in (1, 2)]
            keep_p = {3 * g + j: min(win, s) // dil for g, (win, dil) in enumerate(DIL_GROUPS) for j in (1, 2)}
            *qkv_groups, = _qkv_proj(hp_groups, wqkv, tables_p, keep_p, grouped=True)
            tails_p = qkv_groups[N_DIL:]
            rows_all = _qkv_proj([hs[:, None]] * N_DIL, wqkv, tables_s, {c: n_s for c in range(3 * N_DIL)},
                                 grouped=False, max_tile=ROW_TILE // 2)
            for j, c in enumerate(kv_chunks):
                tail = jnp.transpose(tails_p[j], (0, 2, 1, 3))
                kv_p[j].append(tail.reshape(bp, -1, N_HEADS, HEAD_DIM))
                kv_s[j].append(rows_all[c].reshape(bs, t_new, N_HEADS, HEAD_DIM))
            outs = [_attn_prompt(qkv_groups[g], g) for g in range(N_DIL)]
            xp, rows_p, ext_p = _out_proj([o for o, _ in outs], [l for _, l in outs], xp, m_seq[i], wo, head_expand,
                                          ln1g, ln1b, wr, br, alpha, per_token=False)
            kinds = [jnp.concatenate([rows_all[3 * g + j].reshape(bs, t_new, d) for g in range(N_DIL)], axis=1)
                     for j in range(3)]
            qkn = jnp.pad(jnp.stack(kinds, axis=1), ((0, 0), (0, 0), (0, 2 * SUBLANES - N_DIL * t_new), (0, 0)))
            caches_t = [tuple(jnp.transpose(c[li], (0, 2, 3, 1)).reshape(bs, d, c.shape[2]) for c in kv)
                        for kv in kv_caches]
            o_s = _attn_sample(qkn, caches_t, t_new)
            xs, rows_s, ext_s = _out_proj([o_s.reshape(1, 1, n_s, d)], [], xs, m_tok[i], wo, head_expand,
                                          ln1g, ln1b, wr, br, alpha, per_token=True)
        dest, f_sorted = _moe_experts(rows_p, rows_s, ext_p, ext_s, w1_bf, w3_bf, w2_bf, i)
        nxt = min(i + 1, depth - 1)
        ln2g, ln2b = ln2_g[i][None, :], ln2_b[i][None, :]
        regroup = tuple(dil for dil in dils if dil > 1) if (i + 1 < depth and (i + 1) % 2 == 1) else ()
        xp, hp, *hp_dilated = _ln2(dest, xp, f_sorted, 0, m_seq[i], m_seq[nxt], ln2g, ln2b, alpha, per_token=False,
                                   dils=regroup)
        if regroup:
            hp_dilated = iter(hp_dilated)
            hp_groups = [hp[:, None] if dil == 1 else next(hp_dilated) for dil in dils]
        xs, hs = _ln2(dest, xs, f_sorted, n_p, m_tok[i], m_tok[nxt], ln2g, ln2b, alpha, per_token=True)

    stack = lambda lst: jnp.stack(lst, axis=0)
    return (xp, xs.reshape(bs, t_new, d), stack(pool_p), stack(pool_s),
            *[stack(kv_p[j]) for j in range(2 * N_DIL)], *[stack(kv_s[j]) for j in range(2 * N_DIL)])
```

```python
import functools

import numpy as np
import jax
import jax.numpy as jnp
from jax import lax
from jax.experimental import pallas as pl
from jax.experimental.pallas import tpu as pltpu

F32 = jnp.float32
BF16 = jnp.bfloat16
HIGHEST = lax.Precision.HIGHEST

D_MODEL = 1024
POOL_WINDOWS = (2, 4, 8, 16)
N_POOL_GROUPS = len(POOL_WINDOWS)
POOL_GROUP = D_MODEL // N_POOL_GROUPS
POOL_BUF = max(POOL_WINDOWS) - 1
POOL_HALO = 16
HEAD_DIM = 64
N_HEADS = D_MODEL // HEAD_DIM
DIL_GROUPS = ((128, 1), (512, 4), (2048, 16))
N_DIL = len(DIL_GROUPS)
ATT_BAND = 128
ROPE_THETA = 10000.0
N_EXPERT_GROUPS = 4
EXPERTS_PER_GROUP = 8
PAST_LEN = 2048
LN_EPS = 1e-5

LANES = 128
SUBLANES = 8
VMEM_LIMIT_BYTES = 56 * 1024 * 1024

EXT = LANES
EXT_EXPERT0 = N_EXPERT_GROUPS
EXT_ONEHOT0 = 120
ROW_W = D_MODEL + EXT

MOE_TILE = 512
SORT_CHUNK = 512
ROW_TILE = 512
N_DMA_PRIORITIES = 2

_NT = (((1,), (1,)), ((), ()))


def _cparams(*sem):
    return pltpu.CompilerParams(dimension_semantics=sem, vmem_limit_bytes=VMEM_LIMIT_BYTES)


def _iota(shape, dim):
    return lax.broadcasted_iota(jnp.int32, shape, dim)


def _mod_kernel(c_ref, w_ref, b_ref, o_ref):
    o_ref[...] = jnp.dot(c_ref[...], w_ref[...], precision=HIGHEST, preferred_element_type=F32) + b_ref[...]


def _modulation(c_all, mod_w, mod_b):
    n_layers, d, n_out = mod_w.shape
    rows = c_all.shape[0]
    tn = n_out // 4
    return pl.pallas_call(
        _mod_kernel,
        out_shape=jax.ShapeDtypeStruct((n_layers, rows, n_out), F32),
        grid=(n_layers, n_out // tn),
        in_specs=[
            pl.BlockSpec((rows, d), lambda l, j: (0, 0)),
            pl.BlockSpec((None, d, tn), lambda l, j: (l, 0, j)),
            pl.BlockSpec((None, 1, tn), lambda l, j: (l, 0, j)),
        ],
        out_specs=pl.BlockSpec((None, rows, tn), lambda l, j: (l, 0, j)),
        compiler_params=_cparams("arbitrary", "arbitrary"),
        name="modulation",
    )(c_all, mod_w, mod_b.reshape(n_layers, 1, n_out))


def _mod_chunk(m, j):
    return m[:, j * D_MODEL:(j + 1) * D_MODEL]


def _layer_norm(u, g, b):
    mu = jnp.mean(u, axis=-1, keepdims=True)
    uc = u - mu
    var = jnp.mean(uc * uc, axis=-1, keepdims=True)
    return uc * lax.rsqrt(var + LN_EPS) * g + b


def _route(h2, wr_ref, br_ref):
    h_hi = h2.astype(BF16)
    h_lo = (h2 - h_hi.astype(F32)).astype(BF16)
    dot = lambda a, b: jnp.dot(a, b, preferred_element_type=F32)
    logits = dot(h_hi, wr_ref[0]) + dot(h_lo, wr_ref[0]) + dot(h_hi, wr_ref[1]) + br_ref[...]
    lane_f = _iota(logits.shape, 1).astype(F32)
    neg = F32(-jnp.inf)
    big = F32(EXT)
    gl = jnp.where(lane_f < N_EXPERT_GROUPS, logits, neg)
    gmax = jnp.max(gl, axis=-1, keepdims=True)
    gidx = jnp.min(jnp.where(gl == gmax, lane_f, big), axis=-1, keepdims=True)
    g_p = 1.0 / jnp.sum(jnp.exp(gl - gmax), axis=-1, keepdims=True)
    e_lo = EXT_EXPERT0 + EXPERTS_PER_GROUP * gidx
    el = jnp.where((lane_f >= e_lo) & (lane_f < e_lo + EXPERTS_PER_GROUP), logits, neg)
    v1 = jnp.max(el, axis=-1, keepdims=True)
    i1 = jnp.min(jnp.where(el == v1, lane_f, big), axis=-1, keepdims=True)
    el2 = jnp.where(lane_f == i1, neg, el)
    v2 = jnp.max(el2, axis=-1, keepdims=True)
    i2 = jnp.min(jnp.where(el2 == v2, lane_f, big), axis=-1, keepdims=True)
    r = jnp.exp(v2 - v1)
    w1 = g_p / (1.0 + r)
    w2 = w1 * r
    ext = jnp.where(lane_f == i1, w1, 0.0) + jnp.where(lane_f == i2, w2, 0.0)
    return ext + jnp.where(lane_f == gidx + EXT_ONEHOT0, 1.0, 0.0)


def _post_mixer(x, y, m, lng_ref, lnb_ref, wr_ref, br_ref, alpha):
    x1 = _layer_norm(alpha * x + _mod_chunk(m, 2) * y, lng_ref[...], lnb_ref[...])
    h2 = x1 * (1.0 + _mod_chunk(m, 4)) + _mod_chunk(m, 3)
    return x1, h2, _route(h2, wr_ref, br_ref)


def _pool_prompt_kernel(x_ref, xprev_ref, m_ref, pw_ref, ps_ref, lng_ref, lnb_ref, wr_ref, br_ref,
                        x1_ref, hrow_ref, ext_ref, tail_ref, zbuf, *, tile, alpha):
    i = pl.program_id(1)
    m = m_ref[...]
    shift, scale = _mod_chunk(m, 0), _mod_chunk(m, 1)
    x = x_ref[...]
    z = x * (1.0 + scale) + shift
    zprev = xprev_ref[...] * (1.0 + scale) + shift
    zbuf[0:POOL_HALO, :] = jnp.where(i > 0, zprev, 0.0)
    zbuf[POOL_HALO:, :] = z
    pos = i * tile + _iota((tile, 1), 0)
    ys = []
    for g, w in enumerate(POOL_WINDOWS):
        cols = slice(g * POOL_GROUP, (g + 1) * POOL_GROUP)
        win = zbuf[pl.ds(POOL_HALO, tile), cols]
        for j in range(1, w):
            win = win + zbuf[pl.ds(POOL_HALO - j, tile), cols]
        cnt = jnp.minimum(pos + 1, w).astype(F32)
        d = win / cnt - z[:, cols]
        ys.append(jnp.dot(d.astype(BF16), pw_ref[g], preferred_element_type=F32))
    y = jnp.concatenate(ys, axis=-1) * ps_ref[...]
    x1, h2, ext = _post_mixer(x, y, m, lng_ref, lnb_ref, wr_ref, br_ref, alpha)
    x1_ref[...] = x1
    hrow_ref[:, 0:D_MODEL] = h2
    hrow_ref[:, D_MODEL:] = ext
    ext_ref[...] = ext

    @pl.when(i == pl.num_programs(1) - 1)
    def _():
        tail_ref[...] = zbuf[pl.ds(POOL_HALO + tile - POOL_BUF, POOL_BUF), :]


def _pool_prompt(x, m, pool_w, pool_scale, ln_g, ln_b, wr, br, alpha, tile=256):
    b, s, d = x.shape
    nt = s // tile
    halo_blocks = tile // POOL_HALO
    full = lambda bi, i: (0, 0)
    kern = functools.partial(_pool_prompt_kernel, tile=tile, alpha=alpha)
    return pl.pallas_call(
        kern,
        out_shape=(
            jax.ShapeDtypeStruct((b, s, d), F32),
            jax.ShapeDtypeStruct((b * s, ROW_W), F32),
            jax.ShapeDtypeStruct((b * s, EXT), F32),
            jax.ShapeDtypeStruct((b, POOL_BUF, d), F32),
        ),
        grid=(b, nt),
        in_specs=[
            pl.BlockSpec((None, tile, d), lambda bi, i: (bi, i, 0)),
            pl.BlockSpec((None, POOL_HALO, d), lambda bi, i: (bi, jnp.maximum(i * halo_blocks - 1, 0), 0)),
            pl.BlockSpec((None, 1, 6 * d), lambda bi, i: (bi, 0, 0)),
            pl.BlockSpec((N_POOL_GROUPS, POOL_GROUP, POOL_GROUP), lambda bi, i: (0, 0, 0)),
            pl.BlockSpec((1, d), full),
            pl.BlockSpec((1, d), full),
            pl.BlockSpec((1, d), full),
            pl.BlockSpec((2, d, EXT), lambda *_: (0, 0, 0)),
            pl.BlockSpec((1, EXT), full),
        ],
        out_specs=(
            pl.BlockSpec((None, tile, d), lambda bi, i: (bi, i, 0)),
            pl.BlockSpec((tile, ROW_W), lambda bi, i: (bi * nt + i, 0)),
            pl.BlockSpec((tile, EXT), lambda bi, i: (bi * nt + i, 0)),
            pl.BlockSpec((None, POOL_BUF, d), lambda bi, i: (bi, 0, 0)),
        ),
        scratch_shapes=[pltpu.VMEM((tile + POOL_HALO, d), F32)],
        compiler_params=_cparams("arbitrary", "arbitrary"),
        name="pool_prompt",
    )(x, x, m, pool_w, pool_scale, ln_g, ln_b, wr, br)


def _pool_sample_kernel(st_ref, x_ref, m_ref, pw_ref, ps_ref, lng_ref, lnb_ref, wr_ref, br_ref,
                        x1_ref, hrow_ref, ext_ref, znew_ref, *, n_new, alpha):
    d = D_MODEL
    rows = [st_ref[:, r * d:(r + 1) * d] for r in range(POOL_BUF)]
    xs = []
    m = m_ref[...]
    for t in range(n_new):
        x = x_ref[:, t * d:(t + 1) * d]
        z = x * (1.0 + _mod_chunk(m, 1)) + _mod_chunk(m, 0)
        znew_ref[:, t * d:(t + 1) * d] = z
        rows.append(z)
        xs.append(x)
    for t in range(n_new):
        last = POOL_BUF + t
        ys = []
        for g, w in enumerate(POOL_WINDOWS):
            cols = slice(g * POOL_GROUP, (g + 1) * POOL_GROUP)
            win = rows[last][:, cols]
            for j in range(1, w):
                win = win + rows[last - j][:, cols]
            dgrp = win / F32(w) - rows[last][:, cols]
            ys.append(jnp.dot(dgrp.astype(BF16), pw_ref[g], preferred_element_type=F32))
        y = jnp.concatenate(ys, axis=-1) * ps_ref[...]
        x1, h2, ext = _post_mixer(xs[t], y, m, lng_ref, lnb_ref, wr_ref, br_ref, alpha)
        x1_ref[:, t * d:(t + 1) * d] = x1
        hrow_ref[:, t * ROW_W:t * ROW_W + d] = h2
        hrow_ref[:, t * ROW_W + d:(t + 1) * ROW_W] = ext
        ext_ref[:, t * EXT:(t + 1) * EXT] = ext


def _pool_sample(state, x, m, pool_w, pool_scale, ln_g, ln_b, wr, br, alpha, bb=32):
    b = x.shape[0]
    d = D_MODEL
    n_new = x.shape[1] // d
    full = lambda i: (0, 0)
    kern = functools.partial(_pool_sample_kernel, n_new=n_new, alpha=alpha)
    return pl.pallas_call(
        kern,
        out_shape=(
            jax.ShapeDtypeStruct((b, n_new * d), F32),
            jax.ShapeDtypeStruct((b, n_new * ROW_W), F32),
            jax.ShapeDtypeStruct((b, n_new * EXT), F32),
            jax.ShapeDtypeStruct((b, n_new * d), F32),
        ),
        grid=(b // bb,),
        in_specs=[
            pl.BlockSpec((bb, POOL_BUF * d), lambda i: (i, 0)),
            pl.BlockSpec((bb, n_new * d), lambda i: (i, 0)),
            pl.BlockSpec((bb, 6 * d), lambda i: (i, 0)),
            pl.BlockSpec((N_POOL_GROUPS, POOL_GROUP, POOL_GROUP), lambda i: (0, 0, 0)),
            pl.BlockSpec((1, d), full),
            pl.BlockSpec((1, d), full),
            pl.BlockSpec((1, d), full),
            pl.BlockSpec((2, d, EXT), lambda *_: (0, 0, 0)),
            pl.BlockSpec((1, EXT), full),
        ],
        out_specs=(
            pl.BlockSpec((bb, n_new * d), lambda i: (i, 0)),
            pl.BlockSpec((bb, n_new * ROW_W), lambda i: (i, 0)),
            pl.BlockSpec((bb, n_new * EXT), lambda i: (i, 0)),
            pl.BlockSpec((bb, n_new * d), lambda i: (i, 0)),
        ),
        compiler_params=_cparams("arbitrary"),
        name="pool_sample",
    )(state, x, m, pool_w, pool_scale, ln_g, ln_b, wr, br)


def _sort_kernel(extp_ref, exts_ref, dest_ref, meta_ref, oh_ref, *, n_chunks_p, n_chunks_s):
    ch = SORT_CHUNK
    r_io = _iota((SUBLANES, EXT), 0)
    l_io = _iota((SUBLANES, EXT), 1)
    sel = jnp.where((l_io == r_io + EXT_ONEHOT0) & (r_io < N_EXPERT_GROUPS), 1.0, 0.0).astype(BF16)

    def count_from(ext_ref, chunk0):
        def body(c, cnt):
            ext = ext_ref[pl.ds(pl.multiple_of(c * ch, ch), ch), :]
            oh = lax.dot_general(sel, ext.astype(BF16), _NT, preferred_element_type=F32)
            oh_ref[chunk0 + c] = oh
            return cnt + jnp.sum(oh, axis=-1, keepdims=True)
        return body

    counts = lax.fori_loop(0, n_chunks_p, count_from(extp_ref, 0), jnp.zeros((SUBLANES, 1), F32))
    counts = lax.fori_loop(0, n_chunks_s, count_from(exts_ref, n_chunks_p), counts)
    padded = jnp.floor((counts + (MOE_TILE - 1)) * (1.0 / MOE_TILE)) * MOE_TILE
    row = _iota((SUBLANES, 1), 0)
    starts = jnp.zeros((SUBLANES, 1), F32)
    for g in range(1, N_EXPERT_GROUPS):
        starts = starts + jnp.where(row >= g, padded[g - 1:g, :], 0.0)
    tri = jnp.where(_iota((ch, ch), 0) < _iota((ch, ch), 1), 1.0, 0.0).astype(BF16)

    def dest_body(c, base):
        oh = oh_ref[c]
        pre = jnp.dot(oh.astype(BF16), tri, preferred_element_type=F32)
        dest = jnp.sum(oh * (base + pre), axis=0, keepdims=True)
        dest_ref[pl.ds(c, 1), :] = dest.astype(jnp.int32)
        return base + jnp.sum(oh, axis=-1, keepdims=True)

    lax.fori_loop(0, n_chunks_p + n_chunks_s, dest_body, starts)
    ends = starts + padded
    mrow = _iota((SUBLANES, EXT), 0)
    tile_lo = (_iota((SUBLANES, EXT), 1) * MOE_TILE).astype(F32)
    tgroup = jnp.sum(jnp.where((mrow < N_EXPERT_GROUPS - 1) & (tile_lo >= ends), 1.0, 0.0), axis=0, keepdims=True)
    n_used = jnp.sum(jnp.where(row < N_EXPERT_GROUPS, padded, 0.0), axis=0, keepdims=True) * (1.0 / MOE_TILE)
    meta = jnp.where(mrow == 0, tgroup, jnp.where(mrow == 1, n_used, 0.0))
    meta_ref[...] = meta.astype(jnp.int32)


def _group_sort(ext_p, ext_s):
    n_p, n_s = ext_p.shape[0], ext_s.shape[0]
    ncp, ncs = n_p // SORT_CHUNK, n_s // SORT_CHUNK
    n_tiles = (n_p + n_s) // MOE_TILE + N_EXPERT_GROUPS
    assert n_tiles <= EXT
    dest, meta = pl.pallas_call(
        functools.partial(_sort_kernel, n_chunks_p=ncp, n_chunks_s=ncs),
        out_shape=(
            jax.ShapeDtypeStruct((ncp + ncs, SORT_CHUNK), jnp.int32),
            jax.ShapeDtypeStruct((SUBLANES, EXT), jnp.int32),
        ),
        grid=(1,),
        in_specs=[pl.BlockSpec((n_p, EXT), lambda i: (0, 0)), pl.BlockSpec((n_s, EXT), lambda i: (0, 0))],
        out_specs=(
            pl.BlockSpec((ncp + ncs, SORT_CHUNK), lambda i: (0, 0)),
            pl.BlockSpec((SUBLANES, EXT), lambda i: (0, 0)),
        ),
        scratch_shapes=[pltpu.VMEM((ncp + ncs, SUBLANES, SORT_CHUNK), F32)],
        compiler_params=_cparams("arbitrary"),
        name="group_sort",
    )(ext_p, ext_s)
    return dest.reshape(-1), meta[0, :n_tiles], meta[1, :1], n_tiles


def _scatter_kernel(dest_ref, rp_ref, rs_ref, init_ref, out_ref, sem, *, tiles_p):
    del init_ref
    i = pl.program_id(0)
    base = i * ROW_TILE

    def run(src_ref):
        def issue(k2, a):
            for prio in range(N_DMA_PRIORITIES):
                k = N_DMA_PRIORITIES * k2 + prio
                pltpu.make_async_copy(src_ref.at[pl.ds(k, 1)], out_ref.at[pl.ds(dest_ref[base + k], 1)],
                                      sem).start(priority=prio)
            return a

        def wait(k, a):
            pltpu.make_async_copy(src_ref.at[pl.ds(0, 1)], out_ref.at[pl.ds(0, 1)], sem).wait()
            return a

        lax.fori_loop(0, ROW_TILE // N_DMA_PRIORITIES, issue, 0, unroll=4)
        lax.fori_loop(0, ROW_TILE, wait, 0, unroll=8)

    @pl.when(i < tiles_p)
    def _():
        run(rp_ref)

    @pl.when(i >= tiles_p)
    def _():
        run(rs_ref)


def _scatter_rows(dest, rows_p, rows_s, n_pad):
    n_p, n_s = rows_p.shape[0], rows_s.shape[0]
    w = rows_p.shape[1]
    tiles_p, tiles_s = n_p // ROW_TILE, n_s // ROW_TILE
    init = jnp.zeros((n_pad, w), rows_p.dtype)
    any_spec = pl.BlockSpec(memory_space=pl.ANY)
    return pl.pallas_call(
        functools.partial(_scatter_kernel, tiles_p=tiles_p),
        out_shape=jax.ShapeDtypeStruct((n_pad, w), rows_p.dtype),
        grid_spec=pltpu.PrefetchScalarGridSpec(
            num_scalar_prefetch=1,
            grid=(tiles_p + tiles_s,),
            in_specs=[
                pl.BlockSpec((ROW_TILE, w), lambda i, dst: (jnp.minimum(i, tiles_p - 1), 0)),
                pl.BlockSpec((ROW_TILE, w), lambda i, dst: (jnp.maximum(i - tiles_p, 0), 0)),
                any_spec,
            ],
            out_specs=any_spec,
            scratch_shapes=[pltpu.SemaphoreType.DMA(())],
        ),
        input_output_aliases={3: 0},
        compiler_params=_cparams("arbitrary"),
        name="scatter_rows",
    )(dest, rows_p, rows_s, init)


def _ffn_kernel(tg_ref, nu_ref, rows_ref, w1_ref, w3_ref, w2_ref, out_ref):
    i = pl.program_id(0)

    @pl.when(i < nu_ref[0])
    def _():
        x = rows_ref[:, 0:D_MODEL].astype(BF16)
        ext = rows_ref[:, D_MODEL:]
        lane = _iota(ext.shape, 1)
        base = EXT_EXPERT0 + EXPERTS_PER_GROUP * tg_ref[i]
        acc = jnp.zeros(out_ref.shape, F32)
        for e in range(EXPERTS_PER_GROUP):
            a = jnp.dot(x, w1_ref[e], preferred_element_type=F32)
            b = jnp.dot(x, w3_ref[e], preferred_element_type=F32)
            comb = jnp.sum(jnp.where(lane == base + e, ext, 0.0), axis=-1, keepdims=True)
            hid = (a * jax.nn.sigmoid(a)) * b * comb
            acc = acc + jnp.dot(hid.astype(BF16), w2_ref[e], preferred_element_type=F32)
        out_ref[...] = acc

    @pl.when(i >= nu_ref[0])
    def _():
        out_ref[...] = jnp.zeros(out_ref.shape, F32)


def _moe_ffn(tile_group, n_used, rows_sorted, w1, w3, w2, n_tiles, layer):
    e, d, f = w1.shape[2:]
    wspec = lambda s1, s2: pl.BlockSpec((None, None, e, s1, s2), lambda i, tg, nu: (layer, tg[i], 0, 0, 0))
    return pl.pallas_call(
        _ffn_kernel,
        out_shape=jax.ShapeDtypeStruct((n_tiles * MOE_TILE, d), F32),
        grid_spec=pltpu.PrefetchScalarGridSpec(
            num_scalar_prefetch=2,
            grid=(n_tiles,),
            in_specs=[
                pl.BlockSpec((MOE_TILE, ROW_W), lambda i, tg, nu: (i, 0)),
                wspec(d, f), wspec(d, f), wspec(f, d),
            ],
            out_specs=pl.BlockSpec((MOE_TILE, d), lambda i, tg, nu: (i, 0)),
        ),
        compiler_params=_cparams("arbitrary"),
        name="moe_ffn",
    )(tile_group, n_used, rows_sorted, w1, w3, w2)


def _ln2_kernel(dest_ref, x_ref, fs_ref, m_ref, mn_ref, lng_ref, lnb_ref, x2_ref, hn_ref, *rest,
                alpha, tile, row0, steps_per_seq, n_steps, dils):
    grouped_refs = rest[:len(dils)]
    fbuf, sem = rest[len(dils):len(dils) + 2]
    step = pl.program_id(0) * steps_per_seq + pl.program_id(1)
    slot = lax.rem(step, 2)

    def start_tile(st, sl):
        base = row0 + st * tile

        def issue(k2, a):
            for prio in range(N_DMA_PRIORITIES):
                k = N_DMA_PRIORITIES * k2 + prio
                pltpu.make_async_copy(fs_ref.at[pl.ds(dest_ref[base + k], 1)], fbuf.at[sl, pl.ds(k, 1)],
                                      sem.at[sl]).start(priority=prio)
            return a

        lax.fori_loop(0, tile // N_DMA_PRIORITIES, issue, 0, unroll=4)

    @pl.when(step == 0)
    def _():
        start_tile(0, 0)

    @pl.when(step + 1 < n_steps)
    def _():
        start_tile(step + 1, 1 - slot)

    def wait(k, a):
        pltpu.make_async_copy(fs_ref.at[pl.ds(0, 1)], fbuf.at[slot, pl.ds(0, 1)], sem.at[slot]).wait()
        return a

    lax.fori_loop(0, tile, wait, 0, unroll=8)
    m = m_ref[...]
    x2 = _layer_norm(alpha * x_ref[...] + _mod_chunk(m, 5) * fbuf[slot], lng_ref[...], lnb_ref[...])
    x2_ref[...] = x2
    mn = mn_ref[...]
    hn = x2 * (1.0 + _mod_chunk(mn, 1)) + _mod_chunk(mn, 0)
    hn_ref[...] = hn.astype(hn_ref.dtype)
    if dils:
        hbuf = rest[-1]
        n_blk = D_MODEL // LANES
        for j in range(n_blk):
            hbuf[j] = hn[:, j * LANES:(j + 1) * LANES]
        for gref, dil in zip(grouped_refs, dils):
            for r in range(dil):
                rows = jnp.concatenate([hbuf[j, pl.ds(r, tile // dil, stride=dil), :] for j in range(n_blk)], axis=-1)
                gref[r] = rows.astype(gref.dtype)


def _mod_spec(tile, d, per_token):
    if per_token:
        return pl.BlockSpec((None, tile, 6 * d), lambda bi, i, *_: (bi, i, 0))
    return pl.BlockSpec((None, 1, 6 * d), lambda bi, i, *_: (bi, 0, 0))


def _ln2(dest, x, f_sorted, row0, m, m_next, ln_g, ln_b, alpha, per_token, dils=()):
    b, s, d = x.shape
    tile = min(ROW_TILE, s)
    nt = s // tile
    mspec = _mod_spec(tile, d, per_token)
    tok = pl.BlockSpec((None, tile, d), lambda bi, i, dst: (bi, i, 0))
    vec = pl.BlockSpec((1, d), lambda bi, i, dst: (0, 0))
    kern = functools.partial(_ln2_kernel, alpha=alpha, tile=tile, row0=row0, steps_per_seq=nt, n_steps=b * nt,
                             dils=tuple(dils))
    grouped_shapes = tuple(jax.ShapeDtypeStruct((b, dil, s // dil, d), BF16) for dil in dils)
    grouped_specs = tuple(pl.BlockSpec((None, dil, tile // dil, d), lambda bi, i, dst: (bi, 0, i, 0)) for dil in dils)
    scratch = [pltpu.VMEM((2, tile, d), F32), pltpu.SemaphoreType.DMA((2,))]
    if dils:
        scratch.append(pltpu.VMEM((d // LANES, tile, LANES), F32))
    return pl.pallas_call(
        kern,
        out_shape=(jax.ShapeDtypeStruct((b, s, d), F32), jax.ShapeDtypeStruct((b, s, d), BF16)) + grouped_shapes,
        grid_spec=pltpu.PrefetchScalarGridSpec(
            num_scalar_prefetch=1,
            grid=(b, nt),
            in_specs=[tok, pl.BlockSpec(memory_space=pl.ANY), mspec, mspec, vec, vec],
            out_specs=(tok, tok) + grouped_specs,
            scratch_shapes=scratch,
        ),
        compiler_params=_cparams("arbitrary", "arbitrary"),
        name="ln2",
    )(dest, x, f_sorted, m, m_next, ln_g, ln_b)


def _rope_tables(pos):
    half = HEAD_DIM // 2
    inv = ROPE_THETA ** (-jnp.arange(half, dtype=F32) / half)
    lane = np.arange(LANES)
    inv_lanes = inv[lane % half]
    sign = jnp.asarray(np.where(lane % HEAD_DIM < half, -1.0, 1.0), F32)
    ang = pos.astype(F32)[..., None] * inv_lanes
    return jnp.cos(ang), jnp.sin(ang) * sign


def _qkv_kernel(*refs, tile, grouped, tail_chunks, tail_first, tail_rows, tiles_per_res, ids=None):
    h_refs = refs[0:N_DIL]
    w_ref = refs[N_DIL]
    cos_refs = refs[N_DIL + 1:2 * N_DIL + 1]
    sin_refs = refs[2 * N_DIL + 1:3 * N_DIL + 1]
    outs = refs[3 * N_DIL + 1:]
    main_refs = outs[:N_DIL] if grouped else ()
    tail_refs = outs[len(main_refs):]
    c, i = ids if ids is not None else (pl.program_id(0), pl.program_id(2))
    first_half = (_iota((tile, LANES), 1) & (HEAD_DIM - 1)) < (HEAD_DIM // 2)

    def rope(acc, cos, sin):
        blocks = []
        for j in range(D_MODEL // LANES):
            blk = acc[:, j * LANES:(j + 1) * LANES]
            partner = jnp.where(first_half, pltpu.roll(blk, LANES - HEAD_DIM // 2, 1), pltpu.roll(blk, HEAD_DIM // 2, 1))
            blocks.append(blk * cos + partner * sin)
        return jnp.concatenate(blocks, axis=-1)

    for grp in range(N_DIL):
        @pl.when(c == grp)
        def _(grp=grp):
            ii = i & (tiles_per_res[grp] - 1)
            h = h_refs[grp][...]
            cos, sin = cos_refs[grp][...], sin_refs[grp][...]
            for j in range(3):
                acc = jnp.dot(h, w_ref[:, j * D_MODEL:(j + 1) * D_MODEL], preferred_element_type=F32)
                res = rope(acc, cos, sin) if j < 2 else acc
                if grouped:
                    main_refs[grp][j] = ((res * (HEAD_DIM ** -0.5)) if j == 0 else res).astype(BF16)
                for t, tref in enumerate(tail_refs):
                    if tail_chunks[t] == 3 * grp + j:
                        @pl.when(ii >= tail_first[t])
                        def _(tref=tref, t=t, res=res):
                            tref[...] = res[tile - tail_rows[t]:, :]


def _qkv_plan(hs, w, tables, tail_keep, grouped, max_tile=ROW_TILE):
    b, d = hs[0].shape[0], hs[0].shape[-1]
    dils = tuple(h.shape[1] for h in hs)
    subs = tuple(h.shape[2] for h in hs)
    tile = min(max_tile, min(subs))
    nt = dils[0] * subs[0] // tile
    tpr = tuple(sub // tile for sub in subs)
    assert all(t & (t - 1) == 0 and dil * t == nt for t, dil in zip(tpr, dils))
    n_chunks = w.shape[1] // d
    tail_chunks = tuple(sorted(tail_keep))
    tail_rows = tuple(min(tail_keep[c], tile) for c in tail_chunks)
    tail_blocks = tuple(tail_keep[c] // r for c, r in zip(tail_chunks, tail_rows))
    tail_first = tuple(tpr[c // 3] - nb for c, nb in zip(tail_chunks, tail_blocks))

    def parked(grp, own, last):
        def imap(c, bi, i):
            before, after = c < grp, c > grp
            return tuple(jnp.where(before, 0, jnp.where(after, l, o)) for o, l in zip(own(bi, i), last))
        return imap

    in_specs, cos_specs, sin_specs = [], [], []
    out_shape, out_specs = [], []
    for grp in range(N_DIL):
        dil, t = dils[grp], tpr[grp]
        in_specs.append(pl.BlockSpec((None, None, tile, d), parked(
            grp, lambda bi, i, t=t: (bi, i // t, i % t, 0), (b - 1, dil - 1, t - 1, 0))))
        tab = pl.BlockSpec((None, tile, LANES), parked(grp, lambda bi, i, t=t: (i // t, i % t, 0), (dil - 1, t - 1, 0)))
        cos_specs.append(tab)
        sin_specs.append(tab)
        if grouped:
            out_shape.append(jax.ShapeDtypeStruct((3, b, dil, subs[grp], d), BF16))
            out_specs.append(pl.BlockSpec((3, None, None, tile, d), parked(
                grp, lambda bi, i, t=t: (0, bi, i // t, i % t, 0), (0, b - 1, dil - 1, t - 1, 0))))
    for c_t, rows, first, nb in zip(tail_chunks, tail_rows, tail_first, tail_blocks):
        dil, t = dils[c_t // 3], tpr[c_t // 3]
        out_shape.append(jax.ShapeDtypeStruct((b, dil, tail_keep[c_t], d), F32))
        out_specs.append(pl.BlockSpec((None, None, rows, d), parked(
            c_t // 3, lambda bi, i, t=t, first=first: (bi, i // t, jnp.maximum(i % t - first, 0), 0),
            (b - 1, dil - 1, nb - 1, 0))))
    n_groups = n_chunks // 3
    w_spec = pl.BlockSpec((d, 3 * d), lambda c, bi, i: (0, jnp.minimum(c, n_groups - 1)))
    return dict(
        kwargs=dict(tile=tile, grouped=grouped, tail_chunks=tail_chunks, tail_first=tail_first, tail_rows=tail_rows,
                    tiles_per_res=tpr),
        grid=(n_groups, b, nt),
        in_specs=in_specs + [w_spec] + cos_specs + sin_specs,
        out_specs=tuple(out_specs),
        out_shape=tuple(out_shape),
        args=(*hs, w, *[t[0] for t in tables], *[t[1] for t in tables]),
    )


def _qkv_proj(hs, w, tables, tail_keep, grouped, max_tile=ROW_TILE):
    plan = _qkv_plan(hs, w, tables, tail_keep, grouped, max_tile)
    return pl.pallas_call(
        functools.partial(_qkv_kernel, **plan["kwargs"]),
        out_shape=plan["out_shape"],
        grid=plan["grid"],
        in_specs=plan["in_specs"],
        out_specs=plan["out_specs"],
        compiler_params=_cparams("arbitrary", "arbitrary", "arbitrary"),
        name="qkv_proj",
    )(*plan["args"])


def _attn_prompt_kernel(q_ref, kp_ref, kc_ref, vp_ref, vc_ref, o_ref, lse_ref, *, n_sub):
    i = pl.program_id(2)
    tq = ATT_BAND
    lane = _iota((tq, LANES), 1)
    low = lane < HEAD_DIM
    qi = _iota((2 * tq, 2 * tq), 0) & (tq - 1)
    kj = _iota((2 * tq, 2 * tq), 1)
    in_prev = (kj < tq) & (kj >= qi)
    in_cur = (kj >= tq) & ((kj - tq) <= qi)
    zero = jnp.zeros((), q_ref.dtype)
    for sub in range(n_sub):
        rows = slice(sub * tq, (sub + 1) * tq)
        before = slice((sub - 1) * tq, sub * tq)
        mask = ((in_prev & (i > 0)) if sub == 0 else in_prev) | in_cur
        lse_all = jnp.zeros((tq, LANES), F32)
        for hp in range(N_HEADS // 2):
            sl = slice(hp * LANES, (hp + 1) * LANES)
            q = q_ref[rows, sl]
            q2 = jnp.concatenate([jnp.where(low, q, zero), jnp.where(low, zero, q)], axis=0)
            k_prev = kp_ref[:, sl] if sub == 0 else kc_ref[before, sl]
            v_prev = vp_ref[:, sl] if sub == 0 else vc_ref[before, sl]
            kw = jnp.concatenate([k_prev, kc_ref[rows, sl]], axis=0)
            vw = jnp.concatenate([v_prev, vc_ref[rows, sl]], axis=0)
            s = lax.dot_general(q2, kw, _NT, preferred_element_type=F32)
            s = jnp.where(mask, s, -jnp.inf)
            mx = jnp.max(s, axis=-1, keepdims=True)
            p = jnp.exp(s - mx)
            den = jnp.sum(p, axis=-1, keepdims=True)
            o2 = jnp.dot(p.astype(vw.dtype), vw, preferred_element_type=F32) * (1.0 / den)
            o_ref[rows, sl] = jnp.where(low, o2[:tq], o2[tq:]).astype(o_ref.dtype)
            lse2 = mx + jnp.log(den)
            lse_all = jnp.where(lane == 2 * hp, lse2[:tq], jnp.where(lane == 2 * hp + 1, lse2[tq:], lse_all))
        lse_ref[rows, :] = lse_all


def _attn_prompt(qkv_g, grp, n_sub=2):
    _, b, dil, sub, d = qkv_g.shape
    tq = ATT_BAND
    rows = n_sub * tq
    cur = lambda which: pl.BlockSpec((None, None, None, rows, d), lambda bi, r, i: (which, bi, r, i, 0))
    prev = lambda which: pl.BlockSpec((None, None, None, tq, d),
                                      lambda bi, r, i: (which, bi, r, jnp.maximum(i * n_sub - 1, 0), 0))
    return pl.pallas_call(
        functools.partial(_attn_prompt_kernel, n_sub=n_sub),
        out_shape=(jax.ShapeDtypeStruct((b, dil, sub, d), BF16), jax.ShapeDtypeStruct((b, dil, sub, LANES), F32)),
        grid=(b, dil, sub // rows),
        in_specs=[cur(0), prev(1), cur(1), prev(2), cur(2)],
        out_specs=(
            pl.BlockSpec((None, None, rows, d), lambda bi, r, i: (bi, r, i, 0)),
            pl.BlockSpec((None, None, rows, LANES), lambda bi, r, i: (bi, r, i, 0)),
        ),
        compiler_params=_cparams("arbitrary", "arbitrary", "arbitrary"),
        name=f"attn_prompt_g{grp}",
    )(qkv_g, qkv_g, qkv_g, qkv_g, qkv_g)


def _attn_sample_kernel(qkn_ref, k0_ref, v0_ref, k1_ref, v1_ref, k2_ref, v2_ref, o_ref, *, n_new):
    rows = k0_ref.shape[0]
    hpb = rows // HEAD_DIM
    nq = n_new * hpb
    shift = hpb.bit_length() - 1
    scale = HEAD_DIM ** -0.5
    neg = F32(-jnp.inf)
    own = (_iota((nq, rows), 1) >> 6) == (_iota((nq, rows), 0) & (hpb - 1))
    n_pad = 2 * SUBLANES
    pad = jnp.zeros((n_pad - n_new, rows), F32)
    scores, values = [], []
    for grp, ((_, dil), k_ref, v_ref) in enumerate(zip(DIL_GROUPS, (k0_ref, k1_ref, k2_ref), (v0_ref, v1_ref, v2_ref))):
        q = qkn_ref[0, grp * n_new:(grp + 1) * n_new, :]
        q_rows = jnp.concatenate([jnp.broadcast_to(q[t:t + 1, :], (hpb, rows)) for t in range(n_new)], axis=0)
        qbd = jnp.where(own, q_rows, 0.0).astype(BF16)
        n_buf = k_ref.shape[1]
        s = jnp.dot(qbd, k_ref[...].astype(BF16), preferred_element_type=F32) * scale
        pos = _iota((nq, n_buf), 1)
        t_row = _iota((nq, n_buf), 0) >> shift
        ok = (pos >= t_row) if dil == 1 else ((pos & (dil - 1)) == t_row)
        scores.append(jnp.where(ok, s, neg))
        values.append((v_ref[...].astype(BF16), _NT))
        k_new = jnp.concatenate([qkn_ref[1, grp * n_new:(grp + 1) * n_new, :], pad], axis=0).astype(BF16)
        v_new = jnp.concatenate([qkn_ref[2, grp * n_new:(grp + 1) * n_new, :], pad], axis=0).astype(BF16)
        s_new = lax.dot_general(qbd, k_new, _NT, preferred_element_type=F32) * scale
        t2 = _iota((nq, n_pad), 1)
        tq = _iota((nq, n_pad), 0) >> shift
        ok_new = (t2 <= tq) & (((tq - t2) & (dil - 1)) == 0)
        scores.append(jnp.where(ok_new, s_new, neg))
        values.append((v_new, (((1,), (0,)), ((), ()))))
    mx = functools.reduce(jnp.maximum, [jnp.max(s, axis=-1, keepdims=True) for s in scores])
    den = jnp.zeros((nq, 1), F32)
    acc = jnp.zeros((nq, rows), F32)
    for s, (v, dims) in zip(scores, values):
        p = jnp.exp(s - mx)
        den = den + jnp.sum(p, axis=-1, keepdims=True)
        acc = acc + lax.dot_general(p.astype(BF16), v, dims, preferred_element_type=F32)
    acc = jnp.where(own, acc * (1.0 / den), 0.0)
    out = jnp.zeros((SUBLANES, rows), F32)
    out_r = _iota((SUBLANES, rows), 0)
    for t in range(n_new):
        out = jnp.where(out_r == t, jnp.sum(acc[t * hpb:(t + 1) * hpb, :], axis=0, keepdims=True), out)
    o_ref[...] = out[0:n_new, :]


def _attn_sample(qkn, caches_t, n_new, rows=512):
    b, d = qkn.shape[0], qkn.shape[-1]
    assert n_new <= SUBLANES and (rows // HEAD_DIM) & (rows // HEAD_DIM - 1) == 0
    cache_args, cache_specs = [], []
    for (win, dil), (ck, cv) in zip(DIL_GROUPS, caches_t):
        assert ck.shape == (b, d, win) and win // dil == ATT_BAND and (dil == 1 or dil >= n_new)
        for c in (ck, cv):
            cache_args.append(c)
            cache_specs.append(pl.BlockSpec((None, rows, win), lambda bi, hh: (bi, hh, 0)))
    return pl.pallas_call(
        functools.partial(_attn_sample_kernel, n_new=n_new),
        out_shape=jax.ShapeDtypeStruct((b, n_new, d), F32),
        grid=(b, d // rows),
        in_specs=[pl.BlockSpec((None, 3, qkn.shape[2], rows), lambda bi, hh: (bi, 0, 0, hh))] + cache_specs,
        out_specs=pl.BlockSpec((None, n_new, rows), lambda bi, hh: (bi, 0, hh)),
        compiler_params=_cparams("arbitrary", "arbitrary"),
        name="attn_sample",
    )(qkn, *cache_args)


def _fused_attn_qkv_kernel(*refs, n_attn_in, n_qkv_in, n_new, steps_inner, per_group, tiles_per_batch, qkv_kwargs):
    attn_in = refs[:n_attn_in]
    qkv_in = refs[n_attn_in:n_attn_in + n_qkv_in]
    o_ref = refs[n_attn_in + n_qkv_in]
    qkv_out = refs[n_attn_in + n_qkv_in + 1:]
    _attn_sample_kernel(*attn_in, o_ref, n_new=n_new)
    step = pl.program_id(0) * steps_inner + pl.program_id(1)
    grp = step // per_group
    tile_i = (step - grp * per_group) & (tiles_per_batch - 1)
    _qkv_kernel(*qkv_in, *qkv_out, ids=(grp, tile_i), **qkv_kwargs)


def _attn_sample_with_qkv(qkn, caches_t, n_new, plan, rows=512):
    b, d = qkn.shape[0], qkn.shape[-1]
    steps_inner = d // rows
    n_groups, b_qkv, nt = plan["grid"]
    per_group = b_qkv * nt
    assert n_groups * per_group <= b * steps_inner and nt & (nt - 1) == 0
    assert n_new <= SUBLANES and (rows // HEAD_DIM) & (rows // HEAD_DIM - 1) == 0

    def ids(bi, hh):
        step = bi * steps_inner + hh
        grp = step // per_group
        t = step - grp * per_group
        return grp, t // nt, t % nt

    remap = lambda spec: pl.BlockSpec(spec.block_shape, lambda bi, hh, m=spec.index_map: m(*ids(bi, hh)))
    cache_args, cache_specs = [], []
    for (win, dil), (ck, cv) in zip(DIL_GROUPS, caches_t):
        assert ck.shape == (b, d, win) and win // dil == ATT_BAND and (dil == 1 or dil >= n_new)
        for c in (ck, cv):
            cache_args.append(c)
            cache_specs.append(pl.BlockSpec((None, rows, win), lambda bi, hh: (bi, hh, 0)))
    attn_specs = [pl.BlockSpec((None, 3, qkn.shape[2], rows), lambda bi, hh: (bi, 0, 0, hh))] + cache_specs
    kern = functools.partial(
        _fused_attn_qkv_kernel, n_attn_in=len(attn_specs), n_qkv_in=len(plan["in_specs"]), n_new=n_new,
        steps_inner=steps_inner, per_group=per_group, tiles_per_batch=nt, qkv_kwargs=plan["kwargs"])
    outs = pl.pallas_call(
        kern,
        out_shape=(jax.ShapeDtypeStruct((b, n_new, d), F32),) + plan["out_shape"],
        grid=(b, steps_inner),
        in_specs=attn_specs + [remap(sp) for sp in plan["in_specs"]],
        out_specs=(pl.BlockSpec((None, n_new, rows), lambda bi, hh: (bi, 0, hh)),) + tuple(remap(sp) for sp in plan["out_specs"]),
        compiler_params=_cparams("arbitrary", "arbitrary"),
        name="attn_sample_qkv",
    )(qkn, *cache_args, *plan["args"])
    return outs[0], outs[1:]


def _oproj_kernel(*refs, dils, n_lse, alpha, tile):
    n_groups = len(dils)
    o_refs = refs[:n_groups]
    lse_refs = refs[n_groups:n_groups + n_lse]
    k = n_groups + n_lse
    x_ref, m_ref, wo_ref, hx_ref, lng_ref, lnb_ref, wr_ref, br_ref = refs[k:k + 8]
    x1_ref, hrow_ref, ext_ref = refs[k + 8:k + 11]
    obuf, lbuf = refs[k + 11:]

    def natural(ref, buf, g, dil):
        if dil == 1:
            return ref[0].astype(F32)
        n_blk = ref.shape[-1] // LANES
        for r in range(dil):
            rows = ref[r].astype(F32)
            for j in range(n_blk):
                buf[g * n_blk + j, pl.ds(r, tile // dil, stride=dil), :] = rows[:, j * LANES:(j + 1) * LANES]
        return jnp.concatenate([buf[g * n_blk + j] for j in range(n_blk)], axis=-1)

    os_ = [natural(o_refs[g], obuf, g, dils[g]) for g in range(n_groups)]
    if n_lse == 0:
        o = os_[0].astype(BF16)
    else:
        lses = [natural(lse_refs[g], lbuf, g, dils[g]) for g in range(n_groups)]
        mx = functools.reduce(jnp.maximum, lses)
        es = [jnp.exp(l - mx) for l in lses]
        inv = 1.0 / functools.reduce(lambda a, b: a + b, es)
        o = jnp.zeros(x_ref.shape, F32)
        hx = hx_ref[...]
        for e, og in zip(es, os_):
            wgt = e * inv
            hi = wgt.astype(BF16)
            lo = (wgt - hi.astype(F32)).astype(BF16)
            wexp = jnp.dot(hi, hx, preferred_element_type=F32) + jnp.dot(lo, hx, preferred_element_type=F32)
            o = o + wexp * og
        o = o.astype(BF16)
    y = jnp.dot(o, wo_ref[...], preferred_element_type=F32)
    x1, h2, ext = _post_mixer(x_ref[...], y, m_ref[...], lng_ref, lnb_ref, wr_ref, br_ref, alpha)
    x1_ref[...] = x1
    hrow_ref[:, 0:D_MODEL] = h2
    hrow_ref[:, D_MODEL:] = ext
    ext_ref[...] = ext


def _out_proj(os_, lses, x, m, w_o, head_expand, ln_g, ln_b, wr, br, alpha, per_token):
    b, s, d = x.shape
    tile = min(ROW_TILE, s)
    nt = s // tile
    dils = tuple(o.shape[1] for o in os_)
    tok = pl.BlockSpec((None, tile, d), lambda bi, i: (bi, i, 0))
    grouped = lambda dil, w: pl.BlockSpec((None, dil, tile // dil, w), lambda bi, i: (bi, 0, i, 0))
    full = lambda bi, i: (0, 0)
    return pl.pallas_call(
        functools.partial(_oproj_kernel, dils=dils, n_lse=len(lses), alpha=alpha, tile=tile),
        out_shape=(
            jax.ShapeDtypeStruct((b, s, d), F32),
            jax.ShapeDtypeStruct((b * s, ROW_W), F32),
            jax.ShapeDtypeStruct((b * s, EXT), F32),
        ),
        grid=(b, nt),
        in_specs=[grouped(dil, d) for dil in dils] + [grouped(dil, LANES) for dil in dils[:len(lses)]] + [
            tok, _mod_spec(tile, d, per_token),
            pl.BlockSpec((d, d), full),
            pl.BlockSpec((LANES, d), full),
            pl.BlockSpec((1, d), full),
            pl.BlockSpec((1, d), full),
            pl.BlockSpec((2, d, EXT), lambda *_: (0, 0, 0)),
            pl.BlockSpec((1, EXT), full),
        ],
        out_specs=(
            tok,
            pl.BlockSpec((tile, ROW_W), lambda bi, i: (bi * nt + i, 0)),
            pl.BlockSpec((tile, EXT), lambda bi, i: (bi * nt + i, 0)),
        ),
        scratch_shapes=[pltpu.VMEM((len(dils) * d // LANES, tile, LANES), F32), pltpu.VMEM((len(dils), tile, LANES), F32)],
        compiler_params=_cparams("arbitrary", "arbitrary"),
        name="out_proj",
    )(*os_, *lses, x, m, w_o, head_expand, ln_g, ln_b, wr, br)


def _router_matrix(w_group, b_group, w_er, b_er):
    d = w_group.shape[0]
    n_e = N_EXPERT_GROUPS * EXPERTS_PER_GROUP
    w_e = jnp.transpose(w_er, (1, 0, 2)).reshape(d, n_e)
    pad = EXT - N_EXPERT_GROUPS - n_e
    wr = jnp.concatenate([w_group, w_e, jnp.zeros((d, pad), F32)], axis=1)
    br = jnp.concatenate([b_group, b_er.reshape(n_e), jnp.zeros((pad,), F32)])[None, :]
    wr_hi = wr.astype(BF16)
    wr_lo = (wr - wr_hi.astype(F32)).astype(BF16)
    return jnp.stack([wr_hi, wr_lo]), br


def _moe_experts(rows_p, rows_s, ext_p, ext_s, w1, w3, w2, layer):
    dest, tile_group, n_used, n_tiles = _group_sort(ext_p, ext_s)
    sorted_rows = _scatter_rows(dest, rows_p, rows_s, n_tiles * MOE_TILE)
    return dest, _moe_ffn(tile_group, n_used, sorted_rows, w1, w3, w2, n_tiles, layer)


def kernel(x_prompt, x_sample, state_pool, cache_k_w128, cache_v_w128, cache_k_w512, cache_v_w512, cache_k_w2048, cache_v_w2048, c_prompt, c_sample, mod_w, mod_b, ln1_g, ln1_b, ln2_g, ln2_b, pool_w, pool_scale, attn_w_qkv, attn_w_o, moe_w_group, moe_b_group, moe_w_expert_router, moe_b_expert_router, moe_w1, moe_w3, moe_w2):
    depth = mod_w.shape[0]
    alpha = float((2.0 * depth) ** 0.25)
    bp, s, d = x_prompt.shape
    bs, t_new, _ = x_sample.shape
    n_p, n_s = bp * s, bs * t_new
    kv_caches = ((cache_k_w128, cache_v_w128), (cache_k_w512, cache_v_w512), (cache_k_w2048, cache_v_w2048))

    c_all = jnp.concatenate([c_sample, c_prompt, jnp.zeros((SUBLANES - bp % SUBLANES, d), F32)], axis=0)
    m_all = _modulation(c_all, mod_w, mod_b)
    m_dec = [m_all[i, :bs] for i in range(depth)]
    m_tok = [jnp.repeat(m, t_new, axis=0).reshape(1, n_s, 6 * d) for m in m_dec]
    m_seq = [m_all[i, bs:bs + bp].reshape(bp, 1, 6 * d) for i in range(depth)]

    head_expand = jnp.asarray(np.arange(LANES)[:, None] == (np.arange(d)[None, :] // HEAD_DIM), BF16)
    dils = tuple(dil for _, dil in DIL_GROUPS)
    tables_p = [_rope_tables(jnp.arange(s // dil, dtype=jnp.int32)[None, :] * dil
                             + jnp.arange(dil, dtype=jnp.int32)[:, None]) for dil in dils]
    tables_s = [_rope_tables(PAST_LEN + (jnp.arange(n_s, dtype=jnp.int32) % t_new)[None, :])] * N_DIL

    w1_bf, w3_bf, w2_bf = moe_w1.astype(BF16), moe_w3.astype(BF16), moe_w2.astype(BF16)
    xp, xs = x_prompt, x_sample.reshape(1, n_s, d)
    hp_groups = hs = None
    pool_p, pool_s = [], []
    kv_p = [[] for _ in range(2 * N_DIL)]
    kv_s = [[] for _ in range(2 * N_DIL)]
    for i in range(depth):
        li = i // 2
        wr, br = _router_matrix(moe_w_group[i], moe_b_group[i], moe_w_expert_router[i], moe_b_expert_router[i])
        ln1g, ln1b = ln1_g[i][None, :], ln1_b[i][None, :]
        if i % 2 == 0:
            pw = pool_w[li].astype(BF16)
            ps = pool_scale[li][None, :]
            xp, rows_p, ext_p, tail = _pool_prompt(xp, m_seq[i], pw, ps, ln1g, ln1b, wr, br, alpha)
            x1s, rows_s, ext_s, znew = _pool_sample(
                state_pool[li].reshape(bs, POOL_BUF * d), xs.reshape(bs, t_new * d),
                m_dec[i], pw, ps, ln1g, ln1b, wr, br, alpha)
            xs = x1s.reshape(1, n_s, d)
            rows_s = rows_s.reshape(n_s, ROW_W)
            ext_s = ext_s.reshape(n_s, EXT)
            pool_p.append(tail)
            pool_s.append(znew.reshape(bs, t_new, d))
        else:
            wqkv = attn_w_qkv[li].astype(BF16)
            wo = attn_w_o[li].astype(BF16)
            kv_chunks = [3 * g + j for g in range(N_DIL) for j in (1, 2)]
            keep_p = {3 * g + j: min(win, s) // dil for g, (win, dil) in enumerate(DIL_GROUPS) for j in (1, 2)}
            rows_all = _qkv_proj([hs[:, None]] * N_DIL, wqkv, tables_s, {c: n_s for c in range(3 * N_DIL)},
                                 grouped=False, max_tile=ROW_TILE // 2)
            kinds = [jnp.concatenate([rows_all[3 * g + j].reshape(bs, t_new, d) for g in range(N_DIL)], axis=1)
                     for j in range(3)]
            qkn = jnp.pad(jnp.stack(kinds, axis=1), ((0, 0), (0, 0), (0, 2 * SUBLANES - N_DIL * t_new), (0, 0)))
            caches_t = [tuple(jnp.transpose(c[li], (0, 2, 3, 1)).reshape(bs, d, c.shape[2]) for c in kv)
                        for kv in kv_caches]
            plan = _qkv_plan(hp_groups, wqkv, tables_p, keep_p, grouped=True, max_tile=ROW_TILE // 2)
            o_s, qkv_outs = _attn_sample_with_qkv(qkn, caches_t, t_new, plan)
            qkv_groups, tails_p = qkv_outs[:N_DIL], qkv_outs[N_DIL:]
            for j, c in enumerate(kv_chunks):
                tail = jnp.transpose(tails_p[j], (0, 2, 1, 3))
                kv_p[j].append(tail.reshape(bp, -1, N_HEADS, HEAD_DIM))
                kv_s[j].append(rows_all[c].reshape(bs, t_new, N_HEADS, HEAD_DIM))
            outs = [_attn_prompt(qkv_groups[g], g) for g in range(N_DIL)]
            xp, rows_p, ext_p = _out_proj([o for o, _ in outs], [l for _, l in outs], xp, m_seq[i], wo, head_expand,
                                          ln1g, ln1b, wr, br, alpha, per_token=False)
            xs, rows_s, ext_s = _out_proj([o_s.reshape(1, 1, n_s, d)], [], xs, m_tok[i], wo, head_expand,
                                          ln1g, ln1b, wr, br, alpha, per_token=True)
        dest, f_sorted = _moe_experts(rows_p, rows_s, ext_p, ext_s, w1_bf, w3_bf, w2_bf, i)
        nxt = min(i + 1, depth - 1)
        ln2g, ln2b = ln2_g[i][None, :], ln2_b[i][None, :]
        regroup = tuple(dil for dil in dils if dil > 1) if (i + 1 < depth and (i + 1) % 2 == 1) else ()
        xp, hp, *hp_dilated = _ln2(dest, xp, f_sorted, 0, m_seq[i], m_seq[nxt], ln2g, ln2b, alpha, per_token=False,
                                   dils=regroup)
        if regroup:
            hp_dilated = iter(hp_dilated)
            hp_groups = [hp[:, None] if dil == 1 else next(hp_dilated) for dil in dils]
        xs, hs = _ln2(dest, xs, f_sorted, n_p, m_tok[i], m_tok[nxt], ln2g, ln2b, alpha, per_token=True)

    stack = lambda lst: jnp.stack(lst, axis=0)
    return (xp, xs.reshape(bs, t_new, d), stack(pool_p), stack(pool_s),
            *[stack(kv_p[j]) for j in range(2 * N_DIL)], *[stack(kv_s[j]) for j in range(2 * N_DIL)])
```

```python
import functools

import numpy as np
import jax
import jax.numpy as jnp
from jax import lax
from jax.experimental import pallas as pl
from jax.experimental.pallas import tpu as pltpu

F32 = jnp.float32
BF16 = jnp.bfloat16
HIGHEST = lax.Precision.HIGHEST

D_MODEL = 1024
POOL_WINDOWS = (2, 4, 8, 16)
N_POOL_GROUPS = len(POOL_WINDOWS)
POOL_GROUP = D_MODEL // N_POOL_GROUPS
POOL_BUF = max(POOL_WINDOWS) - 1
POOL_HALO = 16
HEAD_DIM = 64
N_HEADS = D_MODEL // HEAD_DIM
DIL_GROUPS = ((128, 1), (512, 4), (2048, 16))
N_DIL = len(DIL_GROUPS)
ATT_BAND = 128
ROPE_THETA = 10000.0
N_EXPERT_GROUPS = 4
EXPERTS_PER_GROUP = 8
PAST_LEN = 2048
LN_EPS = 1e-5

LANES = 128
SUBLANES = 8
VMEM_LIMIT_BYTES = 56 * 1024 * 1024

EXT = LANES
EXT_EXPERT0 = N_EXPERT_GROUPS
EXT_ONEHOT0 = 120
ROW_W = D_MODEL + EXT

MOE_TILE = 512
SORT_CHUNK = 512
ROW_TILE = 512
N_DMA_PRIORITIES = 2

_NT = (((1,), (1,)), ((), ()))


def _cparams(*sem):
    return pltpu.CompilerParams(dimension_semantics=sem, vmem_limit_bytes=VMEM_LIMIT_BYTES)


def _iota(shape, dim):
    return lax.broadcasted_iota(jnp.int32, shape, dim)


def _mod_kernel(c_ref, w_ref, b_ref, o_ref):
    o_ref[...] = jnp.dot(c_ref[...], w_ref[...], precision=HIGHEST, preferred_element_type=F32) + b_ref[...]


def _modulation(c_all, mod_w, mod_b):
    n_layers, d, n_out = mod_w.shape
    rows = c_all.shape[0]
    tn = n_out // 4
    return pl.pallas_call(
        _mod_kernel,
        out_shape=jax.ShapeDtypeStruct((n_layers, rows, n_out), F32),
        grid=(n_layers, n_out // tn),
        in_specs=[
            pl.BlockSpec((rows, d), lambda l, j: (0, 0)),
            pl.BlockSpec((None, d, tn), lambda l, j: (l, 0, j)),
            pl.BlockSpec((None, 1, tn), lambda l, j: (l, 0, j)),
        ],
        out_specs=pl.BlockSpec((None, rows, tn), lambda l, j: (l, 0, j)),
        compiler_params=_cparams("arbitrary", "arbitrary"),
        name="modulation",
    )(c_all, mod_w, mod_b.reshape(n_layers, 1, n_out))


def _mod_chunk(m, j):
    return m[:, j * D_MODEL:(j + 1) * D_MODEL]


def _layer_norm(u, g, b):
    mu = jnp.mean(u, axis=-1, keepdims=True)
    uc = u - mu
    var = jnp.mean(uc * uc, axis=-1, keepdims=True)
    return uc * lax.rsqrt(var + LN_EPS) * g + b


def _route(h2, wr_ref, br_ref):
    h_hi = h2.astype(BF16)
    h_lo = (h2 - h_hi.astype(F32)).astype(BF16)
    dot = lambda a, b: jnp.dot(a, b, preferred_element_type=F32)
    logits = dot(h_hi, wr_ref[0]) + dot(h_lo, wr_ref[0]) + dot(h_hi, wr_ref[1]) + br_ref[...]
    lane_f = _iota(logits.shape, 1).astype(F32)
    neg = F32(-jnp.inf)
    big = F32(EXT)
    gl = jnp.where(lane_f < N_EXPERT_GROUPS, logits, neg)
    gmax = jnp.max(gl, axis=-1, keepdims=True)
    gidx = jnp.min(jnp.where(gl == gmax, lane_f, big), axis=-1, keepdims=True)
    g_p = 1.0 / jnp.sum(jnp.exp(gl - gmax), axis=-1, keepdims=True)
    e_lo = EXT_EXPERT0 + EXPERTS_PER_GROUP * gidx
    el = jnp.where((lane_f >= e_lo) & (lane_f < e_lo + EXPERTS_PER_GROUP), logits, neg)
    v1 = jnp.max(el, axis=-1, keepdims=True)
    i1 = jnp.min(jnp.where(el == v1, lane_f, big), axis=-1, keepdims=True)
    el2 = jnp.where(lane_f == i1, neg, el)
    v2 = jnp.max(el2, axis=-1, keepdims=True)
    i2 = jnp.min(jnp.where(el2 == v2, lane_f, big), axis=-1, keepdims=True)
    r = jnp.exp(v2 - v1)
    w1 = g_p / (1.0 + r)
    w2 = w1 * r
    ext = jnp.where(lane_f == i1, w1, 0.0) + jnp.where(lane_f == i2, w2, 0.0)
    return ext + jnp.where(lane_f == gidx + EXT_ONEHOT0, 1.0, 0.0)


def _post_mixer(x, y, m, lng_ref, lnb_ref, wr_ref, br_ref, alpha):
    x1 = _layer_norm(alpha * x + _mod_chunk(m, 2) * y, lng_ref[...], lnb_ref[...])
    h2 = x1 * (1.0 + _mod_chunk(m, 4)) + _mod_chunk(m, 3)
    return x1, h2, _route(h2, wr_ref, br_ref)


def _pool_prompt_kernel(x_ref, xprev_ref, m_ref, pw_ref, ps_ref, lng_ref, lnb_ref, wr_ref, br_ref,
                        x1_ref, hrow_ref, ext_ref, tail_ref, zbuf, *, tile, alpha):
    i = pl.program_id(1)
    m = m_ref[...]
    shift, scale = _mod_chunk(m, 0), _mod_chunk(m, 1)
    x = x_ref[...]
    z = x * (1.0 + scale) + shift
    zprev = xprev_ref[...] * (1.0 + scale) + shift
    zbuf[0:POOL_HALO, :] = jnp.where(i > 0, zprev, 0.0)
    zbuf[POOL_HALO:, :] = z
    pos = i * tile + _iota((tile, 1), 0)
    ys = []
    for g, w in enumerate(POOL_WINDOWS):
        cols = slice(g * POOL_GROUP, (g + 1) * POOL_GROUP)
        win = zbuf[pl.ds(POOL_HALO, tile), cols]
        for j in range(1, w):
            win = win + zbuf[pl.ds(POOL_HALO - j, tile), cols]
        cnt = jnp.minimum(pos + 1, w).astype(F32)
        d = win / cnt - z[:, cols]
        ys.append(jnp.dot(d.astype(BF16), pw_ref[g], preferred_element_type=F32))
    y = jnp.concatenate(ys, axis=-1) * ps_ref[...]
    x1, h2, ext = _post_mixer(x, y, m, lng_ref, lnb_ref, wr_ref, br_ref, alpha)
    x1_ref[...] = x1
    hrow_ref[:, 0:D_MODEL] = h2
    hrow_ref[:, D_MODEL:] = ext
    ext_ref[...] = ext

    @pl.when(i == pl.num_programs(1) - 1)
    def _():
        tail_ref[...] = zbuf[pl.ds(POOL_HALO + tile - POOL_BUF, POOL_BUF), :]


def _pool_prompt(x, m, pool_w, pool_scale, ln_g, ln_b, wr, br, alpha, tile=ROW_TILE):
    b, s, d = x.shape
    nt = s // tile
    halo_blocks = tile // POOL_HALO
    full = lambda bi, i: (0, 0)
    kern = functools.partial(_pool_prompt_kernel, tile=tile, alpha=alpha)
    return pl.pallas_call(
        kern,
        out_shape=(
            jax.ShapeDtypeStruct((b, s, d), F32),
            jax.ShapeDtypeStruct((b * s, ROW_W), F32),
            jax.ShapeDtypeStruct((b * s, EXT), F32),
            jax.ShapeDtypeStruct((b, POOL_BUF, d), F32),
        ),
        grid=(b, nt),
        in_specs=[
            pl.BlockSpec((None, tile, d), lambda bi, i: (bi, i, 0)),
            pl.BlockSpec((None, POOL_HALO, d), lambda bi, i: (bi, jnp.maximum(i * halo_blocks - 1, 0), 0)),
            pl.BlockSpec((None, 1, 6 * d), lambda bi, i: (bi, 0, 0)),
            pl.BlockSpec((N_POOL_GROUPS, POOL_GROUP, POOL_GROUP), lambda bi, i: (0, 0, 0)),
            pl.BlockSpec((1, d), full),
            pl.BlockSpec((1, d), full),
            pl.BlockSpec((1, d), full),
            pl.BlockSpec((2, d, EXT), lambda *_: (0, 0, 0)),
            pl.BlockSpec((1, EXT), full),
        ],
        out_specs=(
            pl.BlockSpec((None, tile, d), lambda bi, i: (bi, i, 0)),
            pl.BlockSpec((tile, ROW_W), lambda bi, i: (bi * nt + i, 0)),
            pl.BlockSpec((tile, EXT), lambda bi, i: (bi * nt + i, 0)),
            pl.BlockSpec((None, POOL_BUF, d), lambda bi, i: (bi, 0, 0)),
        ),
        scratch_shapes=[pltpu.VMEM((tile + POOL_HALO, d), F32)],
        compiler_params=_cparams("arbitrary", "arbitrary"),
        name="pool_prompt",
    )(x, x, m, pool_w, pool_scale, ln_g, ln_b, wr, br)


def _pool_sample_kernel(st_ref, x_ref, m_ref, pw_ref, ps_ref, lng_ref, lnb_ref, wr_ref, br_ref,
                        x1_ref, hrow_ref, ext_ref, znew_ref, *, n_new, alpha):
    d = D_MODEL
    rows = [st_ref[:, r * d:(r + 1) * d] for r in range(POOL_BUF)]
    xs = []
    m = m_ref[...]
    for t in range(n_new):
        x = x_ref[:, t * d:(t + 1) * d]
        z = x * (1.0 + _mod_chunk(m, 1)) + _mod_chunk(m, 0)
        znew_ref[:, t * d:(t + 1) * d] = z
        rows.append(z)
        xs.append(x)
    for t in range(n_new):
        last = POOL_BUF + t
        ys = []
        for g, w in enumerate(POOL_WINDOWS):
            cols = slice(g * POOL_GROUP, (g + 1) * POOL_GROUP)
            win = rows[last][:, cols]
            for j in range(1, w):
                win = win + rows[last - j][:, cols]
            dgrp = win / F32(w) - rows[last][:, cols]
            ys.append(jnp.dot(dgrp.astype(BF16), pw_ref[g], preferred_element_type=F32))
        y = jnp.concatenate(ys, axis=-1) * ps_ref[...]
        x1, h2, ext = _post_mixer(xs[t], y, m, lng_ref, lnb_ref, wr_ref, br_ref, alpha)
        x1_ref[:, t * d:(t + 1) * d] = x1
        hrow_ref[:, t * ROW_W:t * ROW_W + d] = h2
        hrow_ref[:, t * ROW_W + d:(t + 1) * ROW_W] = ext
        ext_ref[:, t * EXT:(t + 1) * EXT] = ext


def _pool_sample(state, x, m, pool_w, pool_scale, ln_g, ln_b, wr, br, alpha, bb=32):
    b = x.shape[0]
    d = D_MODEL
    n_new = x.shape[1] // d
    full = lambda i: (0, 0)
    kern = functools.partial(_pool_sample_kernel, n_new=n_new, alpha=alpha)
    return pl.pallas_call(
        kern,
        out_shape=(
            jax.ShapeDtypeStruct((b, n_new * d), F32),
            jax.ShapeDtypeStruct((b, n_new * ROW_W), F32),
            jax.ShapeDtypeStruct((b, n_new * EXT), F32),
            jax.ShapeDtypeStruct((b, n_new * d), F32),
        ),
        grid=(b // bb,),
        in_specs=[
            pl.BlockSpec((bb, POOL_BUF * d), lambda i: (i, 0)),
            pl.BlockSpec((bb, n_new * d), lambda i: (i, 0)),
            pl.BlockSpec((bb, 6 * d), lambda i: (i, 0)),
            pl.BlockSpec((N_POOL_GROUPS, POOL_GROUP, POOL_GROUP), lambda i: (0, 0, 0)),
            pl.BlockSpec((1, d), full),
            pl.BlockSpec((1, d), full),
            pl.BlockSpec((1, d), full),
            pl.BlockSpec((2, d, EXT), lambda *_: (0, 0, 0)),
            pl.BlockSpec((1, EXT), full),
        ],
        out_specs=(
            pl.BlockSpec((bb, n_new * d), lambda i: (i, 0)),
            pl.BlockSpec((bb, n_new * ROW_W), lambda i: (i, 0)),
            pl.BlockSpec((bb, n_new * EXT), lambda i: (i, 0)),
            pl.BlockSpec((bb, n_new * d), lambda i: (i, 0)),
        ),
        compiler_params=_cparams("arbitrary"),
        name="pool_sample",
    )(state, x, m, pool_w, pool_scale, ln_g, ln_b, wr, br)


def _sort_kernel(extp_ref, exts_ref, dest_ref, meta_ref, oh_ref, *, n_chunks_p, n_chunks_s):
    ch = SORT_CHUNK
    r_io = _iota((SUBLANES, EXT), 0)
    l_io = _iota((SUBLANES, EXT), 1)
    sel = jnp.where((l_io == r_io + EXT_ONEHOT0) & (r_io < N_EXPERT_GROUPS), 1.0, 0.0).astype(BF16)

    def count_from(ext_ref, chunk0):
        def body(c, cnt):
            ext = ext_ref[pl.ds(pl.multiple_of(c * ch, ch), ch), :]
            oh = lax.dot_general(sel, ext.astype(BF16), _NT, preferred_element_type=F32)
            oh_ref[chunk0 + c] = oh
            return cnt + jnp.sum(oh, axis=-1, keepdims=True)
        return body

    counts = lax.fori_loop(0, n_chunks_p, count_from(extp_ref, 0), jnp.zeros((SUBLANES, 1), F32))
    counts = lax.fori_loop(0, n_chunks_s, count_from(exts_ref, n_chunks_p), counts)
    padded = jnp.floor((counts + (MOE_TILE - 1)) * (1.0 / MOE_TILE)) * MOE_TILE
    row = _iota((SUBLANES, 1), 0)
    starts = jnp.zeros((SUBLANES, 1), F32)
    for g in range(1, N_EXPERT_GROUPS):
        starts = starts + jnp.where(row >= g, padded[g - 1:g, :], 0.0)
    tri = jnp.where(_iota((ch, ch), 0) < _iota((ch, ch), 1), 1.0, 0.0).astype(BF16)

    def dest_body(c, base):
        oh = oh_ref[c]
        pre = jnp.dot(oh.astype(BF16), tri, preferred_element_type=F32)
        dest = jnp.sum(oh * (base + pre), axis=0, keepdims=True)
        dest_ref[pl.ds(c, 1), :] = dest.astype(jnp.int32)
        return base + jnp.sum(oh, axis=-1, keepdims=True)

    lax.fori_loop(0, n_chunks_p + n_chunks_s, dest_body, starts)
    ends = starts + padded
    mrow = _iota((SUBLANES, EXT), 0)
    tile_lo = (_iota((SUBLANES, EXT), 1) * MOE_TILE).astype(F32)
    tgroup = jnp.sum(jnp.where((mrow < N_EXPERT_GROUPS - 1) & (tile_lo >= ends), 1.0, 0.0), axis=0, keepdims=True)
    n_used = jnp.sum(jnp.where(row < N_EXPERT_GROUPS, padded, 0.0), axis=0, keepdims=True) * (1.0 / MOE_TILE)
    meta = jnp.where(mrow == 0, tgroup, jnp.where(mrow == 1, n_used, 0.0))
    meta_ref[...] = meta.astype(jnp.int32)


def _group_sort(ext_p, ext_s):
    n_p, n_s = ext_p.shape[0], ext_s.shape[0]
    ncp, ncs = n_p // SORT_CHUNK, n_s // SORT_CHUNK
    n_tiles = (n_p + n_s) // MOE_TILE + N_EXPERT_GROUPS
    assert n_tiles <= EXT
    dest, meta = pl.pallas_call(
        functools.partial(_sort_kernel, n_chunks_p=ncp, n_chunks_s=ncs),
        out_shape=(
            jax.ShapeDtypeStruct((ncp + ncs, SORT_CHUNK), jnp.int32),
            jax.ShapeDtypeStruct((SUBLANES, EXT), jnp.int32),
        ),
        grid=(1,),
        in_specs=[pl.BlockSpec((n_p, EXT), lambda i: (0, 0)), pl.BlockSpec((n_s, EXT), lambda i: (0, 0))],
        out_specs=(
            pl.BlockSpec((ncp + ncs, SORT_CHUNK), lambda i: (0, 0)),
            pl.BlockSpec((SUBLANES, EXT), lambda i: (0, 0)),
        ),
        scratch_shapes=[pltpu.VMEM((ncp + ncs, SUBLANES, SORT_CHUNK), F32)],
        compiler_params=_cparams("arbitrary"),
        name="group_sort",
    )(ext_p, ext_s)
    return dest.reshape(-1), meta[0, :n_tiles], meta[1, :1], n_tiles


def _scatter_kernel(dest_ref, rp_ref, rs_ref, init_ref, out_ref, sem, *, tiles_p):
    del init_ref
    i = pl.program_id(0)
    base = i * ROW_TILE

    def run(src_ref):
        def issue(k2, a):
            for prio in range(N_DMA_PRIORITIES):
                k = N_DMA_PRIORITIES * k2 + prio
                pltpu.make_async_copy(src_ref.at[pl.ds(k, 1)], out_ref.at[pl.ds(dest_ref[base + k], 1)],
                                      sem).start(priority=prio)
            return a

        lax.fori_loop(0, ROW_TILE // N_DMA_PRIORITIES, issue, 0, unroll=4)
        pltpu.make_async_copy(src_ref, out_ref.at[pl.ds(0, ROW_TILE)], sem).wait()

    @pl.when(i < tiles_p)
    def _():
        run(rp_ref)

    @pl.when(i >= tiles_p)
    def _():
        run(rs_ref)


def _scatter_rows(dest, rows_p, rows_s, n_pad):
    n_p, n_s = rows_p.shape[0], rows_s.shape[0]
    w = rows_p.shape[1]
    tiles_p, tiles_s = n_p // ROW_TILE, n_s // ROW_TILE
    init = jnp.zeros((n_pad, w), rows_p.dtype)
    any_spec = pl.BlockSpec(memory_space=pl.ANY)
    return pl.pallas_call(
        functools.partial(_scatter_kernel, tiles_p=tiles_p),
        out_shape=jax.ShapeDtypeStruct((n_pad, w), rows_p.dtype),
        grid_spec=pltpu.PrefetchScalarGridSpec(
            num_scalar_prefetch=1,
            grid=(tiles_p + tiles_s,),
            in_specs=[
                pl.BlockSpec((ROW_TILE, w), lambda i, dst: (jnp.minimum(i, tiles_p - 1), 0)),
                pl.BlockSpec((ROW_TILE, w), lambda i, dst: (jnp.maximum(i - tiles_p, 0), 0)),
                any_spec,
            ],
            out_specs=any_spec,
            scratch_shapes=[pltpu.SemaphoreType.DMA(())],
        ),
        input_output_aliases={3: 0},
        compiler_params=_cparams("arbitrary"),
        name="scatter_rows",
    )(dest, rows_p, rows_s, init)


def _ffn_kernel(tg_ref, nu_ref, rows_ref, w1_ref, w3_ref, w2_ref, out_ref):
    i = pl.program_id(0)

    @pl.when(i < nu_ref[0])
    def _():
        x = rows_ref[:, 0:D_MODEL].astype(BF16)
        ext = rows_ref[:, D_MODEL:]
        lane = _iota(ext.shape, 1)
        base = EXT_EXPERT0 + EXPERTS_PER_GROUP * tg_ref[i]
        acc = jnp.zeros(out_ref.shape, F32)
        for e in range(EXPERTS_PER_GROUP):
            a = jnp.dot(x, w1_ref[e], preferred_element_type=F32)
            b = jnp.dot(x, w3_ref[e], preferred_element_type=F32)
            comb = jnp.sum(jnp.where(lane == base + e, ext, 0.0), axis=-1, keepdims=True)
            hid = (a * jax.nn.sigmoid(a)) * b * comb
            acc = acc + jnp.dot(hid.astype(BF16), w2_ref[e], preferred_element_type=F32)
        out_ref[...] = acc

    @pl.when(i >= nu_ref[0])
    def _():
        out_ref[...] = jnp.zeros(out_ref.shape, F32)


def _moe_ffn(tile_group, n_used, rows_sorted, w1, w3, w2, n_tiles, layer):
    e, d, f = w1.shape[2:]
    wspec = lambda s1, s2: pl.BlockSpec((None, None, e, s1, s2), lambda i, tg, nu: (layer, tg[i], 0, 0, 0))
    return pl.pallas_call(
        _ffn_kernel,
        out_shape=jax.ShapeDtypeStruct((n_tiles * MOE_TILE, d), F32),
        grid_spec=pltpu.PrefetchScalarGridSpec(
            num_scalar_prefetch=2,
            grid=(n_tiles,),
            in_specs=[
                pl.BlockSpec((MOE_TILE, ROW_W), lambda i, tg, nu: (i, 0)),
                wspec(d, f), wspec(d, f), wspec(f, d),
            ],
            out_specs=pl.BlockSpec((MOE_TILE, d), lambda i, tg, nu: (i, 0)),
        ),
        compiler_params=_cparams("arbitrary"),
        name="moe_ffn",
    )(tile_group, n_used, rows_sorted, w1, w3, w2)


def _ln2_kernel(dest_ref, x_ref, fs_ref, m_ref, mn_ref, lng_ref, lnb_ref, x2_ref, hn_ref, *rest,
                alpha, tile, row0, steps_per_seq, n_steps, dils):
    grouped_refs = rest[:len(dils)]
    fbuf, sem = rest[len(dils):len(dils) + 2]
    step = pl.program_id(0) * steps_per_seq + pl.program_id(1)
    slot = lax.rem(step, 2)

    def start_tile(st, sl):
        base = row0 + st * tile

        def issue(k2, a):
            for prio in range(N_DMA_PRIORITIES):
                k = N_DMA_PRIORITIES * k2 + prio
                pltpu.make_async_copy(fs_ref.at[pl.ds(dest_ref[base + k], 1)], fbuf.at[sl, pl.ds(k, 1)],
                                      sem.at[sl]).start(priority=prio)
            return a

        lax.fori_loop(0, tile // N_DMA_PRIORITIES, issue, 0, unroll=4)

    @pl.when(step == 0)
    def _():
        start_tile(0, 0)

    @pl.when(step + 1 < n_steps)
    def _():
        start_tile(step + 1, 1 - slot)

    pltpu.make_async_copy(fs_ref.at[pl.ds(0, tile)], fbuf.at[slot], sem.at[slot]).wait()
    m = m_ref[...]
    x2 = _layer_norm(alpha * x_ref[...] + _mod_chunk(m, 5) * fbuf[slot], lng_ref[...], lnb_ref[...])
    x2_ref[...] = x2
    mn = mn_ref[...]
    hn = x2 * (1.0 + _mod_chunk(mn, 1)) + _mod_chunk(mn, 0)
    hn_ref[...] = hn.astype(hn_ref.dtype)
    if dils:
        hbuf = rest[-1]
        n_blk = D_MODEL // LANES
        for j in range(n_blk):
            hbuf[j] = hn[:, j * LANES:(j + 1) * LANES]
        for gref, dil in zip(grouped_refs, dils):
            for r in range(dil):
                rows = jnp.concatenate([hbuf[j, pl.ds(r, tile // dil, stride=dil), :] for j in range(n_blk)], axis=-1)
                gref[r] = rows.astype(gref.dtype)


def _mod_spec(tile, d, per_token):
    if per_token:
        return pl.BlockSpec((None, tile, 6 * d), lambda bi, i, *_: (bi, i, 0))
    return pl.BlockSpec((None, 1, 6 * d), lambda bi, i, *_: (bi, 0, 0))


def _ln2(dest, x, f_sorted, row0, m, m_next, ln_g, ln_b, alpha, per_token, dils=()):
    b, s, d = x.shape
    tile = min(ROW_TILE, s)
    nt = s // tile
    mspec = _mod_spec(tile, d, per_token)
    tok = pl.BlockSpec((None, tile, d), lambda bi, i, dst: (bi, i, 0))
    vec = pl.BlockSpec((1, d), lambda bi, i, dst: (0, 0))
    kern = functools.partial(_ln2_kernel, alpha=alpha, tile=tile, row0=row0, steps_per_seq=nt, n_steps=b * nt,
                             dils=tuple(dils))
    grouped_shapes = tuple(jax.ShapeDtypeStruct((b, dil, s // dil, d), BF16) for dil in dils)
    grouped_specs = tuple(pl.BlockSpec((None, dil, tile // dil, d), lambda bi, i, dst: (bi, 0, i, 0)) for dil in dils)
    scratch = [pltpu.VMEM((2, tile, d), F32), pltpu.SemaphoreType.DMA((2,))]
    if dils:
        scratch.append(pltpu.VMEM((d // LANES, tile, LANES), F32))
    return pl.pallas_call(
        kern,
        out_shape=(jax.ShapeDtypeStruct((b, s, d), F32), jax.ShapeDtypeStruct((b, s, d), BF16)) + grouped_shapes,
        grid_spec=pltpu.PrefetchScalarGridSpec(
            num_scalar_prefetch=1,
            grid=(b, nt),
            in_specs=[tok, pl.BlockSpec(memory_space=pl.ANY), mspec, mspec, vec, vec],
            out_specs=(tok, tok) + grouped_specs,
            scratch_shapes=scratch,
        ),
        compiler_params=_cparams("arbitrary", "arbitrary"),
        name="ln2",
    )(dest, x, f_sorted, m, m_next, ln_g, ln_b)


def _rope_tables(pos):
    half = HEAD_DIM // 2
    inv = ROPE_THETA ** (-jnp.arange(half, dtype=F32) / half)
    lane = np.arange(LANES)
    inv_lanes = inv[lane % half]
    sign = jnp.asarray(np.where(lane % HEAD_DIM < half, -1.0, 1.0), F32)
    ang = pos.astype(F32)[..., None] * inv_lanes
    return jnp.cos(ang), jnp.sin(ang) * sign


def _qkv_kernel(*refs, tile, grouped, tail_chunks, tail_first, tail_rows, tiles_per_res, ids=None):
    h_refs = refs[0:N_DIL]
    w_ref = refs[N_DIL]
    cos_refs = refs[N_DIL + 1:2 * N_DIL + 1]
    sin_refs = refs[2 * N_DIL + 1:3 * N_DIL + 1]
    outs = refs[3 * N_DIL + 1:]
    main_refs = outs[:N_DIL] if grouped else ()
    tail_refs = outs[len(main_refs):]
    c, i = ids if ids is not None else (pl.program_id(0), pl.program_id(2))
    first_half = (_iota((tile, LANES), 1) & (HEAD_DIM - 1)) < (HEAD_DIM // 2)

    def rope(acc, cos, sin):
        blocks = []
        for j in range(D_MODEL // LANES):
            blk = acc[:, j * LANES:(j + 1) * LANES]
            partner = jnp.where(first_half, pltpu.roll(blk, LANES - HEAD_DIM // 2, 1), pltpu.roll(blk, HEAD_DIM // 2, 1))
            blocks.append(blk * cos + partner * sin)
        return jnp.concatenate(blocks, axis=-1)

    for grp in range(N_DIL):
        @pl.when(c == grp)
        def _(grp=grp):
            ii = i & (tiles_per_res[grp] - 1)
            h = h_refs[grp][...]
            cos, sin = cos_refs[grp][...], sin_refs[grp][...]
            for j in range(3):
                acc = jnp.dot(h, w_ref[:, j * D_MODEL:(j + 1) * D_MODEL], preferred_element_type=F32)
                res = rope(acc, cos, sin) if j < 2 else acc
                if grouped:
                    main_refs[grp][j] = ((res * (HEAD_DIM ** -0.5)) if j == 0 else res).astype(BF16)
                for t, tref in enumerate(tail_refs):
                    if tail_chunks[t] == 3 * grp + j:
                        @pl.when(ii >= tail_first[t])
                        def _(tref=tref, t=t, res=res):
                            tref[...] = res[tile - tail_rows[t]:, :]


def _qkv_plan(hs, w, tables, tail_keep, grouped, max_tile=ROW_TILE):
    b, d = hs[0].shape[0], hs[0].shape[-1]
    dils = tuple(h.shape[1] for h in hs)
    subs = tuple(h.shape[2] for h in hs)
    tile = min(max_tile, min(subs))
    nt = dils[0] * subs[0] // tile
    tpr = tuple(sub // tile for sub in subs)
    assert all(t & (t - 1) == 0 and dil * t == nt for t, dil in zip(tpr, dils))
    n_chunks = w.shape[1] // d
    tail_chunks = tuple(sorted(tail_keep))
    tail_rows = tuple(min(tail_keep[c], tile) for c in tail_chunks)
    tail_blocks = tuple(tail_keep[c] // r for c, r in zip(tail_chunks, tail_rows))
    tail_first = tuple(tpr[c // 3] - nb for c, nb in zip(tail_chunks, tail_blocks))

    def parked(grp, own, last):
        def imap(c, bi, i):
            before, after = c < grp, c > grp
            return tuple(jnp.where(before, 0, jnp.where(after, l, o)) for o, l in zip(own(bi, i), last))
        return imap

    in_specs, cos_specs, sin_specs = [], [], []
    out_shape, out_specs = [], []
    for grp in range(N_DIL):
        dil, t = dils[grp], tpr[grp]
        in_specs.append(pl.BlockSpec((None, None, tile, d), parked(
            grp, lambda bi, i, t=t: (bi, i // t, i % t, 0), (b - 1, dil - 1, t - 1, 0))))
        tab = pl.BlockSpec((None, tile, LANES), parked(grp, lambda bi, i, t=t: (i // t, i % t, 0), (dil - 1, t - 1, 0)))
        cos_specs.append(tab)
        sin_specs.append(tab)
        if grouped:
            out_shape.append(jax.ShapeDtypeStruct((3, b, dil, subs[grp], d), BF16))
            out_specs.append(pl.BlockSpec((3, None, None, tile, d), parked(
                grp, lambda bi, i, t=t: (0, bi, i // t, i % t, 0), (0, b - 1, dil - 1, t - 1, 0))))
    for c_t, rows, first, nb in zip(tail_chunks, tail_rows, tail_first, tail_blocks):
        dil, t = dils[c_t // 3], tpr[c_t // 3]
        out_shape.append(jax.ShapeDtypeStruct((b, dil, tail_keep[c_t], d), F32))
        out_specs.append(pl.BlockSpec((None, None, rows, d), parked(
            c_t // 3, lambda bi, i, t=t, first=first: (bi, i // t, jnp.maximum(i % t - first, 0), 0),
            (b - 1, dil - 1, nb - 1, 0))))
    n_groups = n_chunks // 3
    w_spec = pl.BlockSpec((d, 3 * d), lambda c, bi, i: (0, jnp.minimum(c, n_groups - 1)))
    return dict(
        kwargs=dict(tile=tile, grouped=grouped, tail_chunks=tail_chunks, tail_first=tail_first, tail_rows=tail_rows,
                    tiles_per_res=tpr),
        grid=(n_groups, b, nt),
        in_specs=in_specs + [w_spec] + cos_specs + sin_specs,
        out_specs=tuple(out_specs),
        out_shape=tuple(out_shape),
        args=(*hs, w, *[t[0] for t in tables], *[t[1] for t in tables]),
    )


def _qkv_proj(hs, w, tables, tail_keep, grouped, max_tile=ROW_TILE):
    plan = _qkv_plan(hs, w, tables, tail_keep, grouped, max_tile)
    return pl.pallas_call(
        functools.partial(_qkv_kernel, **plan["kwargs"]),
        out_shape=plan["out_shape"],
        grid=plan["grid"],
        in_specs=plan["in_specs"],
        out_specs=plan["out_specs"],
        compiler_params=_cparams("arbitrary", "arbitrary", "arbitrary"),
        name="qkv_proj",
    )(*plan["args"])


def _attn_prompt_kernel(q_ref, kp_ref, kc_ref, vp_ref, vc_ref, o_ref, lse_ref, *, n_sub):
    i = pl.program_id(2)
    tq = ATT_BAND
    lane = _iota((tq, LANES), 1)
    low = lane < HEAD_DIM
    qi = _iota((2 * tq, 2 * tq), 0) & (tq - 1)
    kj = _iota((2 * tq, 2 * tq), 1)
    in_prev = (kj < tq) & (kj >= qi)
    in_cur = (kj >= tq) & ((kj - tq) <= qi)
    zero = jnp.zeros((), q_ref.dtype)
    for sub in range(n_sub):
        rows = slice(sub * tq, (sub + 1) * tq)
        before = slice((sub - 1) * tq, sub * tq)
        mask = ((in_prev & (i > 0)) if sub == 0 else in_prev) | in_cur
        lse_all = jnp.zeros((tq, LANES), F32)
        for hp in range(N_HEADS // 2):
            sl = slice(hp * LANES, (hp + 1) * LANES)
            q = q_ref[rows, sl]
            q2 = jnp.concatenate([jnp.where(low, q, zero), jnp.where(low, zero, q)], axis=0)
            k_prev = kp_ref[:, sl] if sub == 0 else kc_ref[before, sl]
            v_prev = vp_ref[:, sl] if sub == 0 else vc_ref[before, sl]
            kw = jnp.concatenate([k_prev, kc_ref[rows, sl]], axis=0)
            vw = jnp.concatenate([v_prev, vc_ref[rows, sl]], axis=0)
            s = lax.dot_general(q2, kw, _NT, preferred_element_type=F32)
            s = jnp.where(mask, s, -jnp.inf)
            mx = jnp.max(s, axis=-1, keepdims=True)
            p = jnp.exp(s - mx)
            den = jnp.sum(p, axis=-1, keepdims=True)
            o2 = jnp.dot(p.astype(vw.dtype), vw, preferred_element_type=F32) * (1.0 / den)
            o_ref[rows, sl] = jnp.where(low, o2[:tq], o2[tq:]).astype(o_ref.dtype)
            lse2 = mx + jnp.log(den)
            lse_all = jnp.where(lane == 2 * hp, lse2[:tq], jnp.where(lane == 2 * hp + 1, lse2[tq:], lse_all))
        lse_ref[rows, :] = lse_all


def _attn_prompt(qkv_g, grp, n_sub=4):
    _, b, dil, sub, d = qkv_g.shape
    tq = ATT_BAND
    n_sub = min(n_sub, sub // tq)
    rows = n_sub * tq
    cur = lambda which: pl.BlockSpec((None, None, None, rows, d), lambda bi, r, i: (which, bi, r, i, 0))
    prev = lambda which: pl.BlockSpec((None, None, None, tq, d),
                                      lambda bi, r, i: (which, bi, r, jnp.maximum(i * n_sub - 1, 0), 0))
    return pl.pallas_call(
        functools.partial(_attn_prompt_kernel, n_sub=n_sub),
        out_shape=(jax.ShapeDtypeStruct((b, dil, sub, d), BF16), jax.ShapeDtypeStruct((b, dil, sub, LANES), F32)),
        grid=(b, dil, sub // rows),
        in_specs=[cur(0), prev(1), cur(1), prev(2), cur(2)],
        out_specs=(
            pl.BlockSpec((None, None, rows, d), lambda bi, r, i: (bi, r, i, 0)),
            pl.BlockSpec((None, None, rows, LANES), lambda bi, r, i: (bi, r, i, 0)),
        ),
        compiler_params=_cparams("arbitrary", "arbitrary", "arbitrary"),
        name=f"attn_prompt_g{grp}",
    )(qkv_g, qkv_g, qkv_g, qkv_g, qkv_g)


def _attn_sample_kernel(qkn_ref, k0_ref, v0_ref, k1_ref, v1_ref, k2_ref, v2_ref, o_ref, *, n_new):
    rows = k0_ref.shape[0]
    hpb = rows // HEAD_DIM
    nq = n_new * hpb
    shift = hpb.bit_length() - 1
    scale = HEAD_DIM ** -0.5
    neg = F32(-jnp.inf)
    own = (_iota((nq, rows), 1) >> 6) == (_iota((nq, rows), 0) & (hpb - 1))
    n_pad = 2 * SUBLANES
    pad = jnp.zeros((n_pad - n_new, rows), F32)
    scores, values = [], []
    for grp, ((_, dil), k_ref, v_ref) in enumerate(zip(DIL_GROUPS, (k0_ref, k1_ref, k2_ref), (v0_ref, v1_ref, v2_ref))):
        q = qkn_ref[0, grp * n_new:(grp + 1) * n_new, :]
        q_rows = jnp.concatenate([jnp.broadcast_to(q[t:t + 1, :], (hpb, rows)) for t in range(n_new)], axis=0)
        qbd = jnp.where(own, q_rows, 0.0).astype(BF16)
        n_buf = k_ref.shape[1]
        s = jnp.dot(qbd, k_ref[...].astype(BF16), preferred_element_type=F32) * scale
        pos = _iota((nq, n_buf), 1)
        t_row = _iota((nq, n_buf), 0) >> shift
        ok = (pos >= t_row) if dil == 1 else ((pos & (dil - 1)) == t_row)
        scores.append(jnp.where(ok, s, neg))
        values.append((v_ref[...].astype(BF16), _NT))
        k_new = jnp.concatenate([qkn_ref[1, grp * n_new:(grp + 1) * n_new, :], pad], axis=0).astype(BF16)
        v_new = jnp.concatenate([qkn_ref[2, grp * n_new:(grp + 1) * n_new, :], pad], axis=0).astype(BF16)
        s_new = lax.dot_general(qbd, k_new, _NT, preferred_element_type=F32) * scale
        t2 = _iota((nq, n_pad), 1)
        tq = _iota((nq, n_pad), 0) >> shift
        ok_new = (t2 <= tq) & (((tq - t2) & (dil - 1)) == 0)
        scores.append(jnp.where(ok_new, s_new, neg))
        values.append((v_new, (((1,), (0,)), ((), ()))))
    mx = functools.reduce(jnp.maximum, [jnp.max(s, axis=-1, keepdims=True) for s in scores])
    den = jnp.zeros((nq, 1), F32)
    acc = jnp.zeros((nq, rows), F32)
    for s, (v, dims) in zip(scores, values):
        p = jnp.exp(s - mx)
        den = den + jnp.sum(p, axis=-1, keepdims=True)
        acc = acc + lax.dot_general(p.astype(BF16), v, dims, preferred_element_type=F32)
    acc = jnp.where(own, acc * (1.0 / den), 0.0)
    out = jnp.zeros((SUBLANES, rows), F32)
    out_r = _iota((SUBLANES, rows), 0)
    for t in range(n_new):
        out = jnp.where(out_r == t, jnp.sum(acc[t * hpb:(t + 1) * hpb, :], axis=0, keepdims=True), out)
    o_ref[...] = out[0:n_new, :]


def _fused_attn_qkv_kernel(*refs, n_attn_in, n_qkv_in, n_new, steps_inner, per_group, tiles_per_batch, qkv_kwargs):
    attn_in = refs[:n_attn_in]
    qkv_in = refs[n_attn_in:n_attn_in + n_qkv_in]
    o_ref = refs[n_attn_in + n_qkv_in]
    qkv_out = refs[n_attn_in + n_qkv_in + 1:]
    _attn_sample_kernel(*attn_in, o_ref, n_new=n_new)
    step = pl.program_id(0) * steps_inner + pl.program_id(1)
    grp = step // per_group
    tile_i = (step - grp * per_group) & (tiles_per_batch - 1)
    _qkv_kernel(*qkv_in, *qkv_out, ids=(grp, tile_i), **qkv_kwargs)


def _attn_sample_with_qkv(qkn, caches_t, n_new, plan, rows=512):
    b, d = qkn.shape[0], qkn.shape[-1]
    steps_inner = d // rows
    n_groups, b_qkv, nt = plan["grid"]
    per_group = b_qkv * nt
    assert n_groups * per_group <= b * steps_inner and nt & (nt - 1) == 0
    assert n_new <= SUBLANES and (rows // HEAD_DIM) & (rows // HEAD_DIM - 1) == 0

    def ids(bi, hh):
        step = bi * steps_inner + hh
        grp = step // per_group
        t = step - grp * per_group
        return grp, t // nt, t % nt

    remap = lambda spec: pl.BlockSpec(spec.block_shape, lambda bi, hh, m=spec.index_map: m(*ids(bi, hh)))
    cache_args, cache_specs = [], []
    for (win, dil), (ck, cv) in zip(DIL_GROUPS, caches_t):
        assert ck.shape == (b, d, win) and win // dil == ATT_BAND and (dil == 1 or dil >= n_new)
        for c in (ck, cv):
            cache_args.append(c)
            cache_specs.append(pl.BlockSpec((None, rows, win), lambda bi, hh: (bi, hh, 0)))
    attn_specs = [pl.BlockSpec((None, 3, qkn.shape[2], rows), lambda bi, hh: (bi, 0, 0, hh))] + cache_specs
    kern = functools.partial(
        _fused_attn_qkv_kernel, n_attn_in=len(attn_specs), n_qkv_in=len(plan["in_specs"]), n_new=n_new,
        steps_inner=steps_inner, per_group=per_group, tiles_per_batch=nt, qkv_kwargs=plan["kwargs"])
    outs = pl.pallas_call(
        kern,
        out_shape=(jax.ShapeDtypeStruct((b, n_new, d), F32),) + plan["out_shape"],
        grid=(b, steps_inner),
        in_specs=attn_specs + [remap(sp) for sp in plan["in_specs"]],
        out_specs=(pl.BlockSpec((None, n_new, rows), lambda bi, hh: (bi, 0, hh)),) + tuple(remap(sp) for sp in plan["out_specs"]),
        compiler_params=_cparams("arbitrary", "arbitrary"),
        name="attn_sample_qkv",
    )(qkn, *cache_args, *plan["args"])
    return outs[0], outs[1:]


def _oproj_kernel(*refs, dils, n_lse, alpha, tile):
    n_groups = len(dils)
    o_refs = refs[:n_groups]
    lse_refs = refs[n_groups:n_groups + n_lse]
    k = n_groups + n_lse
    x_ref, m_ref, wo_ref, hx_ref, lng_ref, lnb_ref, wr_ref, br_ref = refs[k:k + 8]
    x1_ref, hrow_ref, ext_ref = refs[k + 8:k + 11]
    obuf, lbuf = refs[k + 11:]

    def natural(ref, buf, g, dil):
        if dil == 1:
            return ref[0].astype(F32)
        n_blk = ref.shape[-1] // LANES
        for r in range(dil):
            rows = ref[r].astype(F32)
            for j in range(n_blk):
                buf[g * n_blk + j, pl.ds(r, tile // dil, stride=dil), :] = rows[:, j * LANES:(j + 1) * LANES]
        return jnp.concatenate([buf[g * n_blk + j] for j in range(n_blk)], axis=-1)

    os_ = [natural(o_refs[g], obuf, g, dils[g]) for g in range(n_groups)]
    if n_lse == 0:
        o = os_[0].astype(BF16)
    else:
        lses = [natural(lse_refs[g], lbuf, g, dils[g]) for g in range(n_groups)]
        mx = functools.reduce(jnp.maximum, lses)
        es = [jnp.exp(l - mx) for l in lses]
        inv = 1.0 / functools.reduce(lambda a, b: a + b, es)
        o = jnp.zeros(x_ref.shape, F32)
        hx = hx_ref[...]
        for e, og in zip(es, os_):
            wgt = e * inv
            hi = wgt.astype(BF16)
            lo = (wgt - hi.astype(F32)).astype(BF16)
            wexp = jnp.dot(hi, hx, preferred_element_type=F32) + jnp.dot(lo, hx, preferred_element_type=F32)
            o = o + wexp * og
        o = o.astype(BF16)
    y = jnp.dot(o, wo_ref[...], preferred_element_type=F32)
    x1, h2, ext = _post_mixer(x_ref[...], y, m_ref[...], lng_ref, lnb_ref, wr_ref, br_ref, alpha)
    x1_ref[...] = x1
    hrow_ref[:, 0:D_MODEL] = h2
    hrow_ref[:, D_MODEL:] = ext
    ext_ref[...] = ext


def _out_proj(os_, lses, x, m, w_o, head_expand, ln_g, ln_b, wr, br, alpha, per_token):
    b, s, d = x.shape
    tile = min(ROW_TILE, s)
    nt = s // tile
    dils = tuple(o.shape[1] for o in os_)
    tok = pl.BlockSpec((None, tile, d), lambda bi, i: (bi, i, 0))
    grouped = lambda dil, w: pl.BlockSpec((None, dil, tile // dil, w), lambda bi, i: (bi, 0, i, 0))
    full = lambda bi, i: (0, 0)
    return pl.pallas_call(
        functools.partial(_oproj_kernel, dils=dils, n_lse=len(lses), alpha=alpha, tile=tile),
        out_shape=(
            jax.ShapeDtypeStruct((b, s, d), F32),
            jax.ShapeDtypeStruct((b * s, ROW_W), F32),
            jax.ShapeDtypeStruct((b * s, EXT), F32),
        ),
        grid=(b, nt),
        in_specs=[grouped(dil, d) for dil in dils] + [grouped(dil, LANES) for dil in dils[:len(lses)]] + [
            tok, _mod_spec(tile, d, per_token),
            pl.BlockSpec((d, d), full),
            pl.BlockSpec((LANES, d), full),
            pl.BlockSpec((1, d), full),
            pl.BlockSpec((1, d), full),
            pl.BlockSpec((2, d, EXT), lambda *_: (0, 0, 0)),
            pl.BlockSpec((1, EXT), full),
        ],
        out_specs=(
            tok,
            pl.BlockSpec((tile, ROW_W), lambda bi, i: (bi * nt + i, 0)),
            pl.BlockSpec((tile, EXT), lambda bi, i: (bi * nt + i, 0)),
        ),
        scratch_shapes=[pltpu.VMEM((len(dils) * d // LANES, tile, LANES), F32), pltpu.VMEM((len(dils), tile, LANES), F32)],
        compiler_params=_cparams("arbitrary", "arbitrary"),
        name="out_proj",
    )(*os_, *lses, x, m, w_o, head_expand, ln_g, ln_b, wr, br)


def _router_matrix(w_group, b_group, w_er, b_er):
    d = w_group.shape[0]
    n_e = N_EXPERT_GROUPS * EXPERTS_PER_GROUP
    w_e = jnp.transpose(w_er, (1, 0, 2)).reshape(d, n_e)
    pad = EXT - N_EXPERT_GROUPS - n_e
    wr = jnp.concatenate([w_group, w_e, jnp.zeros((d, pad), F32)], axis=1)
    br = jnp.concatenate([b_group, b_er.reshape(n_e), jnp.zeros((pad,), F32)])[None, :]
    wr_hi = wr.astype(BF16)
    wr_lo = (wr - wr_hi.astype(F32)).astype(BF16)
    return jnp.stack([wr_hi, wr_lo]), br


def _moe_experts(rows_p, rows_s, ext_p, ext_s, w1, w3, w2, layer):
    dest, tile_group, n_used, n_tiles = _group_sort(ext_p, ext_s)
    sorted_rows = _scatter_rows(dest, rows_p, rows_s, n_tiles * MOE_TILE)
    return dest, _moe_ffn(tile_group, n_used, sorted_rows, w1, w3, w2, n_tiles, layer)


def kernel(x_prompt, x_sample, state_pool, cache_k_w128, cache_v_w128, cache_k_w512, cache_v_w512, cache_k_w2048, cache_v_w2048, c_prompt, c_sample, mod_w, mod_b, ln1_g, ln1_b, ln2_g, ln2_b, pool_w, pool_scale, attn_w_qkv, attn_w_o, moe_w_group, moe_b_group, moe_w_expert_router, moe_b_expert_router, moe_w1, moe_w3, moe_w2):
    depth = mod_w.shape[0]
    alpha = float((2.0 * depth) ** 0.25)
    bp, s, d = x_prompt.shape
    bs, t_new, _ = x_sample.shape
    n_p, n_s = bp * s, bs * t_new
    kv_caches = ((cache_k_w128, cache_v_w128), (cache_k_w512, cache_v_w512), (cache_k_w2048, cache_v_w2048))

    c_all = jnp.concatenate([c_sample, c_prompt, jnp.zeros((SUBLANES - bp % SUBLANES, d), F32)], axis=0)
    m_all = _modulation(c_all, mod_w, mod_b)
    m_dec = [m_all[i, :bs] for i in range(depth)]
    m_tok = [jnp.repeat(m, t_new, axis=0).reshape(1, n_s, 6 * d) for m in m_dec]
    m_seq = [m_all[i, bs:bs + bp].reshape(bp, 1, 6 * d) for i in range(depth)]

    head_expand = jnp.asarray(np.arange(LANES)[:, None] == (np.arange(d)[None, :] // HEAD_DIM), BF16)
    dils = tuple(dil for _, dil in DIL_GROUPS)
    tables_p = [_rope_tables(jnp.arange(s // dil, dtype=jnp.int32)[None, :] * dil
                             + jnp.arange(dil, dtype=jnp.int32)[:, None]) for dil in dils]
    tables_s = [_rope_tables(PAST_LEN + (jnp.arange(n_s, dtype=jnp.int32) % t_new)[None, :])] * N_DIL

    w1_bf, w3_bf, w2_bf = moe_w1.astype(BF16), moe_w3.astype(BF16), moe_w2.astype(BF16)
    xp, xs = x_prompt, x_sample.reshape(1, n_s, d)
    hp_groups = hs = None
    pool_p, pool_s = [], []
    kv_p = [[] for _ in range(2 * N_DIL)]
    kv_s = [[] for _ in range(2 * N_DIL)]
    for i in range(depth):
        li = i // 2
        wr, br = _router_matrix(moe_w_group[i], moe_b_group[i], moe_w_expert_router[i], moe_b_expert_router[i])
        ln1g, ln1b = ln1_g[i][None, :], ln1_b[i][None, :]
        if i % 2 == 0:
            pw = pool_w[li].astype(BF16)
            ps = pool_scale[li][None, :]
            xp, rows_p, ext_p, tail = _pool_prompt(xp, m_seq[i], pw, ps, ln1g, ln1b, wr, br, alpha)
            x1s, rows_s, ext_s, znew = _pool_sample(
                state_pool[li].reshape(bs, POOL_BUF * d), xs.reshape(bs, t_new * d),
                m_dec[i], pw, ps, ln1g, ln1b, wr, br, alpha)
            xs = x1s.reshape(1, n_s, d)
            rows_s = rows_s.reshape(n_s, ROW_W)
            ext_s = ext_s.reshape(n_s, EXT)
            pool_p.append(tail)
            pool_s.append(znew.reshape(bs, t_new, d))
        else:
            wqkv = attn_w_qkv[li].astype(BF16)
            wo = attn_w_o[li].astype(BF16)
            kv_chunks = [3 * g + j for g in range(N_DIL) for j in (1, 2)]
            keep_p = {3 * g + j: min(win, s) // dil for g, (win, dil) in enumerate(DIL_GROUPS) for j in (1, 2)}
            rows_all = _qkv_proj([hs[:, None]] * N_DIL, wqkv, tables_s, {c: n_s for c in range(3 * N_DIL)},
                                 grouped=False, max_tile=ROW_TILE // 2)
            kinds = [jnp.concatenate([rows_all[3 * g + j].reshape(bs, t_new, d) for g in range(N_DIL)], axis=1)
                     for j in range(3)]
            qkn = jnp.pad(jnp.stack(kinds, axis=1), ((0, 0), (0, 0), (0, 2 * SUBLANES - N_DIL * t_new), (0, 0)))
            caches_t = [tuple(jnp.transpose(c[li], (0, 2, 3, 1)).reshape(bs, d, c.shape[2]) for c in kv)
                        for kv in kv_caches]
            plan = _qkv_plan(hp_groups, wqkv, tables_p, keep_p, grouped=True, max_tile=ROW_TILE // 2)
            o_s, qkv_outs = _attn_sample_with_qkv(qkn, caches_t, t_new, plan)
            qkv_groups, tails_p = qkv_outs[:N_DIL], qkv_outs[N_DIL:]
            for j, c in enumerate(kv_chunks):
                tail = jnp.transpose(tails_p[j], (0, 2, 1, 3))
                kv_p[j].append(tail.reshape(bp, -1, N_HEADS, HEAD_DIM))
                kv_s[j].append(rows_all[c].reshape(bs, t_new, N_HEADS, HEAD_DIM))
            outs = [_attn_prompt(qkv_groups[g], g) for g in range(N_DIL)]
            xp, rows_p, ext_p = _out_proj([o for o, _ in outs], [l for _, l in outs], xp, m_seq[i], wo, head_expand,
                                          ln1g, ln1b, wr, br, alpha, per_token=False)
            xs, rows_s, ext_s = _out_proj([o_s.reshape(1, 1, n_s, d)], [], xs, m_tok[i], wo, head_expand,
                                          ln1g, ln1b, wr, br, alpha, per_token=True)
        dest, f_sorted = _moe_experts(rows_p, rows_s, ext_p, ext_s, w1_bf, w3_bf, w2_bf, i)
        nxt = min(i + 1, depth - 1)
        ln2g, ln2b = ln2_g[i][None, :], ln2_b[i][None, :]
        regroup = tuple(dil for dil in dils if dil > 1) if (i + 1 < depth and (i + 1) % 2 == 1) else ()
        xp, hp, *hp_dilated = _ln2(dest, xp, f_sorted, 0, m_seq[i], m_seq[nxt], ln2g, ln2b, alpha, per_token=False,
                                   dils=regroup)
        if regroup:
            hp_dilated = iter(hp_dilated)
            hp_groups = [hp[:, None] if dil == 1 else next(hp_dilated) for dil in dils]
        xs, hs = _ln2(dest, xs, f_sorted, n_p, m_tok[i], m_tok[nxt], ln2g, ln2b, alpha, per_token=True)

    stack = lambda lst: jnp.stack(lst, axis=0)
    return (xp, xs.reshape(bs, t_new, d), stack(pool_p), stack(pool_s),
            *[stack(kv_p[j]) for j in range(2 * N_DIL)], *[stack(kv_s[j]) for j in range(2 * N_DIL)])
```

```python
import functools

import numpy as np
import jax
import jax.numpy as jnp
from jax import lax
from jax.experimental import pallas as pl
from jax.experimental.pallas import tpu as pltpu

F32 = jnp.float32
BF16 = jnp.bfloat16
HIGHEST = lax.Precision.HIGHEST

D_MODEL = 1024
POOL_WINDOWS = (2, 4, 8, 16)
N_POOL_GROUPS = len(POOL_WINDOWS)
POOL_GROUP = D_MODEL // N_POOL_GROUPS
POOL_BUF = max(POOL_WINDOWS) - 1
POOL_HALO = 16
HEAD_DIM = 64
N_HEADS = D_MODEL // HEAD_DIM
DIL_GROUPS = ((128, 1), (512, 4), (2048, 16))
N_DIL = len(DIL_GROUPS)
ATT_BAND = 128
ROPE_THETA = 10000.0
N_EXPERT_GROUPS = 4
EXPERTS_PER_GROUP = 8
PAST_LEN = 2048
LN_EPS = 1e-5

LANES = 128
SUBLANES = 8
VMEM_LIMIT_BYTES = 56 * 1024 * 1024

EXT = LANES
EXT_EXPERT0 = N_EXPERT_GROUPS
EXT_ONEHOT0 = 120
ROW_W = D_MODEL + EXT

MOE_TILE = 512
SORT_CHUNK = 512
ROW_TILE = 512
N_DMA_PRIORITIES = 2

_NT = (((1,), (1,)), ((), ()))


def _cparams(*sem):
    return pltpu.CompilerParams(dimension_semantics=sem, vmem_limit_bytes=VMEM_LIMIT_BYTES)


def _iota(shape, dim):
    return lax.broadcasted_iota(jnp.int32, shape, dim)


def _mod_kernel(c_ref, w_ref, b_ref, o_ref):
    o_ref[...] = jnp.dot(c_ref[...], w_ref[...], precision=HIGHEST, preferred_element_type=F32) + b_ref[...]


def _modulation(c_all, mod_w, mod_b):
    n_layers, d, n_out = mod_w.shape
    rows = c_all.shape[0]
    tn = n_out // 4
    return pl.pallas_call(
        _mod_kernel,
        out_shape=jax.ShapeDtypeStruct((n_layers, rows, n_out), F32),
        grid=(n_layers, n_out // tn),
        in_specs=[
            pl.BlockSpec((rows, d), lambda l, j: (0, 0)),
            pl.BlockSpec((None, d, tn), lambda l, j: (l, 0, j)),
            pl.BlockSpec((None, 1, tn), lambda l, j: (l, 0, j)),
        ],
        out_specs=pl.BlockSpec((None, rows, tn), lambda l, j: (l, 0, j)),
        compiler_params=_cparams("arbitrary", "arbitrary"),
        name="modulation",
    )(c_all, mod_w, mod_b.reshape(n_layers, 1, n_out))


def _mod_chunk(m, j):
    return m[:, j * D_MODEL:(j + 1) * D_MODEL]


def _layer_norm(u, g, b):
    mu = jnp.mean(u, axis=-1, keepdims=True)
    uc = u - mu
    var = jnp.mean(uc * uc, axis=-1, keepdims=True)
    return uc * lax.rsqrt(var + LN_EPS) * g + b


def _route(h2, wr_ref, br_ref):
    h_hi = h2.astype(BF16)
    h_lo = (h2 - h_hi.astype(F32)).astype(BF16)
    dot = lambda a, b: jnp.dot(a, b, preferred_element_type=F32)
    logits = dot(h_hi, wr_ref[0]) + dot(h_lo, wr_ref[0]) + dot(h_hi, wr_ref[1]) + br_ref[...]
    lane_f = _iota(logits.shape, 1).astype(F32)
    neg = F32(-jnp.inf)
    big = F32(EXT)
    gl = jnp.where(lane_f < N_EXPERT_GROUPS, logits, neg)
    gmax = jnp.max(gl, axis=-1, keepdims=True)
    gidx = jnp.min(jnp.where(gl == gmax, lane_f, big), axis=-1, keepdims=True)
    g_p = 1.0 / jnp.sum(jnp.exp(gl - gmax), axis=-1, keepdims=True)
    e_lo = EXT_EXPERT0 + EXPERTS_PER_GROUP * gidx
    el = jnp.where((lane_f >= e_lo) & (lane_f < e_lo + EXPERTS_PER_GROUP), logits, neg)
    v1 = jnp.max(el, axis=-1, keepdims=True)
    i1 = jnp.min(jnp.where(el == v1, lane_f, big), axis=-1, keepdims=True)
    el2 = jnp.where(lane_f == i1, neg, el)
    v2 = jnp.max(el2, axis=-1, keepdims=True)
    i2 = jnp.min(jnp.where(el2 == v2, lane_f, big), axis=-1, keepdims=True)
    r = jnp.exp(v2 - v1)
    w1 = g_p / (1.0 + r)
    w2 = w1 * r
    ext = jnp.where(lane_f == i1, w1, 0.0) + jnp.where(lane_f == i2, w2, 0.0)
    return ext + jnp.where(lane_f == gidx + EXT_ONEHOT0, 1.0, 0.0)


def _post_mixer(x, y, m, lng_ref, lnb_ref, wr_ref, br_ref, alpha):
    x1 = _layer_norm(alpha * x + _mod_chunk(m, 2) * y, lng_ref[...], lnb_ref[...])
    h2 = x1 * (1.0 + _mod_chunk(m, 4)) + _mod_chunk(m, 3)
    return x1, h2, _route(h2, wr_ref, br_ref)


def _pool_prompt_kernel(x_ref, xprev_ref, m_ref, pw_ref, ps_ref, lng_ref, lnb_ref, wr_ref, br_ref,
                        x1_ref, hrow_ref, ext_ref, tail_ref, zbuf, *, tile, alpha):
    i = pl.program_id(1)
    m = m_ref[...]
    shift, scale = _mod_chunk(m, 0), _mod_chunk(m, 1)
    x = x_ref[...]
    z = x * (1.0 + scale) + shift
    zprev = xprev_ref[...] * (1.0 + scale) + shift
    zbuf[0:POOL_HALO, :] = jnp.where(i > 0, zprev, 0.0)
    zbuf[POOL_HALO:, :] = z
    pos = i * tile + _iota((tile, 1), 0)
    ys = []
    for g, w in enumerate(POOL_WINDOWS):
        cols = slice(g * POOL_GROUP, (g + 1) * POOL_GROUP)
        win = zbuf[pl.ds(POOL_HALO, tile), cols]
        for j in range(1, w):
            win = win + zbuf[pl.ds(POOL_HALO - j, tile), cols]
        cnt = jnp.minimum(pos + 1, w).astype(F32)
        d = win / cnt - z[:, cols]
        ys.append(jnp.dot(d.astype(BF16), pw_ref[g], preferred_element_type=F32))
    y = jnp.concatenate(ys, axis=-1) * ps_ref[...]
    x1, h2, ext = _post_mixer(x, y, m, lng_ref, lnb_ref, wr_ref, br_ref, alpha)
    x1_ref[...] = x1
    hrow_ref[:, 0:D_MODEL] = h2
    hrow_ref[:, D_MODEL:] = ext
    ext_ref[...] = ext

    @pl.when(i == pl.num_programs(1) - 1)
    def _():
        tail_ref[...] = zbuf[pl.ds(POOL_HALO + tile - POOL_BUF, POOL_BUF), :]


def _pool_prompt(x, m, pool_w, pool_scale, ln_g, ln_b, wr, br, alpha, tile=ROW_TILE):
    b, s, d = x.shape
    nt = s // tile
    halo_blocks = tile // POOL_HALO
    full = lambda bi, i: (0, 0)
    kern = functools.partial(_pool_prompt_kernel, tile=tile, alpha=alpha)
    return pl.pallas_call(
        kern,
        out_shape=(
            jax.ShapeDtypeStruct((b, s, d), F32),
            jax.ShapeDtypeStruct((b * s, ROW_W), F32),
            jax.ShapeDtypeStruct((b * s, EXT), F32),
            jax.ShapeDtypeStruct((b, POOL_BUF, d), F32),
        ),
        grid=(b, nt),
        in_specs=[
            pl.BlockSpec((None, tile, d), lambda bi, i: (bi, i, 0)),
            pl.BlockSpec((None, POOL_HALO, d), lambda bi, i: (bi, jnp.maximum(i * halo_blocks - 1, 0), 0)),
            pl.BlockSpec((None, 1, 6 * d), lambda bi, i: (bi, 0, 0)),
            pl.BlockSpec((N_POOL_GROUPS, POOL_GROUP, POOL_GROUP), lambda bi, i: (0, 0, 0)),
            pl.BlockSpec((1, d), full),
            pl.BlockSpec((1, d), full),
            pl.BlockSpec((1, d), full),
            pl.BlockSpec((2, d, EXT), lambda *_: (0, 0, 0)),
            pl.BlockSpec((1, EXT), full),
        ],
        out_specs=(
            pl.BlockSpec((None, tile, d), lambda bi, i: (bi, i, 0)),
            pl.BlockSpec((tile, ROW_W), lambda bi, i: (bi * nt + i, 0)),
            pl.BlockSpec((tile, EXT), lambda bi, i: (bi * nt + i, 0)),
            pl.BlockSpec((None, POOL_BUF, d), lambda bi, i: (bi, 0, 0)),
        ),
        scratch_shapes=[pltpu.VMEM((tile + POOL_HALO, d), F32)],
        compiler_params=_cparams("arbitrary", "arbitrary"),
        name="pool_prompt",
    )(x, x, m, pool_w, pool_scale, ln_g, ln_b, wr, br)


def _pool_sample_kernel(st_ref, x_ref, m_ref, pw_ref, ps_ref, lng_ref, lnb_ref, wr_ref, br_ref,
                        x1_ref, hrow_ref, ext_ref, znew_ref, *, n_new, alpha):
    d = D_MODEL
    rows = [st_ref[:, r * d:(r + 1) * d] for r in range(POOL_BUF)]
    xs = []
    m = m_ref[...]
    for t in range(n_new):
        x = x_ref[:, t * d:(t + 1) * d]
        z = x * (1.0 + _mod_chunk(m, 1)) + _mod_chunk(m, 0)
        znew_ref[:, t * d:(t + 1) * d] = z
        rows.append(z)
        xs.append(x)
    for t in range(n_new):
        last = POOL_BUF + t
        ys = []
        for g, w in enumerate(POOL_WINDOWS):
            cols = slice(g * POOL_GROUP, (g + 1) * POOL_GROUP)
            win = rows[last][:, cols]
            for j in range(1, w):
                win = win + rows[last - j][:, cols]
            dgrp = win / F32(w) - rows[last][:, cols]
            ys.append(jnp.dot(dgrp.astype(BF16), pw_ref[g], preferred_element_type=F32))
        y = jnp.concatenate(ys, axis=-1) * ps_ref[...]
        x1, h2, ext = _post_mixer(xs[t], y, m, lng_ref, lnb_ref, wr_ref, br_ref, alpha)
        x1_ref[:, t * d:(t + 1) * d] = x1
        hrow_ref[:, t * ROW_W:t * ROW_W + d] = h2
        hrow_ref[:, t * ROW_W + d:(t + 1) * ROW_W] = ext
        ext_ref[:, t * EXT:(t + 1) * EXT] = ext


def _pool_sample(state, x, m, pool_w, pool_scale, ln_g, ln_b, wr, br, alpha, bb=32):
    b = x.shape[0]
    d = D_MODEL
    n_new = x.shape[1] // d
    full = lambda i: (0, 0)
    kern = functools.partial(_pool_sample_kernel, n_new=n_new, alpha=alpha)
    return pl.pallas_call(
        kern,
        out_shape=(
            jax.ShapeDtypeStruct((b, n_new * d), F32),
            jax.ShapeDtypeStruct((b, n_new * ROW_W), F32),
            jax.ShapeDtypeStruct((b, n_new * EXT), F32),
            jax.ShapeDtypeStruct((b, n_new * d), F32),
        ),
        grid=(b // bb,),
        in_specs=[
            pl.BlockSpec((bb, POOL_BUF * d), lambda i: (i, 0)),
            pl.BlockSpec((bb, n_new * d), lambda i: (i, 0)),
            pl.BlockSpec((bb, 6 * d), lambda i: (i, 0)),
            pl.BlockSpec((N_POOL_GROUPS, POOL_GROUP, POOL_GROUP), lambda i: (0, 0, 0)),
            pl.BlockSpec((1, d), full),
            pl.BlockSpec((1, d), full),
            pl.BlockSpec((1, d), full),
            pl.BlockSpec((2, d, EXT), lambda *_: (0, 0, 0)),
            pl.BlockSpec((1, EXT), full),
        ],
        out_specs=(
            pl.BlockSpec((bb, n_new * d), lambda i: (i, 0)),
            pl.BlockSpec((bb, n_new * ROW_W), lambda i: (i, 0)),
            pl.BlockSpec((bb, n_new * EXT), lambda i: (i, 0)),
            pl.BlockSpec((bb, n_new * d), lambda i: (i, 0)),
        ),
        compiler_params=_cparams("arbitrary"),
        name="pool_sample",
    )(state, x, m, pool_w, pool_scale, ln_g, ln_b, wr, br)


def _sort_kernel(extp_ref, exts_ref, dest_ref, meta_ref, oh_ref, *, n_chunks_p, n_chunks_s):
    ch = SORT_CHUNK
    r_io = _iota((SUBLANES, EXT), 0)
    l_io = _iota((SUBLANES, EXT), 1)
    sel = jnp.where((l_io == r_io + EXT_ONEHOT0) & (r_io < N_EXPERT_GROUPS), 1.0, 0.0).astype(BF16)

    def count_from(ext_ref, chunk0):
        def body(c, cnt):
            ext = ext_ref[pl.ds(pl.multiple_of(c * ch, ch), ch), :]
            oh = lax.dot_general(sel, ext.astype(BF16), _NT, preferred_element_type=F32)
            oh_ref[chunk0 + c] = oh
            return cnt + jnp.sum(oh, axis=-1, keepdims=True)
        return body

    counts = lax.fori_loop(0, n_chunks_p, count_from(extp_ref, 0), jnp.zeros((SUBLANES, 1), F32))
    counts = lax.fori_loop(0, n_chunks_s, count_from(exts_ref, n_chunks_p), counts)
    padded = jnp.floor((counts + (MOE_TILE - 1)) * (1.0 / MOE_TILE)) * MOE_TILE
    row = _iota((SUBLANES, 1), 0)
    starts = jnp.zeros((SUBLANES, 1), F32)
    for g in range(1, N_EXPERT_GROUPS):
        starts = starts + jnp.where(row >= g, padded[g - 1:g, :], 0.0)
    tri = jnp.where(_iota((ch, ch), 0) < _iota((ch, ch), 1), 1.0, 0.0).astype(BF16)

    def dest_body(c, base):
        oh = oh_ref[c]
        pre = jnp.dot(oh.astype(BF16), tri, preferred_element_type=F32)
        dest = jnp.sum(oh * (base + pre), axis=0, keepdims=True)
        dest_ref[pl.ds(c, 1), :] = dest.astype(jnp.int32)
        return base + jnp.sum(oh, axis=-1, keepdims=True)

    lax.fori_loop(0, n_chunks_p + n_chunks_s, dest_body, starts)
    ends = starts + padded
    mrow = _iota((SUBLANES, EXT), 0)
    tile_lo = (_iota((SUBLANES, EXT), 1) * MOE_TILE).astype(F32)
    tgroup = jnp.sum(jnp.where((mrow < N_EXPERT_GROUPS - 1) & (tile_lo >= ends), 1.0, 0.0), axis=0, keepdims=True)
    n_used = jnp.sum(jnp.where(row < N_EXPERT_GROUPS, padded, 0.0), axis=0, keepdims=True) * (1.0 / MOE_TILE)
    pad_start = jnp.sum(jnp.where(mrow == _iota((SUBLANES, EXT), 1), starts + counts, 0.0), axis=0, keepdims=True)
    meta = jnp.where(mrow == 0, tgroup, jnp.where(mrow == 1, n_used, jnp.where(mrow == 2, pad_start, 0.0)))
    meta_ref[...] = meta.astype(jnp.int32)


def _group_sort(ext_p, ext_s):
    n_p, n_s = ext_p.shape[0], ext_s.shape[0]
    ncp, ncs = n_p // SORT_CHUNK, n_s // SORT_CHUNK
    n_tiles = (n_p + n_s) // MOE_TILE + N_EXPERT_GROUPS
    assert n_tiles <= EXT
    dest, meta = pl.pallas_call(
        functools.partial(_sort_kernel, n_chunks_p=ncp, n_chunks_s=ncs),
        out_shape=(
            jax.ShapeDtypeStruct((ncp + ncs, SORT_CHUNK), jnp.int32),
            jax.ShapeDtypeStruct((SUBLANES, EXT), jnp.int32),
        ),
        grid=(1,),
        in_specs=[pl.BlockSpec((n_p, EXT), lambda i: (0, 0)), pl.BlockSpec((n_s, EXT), lambda i: (0, 0))],
        out_specs=(
            pl.BlockSpec((ncp + ncs, SORT_CHUNK), lambda i: (0, 0)),
            pl.BlockSpec((SUBLANES, EXT), lambda i: (0, 0)),
        ),
        scratch_shapes=[pltpu.VMEM((ncp + ncs, SUBLANES, SORT_CHUNK), F32)],
        compiler_params=_cparams("arbitrary"),
        name="group_sort",
    )(ext_p, ext_s)
    return dest.reshape(-1), meta[0, :n_tiles], meta[1, :1], meta[2, :N_EXPERT_GROUPS], n_tiles


def _scatter_kernel(dest_ref, pad_ref, rp_ref, rs_ref, out_ref, zbuf, sem, zsem, *, tiles_p, n_tokens, n_tail_blocks):
    i = pl.program_id(0)
    base = i * ROW_TILE

    @pl.when(i == 0)
    def _():
        zbuf[...] = jnp.zeros(zbuf.shape, zbuf.dtype)
        for g in range(N_EXPERT_GROUPS):
            row0 = pl.multiple_of((pad_ref[g] // SUBLANES) * SUBLANES, SUBLANES)
            zero_copy = pltpu.make_async_copy(zbuf, out_ref.at[pl.ds(row0, zbuf.shape[0])], zsem)
            zero_copy.start()
            zero_copy.wait()
        for k in range(n_tail_blocks):
            zero_copy = pltpu.make_async_copy(zbuf, out_ref.at[pl.ds(n_tokens + k * zbuf.shape[0], zbuf.shape[0])], zsem)
            zero_copy.start()
            zero_copy.wait()

    def run(src_ref):
        def issue(k2, a):
            for prio in range(N_DMA_PRIORITIES):
                k = N_DMA_PRIORITIES * k2 + prio
                pltpu.make_async_copy(src_ref.at[pl.ds(k, 1)], out_ref.at[pl.ds(dest_ref[base + k], 1)],
                                      sem).start(priority=prio)
            return a

        lax.fori_loop(0, ROW_TILE // N_DMA_PRIORITIES, issue, 0, unroll=4)
        pltpu.make_async_copy(src_ref, out_ref.at[pl.ds(0, ROW_TILE)], sem).wait()

    @pl.when(i < tiles_p)
    def _():
        run(rp_ref)

    @pl.when(i >= tiles_p)
    def _():
        run(rs_ref)


def _scatter_rows(dest, pad_start, rows_p, rows_s, n_pad):
    n_p, n_s = rows_p.shape[0], rows_s.shape[0]
    w = rows_p.shape[1]
    tiles_p, tiles_s = n_p // ROW_TILE, n_s // ROW_TILE
    zero_rows = MOE_TILE + SUBLANES
    n_tokens = n_p + n_s
    n_tail_blocks = -(-(n_pad + zero_rows - n_tokens) // zero_rows)
    return pl.pallas_call(
        functools.partial(_scatter_kernel, tiles_p=tiles_p, n_tokens=n_tokens, n_tail_blocks=n_tail_blocks),
        out_shape=jax.ShapeDtypeStruct((n_tokens + n_tail_blocks * zero_rows, w), rows_p.dtype),
        grid_spec=pltpu.PrefetchScalarGridSpec(
            num_scalar_prefetch=2,
            grid=(tiles_p + tiles_s,),
            in_specs=[
                pl.BlockSpec((ROW_TILE, w), lambda i, dst, pad: (jnp.minimum(i, tiles_p - 1), 0)),
                pl.BlockSpec((ROW_TILE, w), lambda i, dst, pad: (jnp.maximum(i - tiles_p, 0), 0)),
            ],
            out_specs=pl.BlockSpec(memory_space=pl.ANY),
            scratch_shapes=[pltpu.VMEM((zero_rows, w), rows_p.dtype), pltpu.SemaphoreType.DMA(()),
                            pltpu.SemaphoreType.DMA(())],
        ),
        compiler_params=_cparams("arbitrary"),
        name="scatter_rows",
    )(dest, pad_start, rows_p, rows_s)


def _ffn_kernel(tg_ref, nu_ref, rows_ref, w1_ref, w3_ref, w2_ref, out_ref):
    i = pl.program_id(0)

    @pl.when(i < nu_ref[0])
    def _():
        x = rows_ref[:, 0:D_MODEL].astype(BF16)
        ext = rows_ref[:, D_MODEL:]
        lane = _iota(ext.shape, 1)
        base = EXT_EXPERT0 + EXPERTS_PER_GROUP * tg_ref[i]
        acc = jnp.zeros(out_ref.shape, F32)
        for e in range(EXPERTS_PER_GROUP):
            a = jnp.dot(x, w1_ref[e], preferred_element_type=F32)
            b = jnp.dot(x, w3_ref[e], preferred_element_type=F32)
            comb = jnp.sum(jnp.where(lane == base + e, ext, 0.0), axis=-1, keepdims=True)
            hid = (a * jax.nn.sigmoid(a)) * b * comb
            acc = acc + jnp.dot(hid.astype(BF16), w2_ref[e], preferred_element_type=F32)
        out_ref[...] = acc

    @pl.when(i >= nu_ref[0])
    def _():
        out_ref[...] = jnp.zeros(out_ref.shape, F32)


def _moe_ffn(tile_group, n_used, rows_sorted, w1, w3, w2, n_tiles, layer):
    e, d, f = w1.shape[2:]
    wspec = lambda s1, s2: pl.BlockSpec((None, None, e, s1, s2), lambda i, tg, nu: (layer, tg[i], 0, 0, 0))
    return pl.pallas_call(
        _ffn_kernel,
        out_shape=jax.ShapeDtypeStruct((n_tiles * MOE_TILE, d), F32),
        grid_spec=pltpu.PrefetchScalarGridSpec(
            num_scalar_prefetch=2,
            grid=(n_tiles,),
            in_specs=[
                pl.BlockSpec((MOE_TILE, ROW_W), lambda i, tg, nu: (jnp.minimum(i, nu[0] - 1), 0)),
                wspec(d, f), wspec(d, f), wspec(f, d),
            ],
            out_specs=pl.BlockSpec((MOE_TILE, d), lambda i, tg, nu: (i, 0)),
        ),
        compiler_params=_cparams("arbitrary"),
        name="moe_ffn",
    )(tile_group, n_used, rows_sorted, w1, w3, w2)


def _ln2_kernel(dest_ref, x_ref, fs_ref, m_ref, mn_ref, lng_ref, lnb_ref, x2_ref, hn_ref, *rest,
                alpha, tile, row0, steps_per_seq, n_steps, dils):
    grouped_refs = rest[:len(dils)]
    fbuf, sem = rest[len(dils):len(dils) + 2]
    step = pl.program_id(0) * steps_per_seq + pl.program_id(1)
    slot = lax.rem(step, 2)

    def start_tile(st, sl):
        base = row0 + st * tile

        def issue(k2, a):
            for prio in range(N_DMA_PRIORITIES):
                k = N_DMA_PRIORITIES * k2 + prio
                pltpu.make_async_copy(fs_ref.at[pl.ds(dest_ref[base + k], 1)], fbuf.at[sl, pl.ds(k, 1)],
                                      sem.at[sl]).start(priority=prio)
            return a

        lax.fori_loop(0, tile // N_DMA_PRIORITIES, issue, 0, unroll=4)

    @pl.when(step == 0)
    def _():
        start_tile(0, 0)

    @pl.when(step + 1 < n_steps)
    def _():
        start_tile(step + 1, 1 - slot)

    pltpu.make_async_copy(fs_ref.at[pl.ds(0, tile)], fbuf.at[slot], sem.at[slot]).wait()
    m = m_ref[...]
    x2 = _layer_norm(alpha * x_ref[...] + _mod_chunk(m, 5) * fbuf[slot], lng_ref[...], lnb_ref[...])
    x2_ref[...] = x2
    mn = mn_ref[...]
    hn = x2 * (1.0 + _mod_chunk(mn, 1)) + _mod_chunk(mn, 0)
    hn_ref[...] = hn.astype(hn_ref.dtype)
    if dils:
        hbuf = rest[-1]
        n_blk = D_MODEL // LANES
        for j in range(n_blk):
            hbuf[j] = hn[:, j * LANES:(j + 1) * LANES]
        for gref, dil in zip(grouped_refs, dils):
            for r in range(dil):
                rows = jnp.concatenate([hbuf[j, pl.ds(r, tile // dil, stride=dil), :] for j in range(n_blk)], axis=-1)
                gref[r] = rows.astype(gref.dtype)


def _mod_spec(tile, d, per_token):
    if per_token:
        return pl.BlockSpec((None, tile, 6 * d), lambda bi, i, *_: (bi, i, 0))
    return pl.BlockSpec((None, 1, 6 * d), lambda bi, i, *_: (bi, 0, 0))


def _ln2(dest, x, f_sorted, row0, m, m_next, ln_g, ln_b, alpha, per_token, dils=()):
    b, s, d = x.shape
    tile = min(ROW_TILE, s)
    nt = s // tile
    mspec = _mod_spec(tile, d, per_token)
    tok = pl.BlockSpec((None, tile, d), lambda bi, i, dst: (bi, i, 0))
    vec = pl.BlockSpec((1, d), lambda bi, i, dst: (0, 0))
    kern = functools.partial(_ln2_kernel, alpha=alpha, tile=tile, row0=row0, steps_per_seq=nt, n_steps=b * nt,
                             dils=tuple(dils))
    grouped_shapes = tuple(jax.ShapeDtypeStruct((b, dil, s // dil, d), BF16) for dil in dils)
    grouped_specs = tuple(pl.BlockSpec((None, dil, tile // dil, d), lambda bi, i, dst: (bi, 0, i, 0)) for dil in dils)
    scratch = [pltpu.VMEM((2, tile, d), F32), pltpu.SemaphoreType.DMA((2,))]
    if dils:
        scratch.append(pltpu.VMEM((d // LANES, tile, LANES), F32))
    return pl.pallas_call(
        kern,
        out_shape=(jax.ShapeDtypeStruct((b, s, d), F32), jax.ShapeDtypeStruct((b, s, d), BF16)) + grouped_shapes,
        grid_spec=pltpu.PrefetchScalarGridSpec(
            num_scalar_prefetch=1,
            grid=(b, nt),
            in_specs=[tok, pl.BlockSpec(memory_space=pl.ANY), mspec, mspec, vec, vec],
            out_specs=(tok, tok) + grouped_specs,
            scratch_shapes=scratch,
        ),
        compiler_params=_cparams("arbitrary", "arbitrary"),
        name="ln2",
    )(dest, x, f_sorted, m, m_next, ln_g, ln_b)


def _rope_tables(pos):
    half = HEAD_DIM // 2
    inv = ROPE_THETA ** (-jnp.arange(half, dtype=F32) / half)
    lane = np.arange(LANES)
    inv_lanes = inv[lane % half]
    sign = jnp.asarray(np.where(lane % HEAD_DIM < half, -1.0, 1.0), F32)
    ang = pos.astype(F32)[..., None] * inv_lanes
    return jnp.cos(ang), jnp.sin(ang) * sign


def _qkv_kernel(*refs, tile, grouped, tail_chunks, tail_first, tail_rows, tiles_per_res, ids=None):
    h_refs = refs[0:N_DIL]
    w_ref = refs[N_DIL]
    cos_refs = refs[N_DIL + 1:2 * N_DIL + 1]
    sin_refs = refs[2 * N_DIL + 1:3 * N_DIL + 1]
    outs = refs[3 * N_DIL + 1:]
    main_refs = outs[:N_DIL] if grouped else ()
    tail_refs = outs[len(main_refs):]
    c, i = ids if ids is not None else (pl.program_id(0), pl.program_id(2))
    first_half = (_iota((tile, LANES), 1) & (HEAD_DIM - 1)) < (HEAD_DIM // 2)

    def rope(acc, cos, sin):
        blocks = []
        for j in range(D_MODEL // LANES):
            blk = acc[:, j * LANES:(j + 1) * LANES]
            partner = jnp.where(first_half, pltpu.roll(blk, LANES - HEAD_DIM // 2, 1), pltpu.roll(blk, HEAD_DIM // 2, 1))
            blocks.append(blk * cos + partner * sin)
        return jnp.concatenate(blocks, axis=-1)

    for grp in range(N_DIL):
        @pl.when(c == grp)
        def _(grp=grp):
            ii = i & (tiles_per_res[grp] - 1)
            h = h_refs[grp][...]
            cos, sin = cos_refs[grp][...], sin_refs[grp][...]
            for j in range(3):
                acc = jnp.dot(h, w_ref[:, j * D_MODEL:(j + 1) * D_MODEL], preferred_element_type=F32)
                res = rope(acc, cos, sin) if j < 2 else acc
                if grouped:
                    main_refs[grp][j] = ((res * (HEAD_DIM ** -0.5)) if j == 0 else res).astype(BF16)
                for t, tref in enumerate(tail_refs):
                    if tail_chunks[t] == 3 * grp + j:
                        @pl.when(ii >= tail_first[t])
                        def _(tref=tref, t=t, res=res):
                            tref[...] = res[tile - tail_rows[t]:, :]


def _qkv_plan(hs, w, tables, tail_keep, grouped, max_tile=ROW_TILE):
    b, d = hs[0].shape[0], hs[0].shape[-1]
    dils = tuple(h.shape[1] for h in hs)
    subs = tuple(h.shape[2] for h in hs)
    tile = min(max_tile, min(subs))
    nt = dils[0] * subs[0] // tile
    tpr = tuple(sub // tile for sub in subs)
    assert all(t & (t - 1) == 0 and dil * t == nt for t, dil in zip(tpr, dils))
    n_chunks = w.shape[1] // d
    tail_chunks = tuple(sorted(tail_keep))
    tail_rows = tuple(min(tail_keep[c], tile) for c in tail_chunks)
    tail_blocks = tuple(tail_keep[c] // r for c, r in zip(tail_chunks, tail_rows))
    tail_first = tuple(tpr[c // 3] - nb for c, nb in zip(tail_chunks, tail_blocks))

    def parked(grp, own, last):
        def imap(c, bi, i):
            before, after = c < grp, c > grp
            return tuple(jnp.where(before, 0, jnp.where(after, l, o)) for o, l in zip(own(bi, i), last))
        return imap

    in_specs, cos_specs, sin_specs = [], [], []
    out_shape, out_specs = [], []
    for grp in range(N_DIL):
        dil, t = dils[grp], tpr[grp]
        in_specs.append(pl.BlockSpec((None, None, tile, d), parked(
            grp, lambda bi, i, t=t: (bi, i // t, i % t, 0), (b - 1, dil - 1, t - 1, 0))))
        tab = pl.BlockSpec((None, tile, LANES), parked(grp, lambda bi, i, t=t: (i // t, i % t, 0), (dil - 1, t - 1, 0)))
        cos_specs.append(tab)
        sin_specs.append(tab)
        if grouped:
            out_shape.append(jax.ShapeDtypeStruct((3, b, dil, subs[grp], d), BF16))
            out_specs.append(pl.BlockSpec((3, None, None, tile, d), parked(
                grp, lambda bi, i, t=t: (0, bi, i // t, i % t, 0), (0, b - 1, dil - 1, t - 1, 0))))
    for c_t, rows, first, nb in zip(tail_chunks, tail_rows, tail_first, tail_blocks):
        dil, t = dils[c_t // 3], tpr[c_t // 3]
        out_shape.append(jax.ShapeDtypeStruct((b, dil, tail_keep[c_t], d), F32))
        out_specs.append(pl.BlockSpec((None, None, rows, d), parked(
            c_t // 3, lambda bi, i, t=t, first=first: (bi, i // t, jnp.maximum(i % t - first, 0), 0),
            (b - 1, dil - 1, nb - 1, 0))))
    n_groups = n_chunks // 3
    w_spec = pl.BlockSpec((d, 3 * d), lambda c, bi, i: (0, jnp.minimum(c, n_groups - 1)))
    return dict(
        kwargs=dict(tile=tile, grouped=grouped, tail_chunks=tail_chunks, tail_first=tail_first, tail_rows=tail_rows,
                    tiles_per_res=tpr),
        grid=(n_groups, b, nt),
        in_specs=in_specs + [w_spec] + cos_specs + sin_specs,
        out_specs=tuple(out_specs),
        out_shape=tuple(out_shape),
        args=(*hs, w, *[t[0] for t in tables], *[t[1] for t in tables]),
    )


def _qkv_proj(hs, w, tables, tail_keep, grouped, max_tile=ROW_TILE):
    plan = _qkv_plan(hs, w, tables, tail_keep, grouped, max_tile)
    return pl.pallas_call(
        functools.partial(_qkv_kernel, **plan["kwargs"]),
        out_shape=plan["out_shape"],
        grid=plan["grid"],
        in_specs=plan["in_specs"],
        out_specs=plan["out_specs"],
        compiler_params=_cparams("arbitrary", "arbitrary", "arbitrary"),
        name="qkv_proj",
    )(*plan["args"])


def _attn_prompt_kernel(q_ref, kp_ref, kc_ref, vp_ref, vc_ref, o_ref, lse_ref, *, n_sub):
    i = pl.program_id(2)
    tq = ATT_BAND
    lane = _iota((tq, LANES), 1)
    low = lane < HEAD_DIM
    qi = _iota((2 * tq, 2 * tq), 0) & (tq - 1)
    kj = _iota((2 * tq, 2 * tq), 1)
    in_prev = (kj < tq) & (kj >= qi)
    in_cur = (kj >= tq) & ((kj - tq) <= qi)
    zero = jnp.zeros((), q_ref.dtype)
    for sub in range(n_sub):
        rows = slice(sub * tq, (sub + 1) * tq)
        before = slice((sub - 1) * tq, sub * tq)
        mask = ((in_prev & (i > 0)) if sub == 0 else in_prev) | in_cur
        lse_all = jnp.zeros((tq, LANES), F32)
        for hp in range(N_HEADS // 2):
            sl = slice(hp * LANES, (hp + 1) * LANES)
            q = q_ref[rows, sl]
            q2 = jnp.concatenate([jnp.where(low, q, zero), jnp.where(low, zero, q)], axis=0)
            k_prev = kp_ref[:, sl] if sub == 0 else kc_ref[before, sl]
            v_prev = vp_ref[:, sl] if sub == 0 else vc_ref[before, sl]
            kw = jnp.concatenate([k_prev, kc_ref[rows, sl]], axis=0)
            vw = jnp.concatenate([v_prev, vc_ref[rows, sl]], axis=0)
            s = lax.dot_general(q2, kw, _NT, preferred_element_type=F32)
            s = jnp.where(mask, s, -jnp.inf)
            mx = jnp.max(s, axis=-1, keepdims=True)
            p = jnp.exp(s - mx)
            den = jnp.sum(p, axis=-1, keepdims=True)
            o2 = jnp.dot(p.astype(vw.dtype), vw, preferred_element_type=F32) * (1.0 / den)
            o_ref[rows, sl] = jnp.where(low, o2[:tq], o2[tq:]).astype(o_ref.dtype)
            lse2 = mx + jnp.log(den)
            lse_all = jnp.where(lane == 2 * hp, lse2[:tq], jnp.where(lane == 2 * hp + 1, lse2[tq:], lse_all))
        lse_ref[rows, :] = lse_all


def _attn_prompt(qkv_g, grp, n_sub=4):
    _, b, dil, sub, d = qkv_g.shape
    tq = ATT_BAND
    n_sub = min(n_sub, sub // tq)
    rows = n_sub * tq
    cur = lambda which: pl.BlockSpec((None, None, None, rows, d), lambda bi, r, i: (which, bi, r, i, 0))
    prev = lambda which: pl.BlockSpec((None, None, None, tq, d),
                                      lambda bi, r, i: (which, bi, r, jnp.maximum(i * n_sub - 1, 0), 0))
    return pl.pallas_call(
        functools.partial(_attn_prompt_kernel, n_sub=n_sub),
        out_shape=(jax.ShapeDtypeStruct((b, dil, sub, d), BF16), jax.ShapeDtypeStruct((b, dil, sub, LANES), F32)),
        grid=(b, dil, sub // rows),
        in_specs=[cur(0), prev(1), cur(1), prev(2), cur(2)],
        out_specs=(
            pl.BlockSpec((None, None, rows, d), lambda bi, r, i: (bi, r, i, 0)),
            pl.BlockSpec((None, None, rows, LANES), lambda bi, r, i: (bi, r, i, 0)),
        ),
        compiler_params=_cparams("arbitrary", "arbitrary", "arbitrary"),
        name=f"attn_prompt_g{grp}",
    )(qkv_g, qkv_g, qkv_g, qkv_g, qkv_g)


def _attn_sample_kernel(qkn_ref, k0_ref, v0_ref, k1_ref, v1_ref, k2_ref, v2_ref, o_ref, *, n_new):
    rows = k0_ref.shape[0]
    hpb = rows // HEAD_DIM
    nq = n_new * hpb
    shift = hpb.bit_length() - 1
    scale = HEAD_DIM ** -0.5
    neg = F32(-jnp.inf)
    own = (_iota((nq, rows), 1) >> 6) == (_iota((nq, rows), 0) & (hpb - 1))
    n_pad = 2 * SUBLANES
    pad = jnp.zeros((n_pad - n_new, rows), F32)
    scores, values = [], []
    for grp, ((_, dil), k_ref, v_ref) in enumerate(zip(DIL_GROUPS, (k0_ref, k1_ref, k2_ref), (v0_ref, v1_ref, v2_ref))):
        q = qkn_ref[0, grp * n_new:(grp + 1) * n_new, :]
        q_rows = jnp.concatenate([jnp.broadcast_to(q[t:t + 1, :], (hpb, rows)) for t in range(n_new)], axis=0)
        qbd = jnp.where(own, q_rows, 0.0).astype(BF16)
        n_buf = k_ref.shape[1]
        s = jnp.dot(qbd, k_ref[...].astype(BF16), preferred_element_type=F32) * scale
        pos = _iota((nq, n_buf), 1)
        t_row = _iota((nq, n_buf), 0) >> shift
        ok = (pos >= t_row) if dil == 1 else ((pos & (dil - 1)) == t_row)
        scores.append(jnp.where(ok, s, neg))
        values.append((v_ref[...].astype(BF16), _NT))
        k_new = jnp.concatenate([qkn_ref[1, grp * n_new:(grp + 1) * n_new, :], pad], axis=0).astype(BF16)
        v_new = jnp.concatenate([qkn_ref[2, grp * n_new:(grp + 1) * n_new, :], pad], axis=0).astype(BF16)
        s_new = lax.dot_general(qbd, k_new, _NT, preferred_element_type=F32) * scale
        t2 = _iota((nq, n_pad), 1)
        tq = _iota((nq, n_pad), 0) >> shift
        ok_new = (t2 <= tq) & (((tq - t2) & (dil - 1)) == 0)
        scores.append(jnp.where(ok_new, s_new, neg))
        values.append((v_new, (((1,), (0,)), ((), ()))))
    mx = functools.reduce(jnp.maximum, [jnp.max(s, axis=-1, keepdims=True) for s in scores])
    den = jnp.zeros((nq, 1), F32)
    acc = jnp.zeros((nq, rows), F32)
    for s, (v, dims) in zip(scores, values):
        p = jnp.exp(s - mx)
        den = den + jnp.sum(p, axis=-1, keepdims=True)
        acc = acc + lax.dot_general(p.astype(BF16), v, dims, preferred_element_type=F32)
    acc = jnp.where(own, acc * (1.0 / den), 0.0)
    out = jnp.zeros((SUBLANES, rows), F32)
    out_r = _iota((SUBLANES, rows), 0)
    for t in range(n_new):
        out = jnp.where(out_r == t, jnp.sum(acc[t * hpb:(t + 1) * hpb, :], axis=0, keepdims=True), out)
    o_ref[...] = out[0:n_new, :]


def _fused_attn_qkv_kernel(*refs, n_attn_in, n_qkv_in, n_new, steps_inner, per_group, tiles_per_batch, qkv_kwargs):
    attn_in = refs[:n_attn_in]
    qkv_in = refs[n_attn_in:n_attn_in + n_qkv_in]
    o_ref = refs[n_attn_in + n_qkv_in]
    qkv_out = refs[n_attn_in + n_qkv_in + 1:]
    _attn_sample_kernel(*attn_in, o_ref, n_new=n_new)
    step = pl.program_id(0) * steps_inner + pl.program_id(1)
    grp = step // per_group
    tile_i = (step - grp * per_group) & (tiles_per_batch - 1)
    _qkv_kernel(*qkv_in, *qkv_out, ids=(grp, tile_i), **qkv_kwargs)


def _attn_sample_with_qkv(qkn, caches_t, n_new, plan, rows=512):
    b, d = qkn.shape[0], qkn.shape[-1]
    steps_inner = d // rows
    n_groups, b_qkv, nt = plan["grid"]
    per_group = b_qkv * nt
    assert n_groups * per_group <= b * steps_inner and nt & (nt - 1) == 0
    assert n_new <= SUBLANES and (rows // HEAD_DIM) & (rows // HEAD_DIM - 1) == 0

    def ids(bi, hh):
        step = bi * steps_inner + hh
        grp = step // per_group
        t = step - grp * per_group
        return grp, t // nt, t % nt

    remap = lambda spec: pl.BlockSpec(spec.block_shape, lambda bi, hh, m=spec.index_map: m(*ids(bi, hh)))
    cache_args, cache_specs = [], []
    for (win, dil), (ck, cv) in zip(DIL_GROUPS, caches_t):
        assert ck.shape == (b, d, win) and win // dil == ATT_BAND and (dil == 1 or dil >= n_new)
        for c in (ck, cv):
            cache_args.append(c)
            cache_specs.append(pl.BlockSpec((None, rows, win), lambda bi, hh: (bi, hh, 0)))
    attn_specs = [pl.BlockSpec((None, 3, qkn.shape[2], rows), lambda bi, hh: (bi, 0, 0, hh))] + cache_specs
    kern = functools.partial(
        _fused_attn_qkv_kernel, n_attn_in=len(attn_specs), n_qkv_in=len(plan["in_specs"]), n_new=n_new,
        steps_inner=steps_inner, per_group=per_group, tiles_per_batch=nt, qkv_kwargs=plan["kwargs"])
    outs = pl.pallas_call(
        kern,
        out_shape=(jax.ShapeDtypeStruct((b, n_new, d), F32),) + plan["out_shape"],
        grid=(b, steps_inner),
        in_specs=attn_specs + [remap(sp) for sp in plan["in_specs"]],
        out_specs=(pl.BlockSpec((None, n_new, rows), lambda bi, hh: (bi, 0, hh)),) + tuple(remap(sp) for sp in plan["out_specs"]),
        compiler_params=_cparams("arbitrary", "arbitrary"),
        name="attn_sample_qkv",
    )(qkn, *cache_args, *plan["args"])
    return outs[0], outs[1:]


def _oproj_kernel(*refs, dils, n_lse, alpha, tile):
    n_groups = len(dils)
    o_refs = refs[:n_groups]
    lse_refs = refs[n_groups:n_groups + n_lse]
    k = n_groups + n_lse
    x_ref, m_ref, wo_ref, hx_ref, lng_ref, lnb_ref, wr_ref, br_ref = refs[k:k + 8]
    x1_ref, hrow_ref, ext_ref = refs[k + 8:k + 11]
    obuf, lbuf = refs[k + 11:]

    def natural(ref, buf, g, dil):
        if dil == 1:
            return ref[0].astype(F32)
        n_blk = ref.shape[-1] // LANES
        for r in range(dil):
            rows = ref[r].astype(F32)
            for j in range(n_blk):
                buf[g * n_blk + j, pl.ds(r, tile // dil, stride=dil), :] = rows[:, j * LANES:(j + 1) * LANES]
        return jnp.concatenate([buf[g * n_blk + j] for j in range(n_blk)], axis=-1)

    os_ = [natural(o_refs[g], obuf, g, dils[g]) for g in range(n_groups)]
    if n_lse == 0:
        o = os_[0].astype(BF16)
    else:
        lses = [natural(lse_refs[g], lbuf, g, dils[g]) for g in range(n_groups)]
        mx = functools.reduce(jnp.maximum, lses)
        es = [jnp.exp(l - mx) for l in lses]
        inv = 1.0 / functools.reduce(lambda a, b: a + b, es)
        o = jnp.zeros(x_ref.shape, F32)
        hx = hx_ref[...]
        for e, og in zip(es, os_):
            wgt = e * inv
            hi = wgt.astype(BF16)
            lo = (wgt - hi.astype(F32)).astype(BF16)
            wexp = jnp.dot(hi, hx, preferred_element_type=F32) + jnp.dot(lo, hx, preferred_element_type=F32)
            o = o + wexp * og
        o = o.astype(BF16)
    y = jnp.dot(o, wo_ref[...], preferred_element_type=F32)
    x1, h2, ext = _post_mixer(x_ref[...], y, m_ref[...], lng_ref, lnb_ref, wr_ref, br_ref, alpha)
    x1_ref[...] = x1
    hrow_ref[:, 0:D_MODEL] = h2
    hrow_ref[:, D_MODEL:] = ext
    ext_ref[...] = ext


def _out_proj(os_, lses, x, m, w_o, head_expand, ln_g, ln_b, wr, br, alpha, per_token):
    b, s, d = x.shape
    tile = min(ROW_TILE, s)
    nt = s // tile
    dils = tuple(o.shape[1] for o in os_)
    tok = pl.BlockSpec((None, tile, d), lambda bi, i: (bi, i, 0))
    grouped = lambda dil, w: pl.BlockSpec((None, dil, tile // dil, w), lambda bi, i: (bi, 0, i, 0))
    full = lambda bi, i: (0, 0)
    return pl.pallas_call(
        functools.partial(_oproj_kernel, dils=dils, n_lse=len(lses), alpha=alpha, tile=tile),
        out_shape=(
            jax.ShapeDtypeStruct((b, s, d), F32),
            jax.ShapeDtypeStruct((b * s, ROW_W), F32),
            jax.ShapeDtypeStruct((b * s, EXT), F32),
        ),
        grid=(b, nt),
        in_specs=[grouped(dil, d) for dil in dils] + [grouped(dil, LANES) for dil in dils[:len(lses)]] + [
            tok, _mod_spec(tile, d, per_token),
            pl.BlockSpec((d, d), full),
            pl.BlockSpec((LANES, d), full),
            pl.BlockSpec((1, d), full),
            pl.BlockSpec((1, d), full),
            pl.BlockSpec((2, d, EXT), lambda *_: (0, 0, 0)),
            pl.BlockSpec((1, EXT), full),
        ],
        out_specs=(
            tok,
            pl.BlockSpec((tile, ROW_W), lambda bi, i: (bi * nt + i, 0)),
            pl.BlockSpec((tile, EXT), lambda bi, i: (bi * nt + i, 0)),
        ),
        scratch_shapes=[pltpu.VMEM((len(dils) * d // LANES, tile, LANES), F32), pltpu.VMEM((len(dils), tile, LANES), F32)],
        compiler_params=_cparams("arbitrary", "arbitrary"),
        name="out_proj",
    )(*os_, *lses, x, m, w_o, head_expand, ln_g, ln_b, wr, br)


def _router_matrix(w_group, b_group, w_er, b_er):
    d = w_group.shape[0]
    n_e = N_EXPERT_GROUPS * EXPERTS_PER_GROUP
    w_e = jnp.transpose(w_er, (1, 0, 2)).reshape(d, n_e)
    pad = EXT - N_EXPERT_GROUPS - n_e
    wr = jnp.concatenate([w_group, w_e, jnp.zeros((d, pad), F32)], axis=1)
    br = jnp.concatenate([b_group, b_er.reshape(n_e), jnp.zeros((pad,), F32)])[None, :]
    wr_hi = wr.astype(BF16)
    wr_lo = (wr - wr_hi.astype(F32)).astype(BF16)
    return jnp.stack([wr_hi, wr_lo]), br


def _moe_experts(rows_p, rows_s, ext_p, ext_s, w1, w3, w2, layer):
    dest, tile_group, n_used, pad_start, n_tiles = _group_sort(ext_p, ext_s)
    sorted_rows = _scatter_rows(dest, pad_start, rows_p, rows_s, n_tiles * MOE_TILE)
    return dest, _moe_ffn(tile_group, n_used, sorted_rows, w1, w3, w2, n_tiles, layer)


def kernel(x_prompt, x_sample, state_pool, cache_k_w128, cache_v_w128, cache_k_w512, cache_v_w512, cache_k_w2048, cache_v_w2048, c_prompt, c_sample, mod_w, mod_b, ln1_g, ln1_b, ln2_g, ln2_b, pool_w, pool_scale, attn_w_qkv, attn_w_o, moe_w_group, moe_b_group, moe_w_expert_router, moe_b_expert_router, moe_w1, moe_w3, moe_w2):
    depth = mod_w.shape[0]
    alpha = float((2.0 * depth) ** 0.25)
    bp, s, d = x_prompt.shape
    bs, t_new, _ = x_sample.shape
    n_p, n_s = bp * s, bs * t_new
    kv_caches = ((cache_k_w128, cache_v_w128), (cache_k_w512, cache_v_w512), (cache_k_w2048, cache_v_w2048))

    c_all = jnp.concatenate([c_sample, c_prompt, jnp.zeros((SUBLANES - bp % SUBLANES, d), F32)], axis=0)
    m_all = _modulation(c_all, mod_w, mod_b)
    m_dec = [m_all[i, :bs] for i in range(depth)]
    m_tok = [jnp.repeat(m, t_new, axis=0).reshape(1, n_s, 6 * d) for m in m_dec]
    m_seq = [m_all[i, bs:bs + bp].reshape(bp, 1, 6 * d) for i in range(depth)]

    head_expand = jnp.asarray(np.arange(LANES)[:, None] == (np.arange(d)[None, :] // HEAD_DIM), BF16)
    dils = tuple(dil for _, dil in DIL_GROUPS)
    tables_p = [_rope_tables(jnp.arange(s // dil, dtype=jnp.int32)[None, :] * dil
                             + jnp.arange(dil, dtype=jnp.int32)[:, None]) for dil in dils]
    tables_s = [_rope_tables(PAST_LEN + (jnp.arange(n_s, dtype=jnp.int32) % t_new)[None, :])] * N_DIL

    w1_bf, w3_bf, w2_bf = moe_w1.astype(BF16), moe_w3.astype(BF16), moe_w2.astype(BF16)
    xp, xs = x_prompt, x_sample.reshape(1, n_s, d)
    hp_groups = hs = None
    pool_p, pool_s = [], []
    kv_p = [[] for _ in range(2 * N_DIL)]
    kv_s = [[] for _ in range(2 * N_DIL)]
    for i in range(depth):
        li = i // 2
        wr, br = _router_matrix(moe_w_group[i], moe_b_group[i], moe_w_expert_router[i], moe_b_expert_router[i])
        ln1g, ln1b = ln1_g[i][None, :], ln1_b[i][None, :]
        if i % 2 == 0:
            pw = pool_w[li].astype(BF16)
            ps = pool_scale[li][None, :]
            xp, rows_p, ext_p, tail = _pool_prompt(xp, m_seq[i], pw, ps, ln1g, ln1b, wr, br, alpha)
            x1s, rows_s, ext_s, znew = _pool_sample(
                state_pool[li].reshape(bs, POOL_BUF * d), xs.reshape(bs, t_new * d),
                m_dec[i], pw, ps, ln1g, ln1b, wr, br, alpha)
            xs = x1s.reshape(1, n_s, d)
            rows_s = rows_s.reshape(n_s, ROW_W)
            ext_s = ext_s.reshape(n_s, EXT)
            pool_p.append(tail)
            pool_s.append(znew.reshape(bs, t_new, d))
        else:
            wqkv = attn_w_qkv[li].astype(BF16)
            wo = attn_w_o[li].astype(BF16)
            kv_chunks = [3 * g + j for g in range(N_DIL) for j in (1, 2)]
            keep_p = {3 * g + j: min(win, s) // dil for g, (win, dil) in enumerate(DIL_GROUPS) for j in (1, 2)}
            rows_all = _qkv_proj([hs[:, None]] * N_DIL, wqkv, tables_s, {c: n_s for c in range(3 * N_DIL)},
                                 grouped=False, max_tile=ROW_TILE // 2)
            kinds = [jnp.concatenate([rows_all[3 * g + j].reshape(bs, t_new, d) for g in range(N_DIL)], axis=1)
                     for j in range(3)]
            qkn = jnp.pad(jnp.stack(kinds, axis=1), ((0, 0), (0, 0), (0, 2 * SUBLANES - N_DIL * t_new), (0, 0)))
            caches_t = [tuple(jnp.transpose(c[li], (0, 2, 3, 1)).reshape(bs, d, c.shape[2]) for c in kv)
                        for kv in kv_caches]
            plan = _qkv_plan(hp_groups, wqkv, tables_p, keep_p, grouped=True, max_tile=ROW_TILE // 2)
            o_s, qkv_outs = _attn_sample_with_qkv(qkn, caches_t, t_new, plan)
            qkv_groups, tails_p = qkv_outs[:N_DIL], qkv_outs[N_DIL:]
            for j, c in enumerate(kv_chunks):
                tail = jnp.transpose(tails_p[j], (0, 2, 1, 3))
                kv_p[j].append(tail.reshape(bp, -1, N_HEADS, HEAD_DIM))
                kv_s[j].append(rows_all[c].reshape(bs, t_new, N_HEADS, HEAD_DIM))
            outs = [_attn_prompt(qkv_groups[g], g) for g in range(N_DIL)]
            xp, rows_p, ext_p = _out_proj([o for o, _ in outs], [l for _, l in outs], xp, m_seq[i], wo, head_expand,
                                          ln1g, ln1b, wr, br, alpha, per_token=False)
            xs, rows_s, ext_s = _out_proj([o_s.reshape(1, 1, n_s, d)], [], xs, m_tok[i], wo, head_expand,
                                          ln1g, ln1b, wr, br, alpha, per_token=True)
        dest, f_sorted = _moe_experts(rows_p, rows_s, ext_p, ext_s, w1_bf, w3_bf, w2_bf, i)
        nxt = min(i + 1, depth - 1)
        ln2g, ln2b = ln2_g[i][None, :], ln2_b[i][None, :]
        regroup = tuple(dil for dil in dils if dil > 1) if (i + 1 < depth and (i + 1) % 2 == 1) else ()
        xp, hp, *hp_dilated = _ln2(dest, xp, f_sorted, 0, m_seq[i], m_seq[nxt], ln2g, ln2b, alpha, per_token=False,
                                   dils=regroup)
        if regroup:
            hp_dilated = iter(hp_dilated)
            hp_groups = [hp[:, None] if dil == 1 else next(hp_dilated) for dil in dils]
        xs, hs = _ln2(dest, xs, f_sorted, n_p, m_tok[i], m_tok[nxt], ln2g, ln2b, alpha, per_token=True)

    stack = lambda lst: jnp.stack(lst, axis=0)
    return (xp, xs.reshape(bs, t_new, d), stack(pool_p), stack(pool_s),
            *[stack(kv_p[j]) for j in range(2 * N_DIL)], *[stack(kv_s[j]) for j in range(2 * N_DIL)])
```
